```python
import math
import jax, jax.numpy as jnp
from jax import lax
import numpy as np

D_MODEL = 4096
BATCH = 4
SEQ = 4096
DEPTH = 1

HEAD_DIM = 128
MOBA_HEADS = 16
FOX_HEADS = 16
MOBA_WIDTH = MOBA_HEADS * HEAD_DIM
FOX_WIDTH = FOX_HEADS * HEAD_DIM
MOBA_BLOCK = 256
MOBA_TOPK = 3
MOBA_Q_CHUNK = 16
FOX_Q_BLOCK = 128
T5_NUM_BUCKETS = 32
T5_MAX_DISTANCE = 128
LN_EPS = 1e-5
FORGET_BIAS_INIT = 3.0
DEEPNORM_ALPHA = (2.0 * DEPTH) ** 0.25
DEEPNORM_BETA = (8.0 * DEPTH) ** -0.25
IN_SIZES = [MOBA_WIDTH] * 4 + [FOX_WIDTH] * 4 + [FOX_HEADS, 2 * D_MODEL]
IN_COLS = int(sum(IN_SIZES))
IN_SPLITS = [int(v) for v in np.cumsum(IN_SIZES)[:-1]]

kernel_name = "moba_fox_gated_hybrid_deepnorm"


def t5_bucket(dist):
    max_exact = T5_NUM_BUCKETS // 2
    d = jnp.maximum(dist, 1).astype(jnp.float32)
    large = max_exact + (jnp.log(d / max_exact) / math.log(T5_MAX_DISTANCE / max_exact)
                         * (T5_NUM_BUCKETS - max_exact)).astype(jnp.int32)
    large = jnp.minimum(large, T5_NUM_BUCKETS - 1)
    return jnp.where(dist < max_exact, dist, large)


def moba_attention(q, k, v, rel_bias_table):
    B, H, S, Dh = q.shape
    nb = -(-S // MOBA_BLOCK)
    pad = nb * MOBA_BLOCK - S
    kp = jnp.pad(k, ((0, 0), (0, 0), (0, pad), (0, 0)))
    vp = jnp.pad(v, ((0, 0), (0, 0), (0, pad), (0, 0)))
    kb = kp.reshape(B, H, nb, MOBA_BLOCK, Dh)
    vb = vp.reshape(B, H, nb, MOBA_BLOCK, Dh)
    k_mean = jnp.mean(kb.astype(jnp.float32), axis=3)
    bias_hb = rel_bias_table.T.astype(jnp.float32)
    scale = HEAD_DIM ** -0.5
    topk = min(MOBA_TOPK, nb)
    C = MOBA_Q_CHUNK
    n_chunks = S // C
    qc = q.reshape(B, H, n_chunks, C, Dh).transpose(2, 0, 1, 3, 4)
    b_idx = jnp.arange(B)[:, None, None, None]
    h_idx = jnp.arange(H)[None, :, None, None]
    blk_ar = jnp.arange(MOBA_BLOCK, dtype=jnp.int32)

    def chunk(args):
        ci, qi = args
        q_pos = ci * C + jnp.arange(C, dtype=jnp.int32)
        own = (ci * C) // MOBA_BLOCK
        gate = jnp.einsum("bhcd,bhnd->bhcn", qi.astype(jnp.float32), k_mean)
        past = jnp.arange(nb)[None, :] < own
        gate = jnp.where(past[None, None], gate, -1e30)
        _, sel = lax.top_k(gate, topk)
        slot_valid = jnp.arange(topk) < own
        sel = jnp.where(slot_valid, sel, 0)
        k_sel = kb[b_idx, h_idx, sel].reshape(B, H, C, topk * MOBA_BLOCK, Dh)
        v_sel = vb[b_idx, h_idx, sel].reshape(B, H, C, topk * MOBA_BLOCK, Dh)
        s_sel = jnp.einsum("bhcd,bhckd->bhck", qi, k_sel).astype(jnp.float32) * scale
        key_pos_sel = (sel[..., None] * MOBA_BLOCK + blk_ar).reshape(B, H, C, topk * MOBA_BLOCK)
        dist_sel = jnp.maximum(q_pos[:, None] - key_pos_sel, 0)
        s_sel = s_sel + bias_hb[h_idx, t5_bucket(dist_sel)]
        valid_sel = jnp.repeat(slot_valid, MOBA_BLOCK)
        s_sel = jnp.where(valid_sel, s_sel, -jnp.inf)
        k_own = lax.dynamic_index_in_dim(kb, own, axis=2, keepdims=False)
        v_own = lax.dynamic_index_in_dim(vb, own, axis=2, keepdims=False)
        s_own = jnp.einsum("bhcd,bhkd->bhck", qi, k_own).astype(jnp.float32) * scale
        dist_own = q_pos[:, None] - (own * MOBA_BLOCK + blk_ar)[None, :]
        s_own = s_own + bias_hb[:, t5_bucket(jnp.maximum(dist_own, 0))]
        s_own = jnp.where(dist_own >= 0, s_own, -jnp.inf)
        p = jax.nn.softmax(jnp.concatenate([s_sel, s_own], axis=-1), axis=-1).astype(v.dtype)
        p_sel, p_own = p[..., :topk * MOBA_BLOCK], p[..., topk * MOBA_BLOCK:]
        return (jnp.einsum("bhck,bhckd->bhcd", p_sel, v_sel)
                + jnp.einsum("bhck,bhkd->bhcd", p_own, v_own))

    out = lax.map(chunk, (jnp.arange(n_chunks, dtype=jnp.int32), qc))
    return out.transpose(1, 2, 0, 3, 4).reshape(B, H, S, Dh)


def forgetting_attention(q, k, v, log_f):
    B, H, S, Dh = q.shape
    c = jnp.cumsum(log_f, axis=-1)
    nqb = S // FOX_Q_BLOCK
    qb = q.reshape(B, H, nqb, FOX_Q_BLOCK, Dh).transpose(2, 0, 1, 3, 4)
    cb = c.reshape(B, H, nqb, FOX_Q_BLOCK).transpose(2, 0, 1, 3)
    key_pos = jnp.arange(S, dtype=jnp.int32)
    scale = HEAD_DIM ** -0.5

    def block(args):
        bi, qi, ci = args
        q_pos = bi * FOX_Q_BLOCK + jnp.arange(FOX_Q_BLOCK, dtype=jnp.int32)
        s = jnp.einsum("bhqd,bhkd->bhqk", qi, k).astype(jnp.float32) * scale
        s = s + ci[..., None] - c[:, :, None, :]
        s = jnp.where(key_pos[None, :] <= q_pos[:, None], s, -jnp.inf)
        p = jax.nn.softmax(s, axis=-1).astype(v.dtype)
        return jnp.einsum("bhqk,bhkd->bhqd", p, v)

    out = lax.map(block, (jnp.arange(nqb, dtype=jnp.int32), qb, cb))
    return out.transpose(1, 2, 0, 3, 4).reshape(B, H, S, Dh)


def layer_norm(h, gain, bias):
    hf = h.astype(jnp.float32)
    mu = jnp.mean(hf, axis=-1, keepdims=True)
    var = jnp.mean(jnp.square(hf - mu), axis=-1, keepdims=True)
    return ((hf - mu) * lax.rsqrt(var + LN_EPS) * gain + bias).astype(h.dtype)


def setup_inputs(seed: int = 0) -> dict:
    key = jax.random.key(seed)
    ks = jax.random.split(key, 10)
    x = jax.random.normal(ks[0], (BATCH, SEQ, D_MODEL), jnp.float32)
    col_scale = jnp.concatenate([
        jnp.ones((2 * MOBA_WIDTH,)), jnp.full((MOBA_WIDTH,), DEEPNORM_BETA), jnp.ones((MOBA_WIDTH,)),
        jnp.ones((2 * FOX_WIDTH,)), jnp.full((FOX_WIDTH,), DEEPNORM_BETA), jnp.ones((FOX_WIDTH,)),
        jnp.ones((FOX_HEADS + 2 * D_MODEL,))]).astype(jnp.float32)
    w_in = jax.random.normal(ks[1], (DEPTH, D_MODEL, IN_COLS), jnp.float32) * D_MODEL ** -0.5 * col_scale
    b_forget = FORGET_BIAS_INIT + 0.1 * jax.random.normal(ks[2], (DEPTH, FOX_HEADS), jnp.float32)
    b_gate = 0.1 * jax.random.normal(ks[3], (DEPTH, 2, D_MODEL), jnp.float32)
    rel_bias_table = 0.5 * jax.random.normal(ks[4], (T5_NUM_BUCKETS, MOBA_HEADS), jnp.float32)
    w_branch = (jax.random.normal(ks[5], (DEPTH, 2, MOBA_WIDTH, D_MODEL), jnp.float32)
                * MOBA_WIDTH ** -0.5 * DEEPNORM_BETA)
    w_out = jax.random.normal(ks[6], (DEPTH, D_MODEL, D_MODEL), jnp.float32) * D_MODEL ** -0.5 * DEEPNORM_BETA
    ln_gain = 1.0 + 0.1 * jax.random.normal(ks[7], (DEPTH, D_MODEL), jnp.float32)
    ln_bias = 0.1 * jax.random.normal(ks[8], (DEPTH, D_MODEL), jnp.float32)
    return {"x": x, "w_in": w_in, "b_forget": b_forget, "b_gate": b_gate,
            "rel_bias_table": rel_bias_table, "w_branch": w_branch, "w_out": w_out,
            "ln_gain": ln_gain, "ln_bias": ln_bias}


def reference(x, w_in, b_forget, b_gate, rel_bias_table, w_branch, w_out, ln_gain, ln_bias):
    B, S, D = x.shape

    def heads(t, H):
        return t.reshape(B, S, H, HEAD_DIM).transpose(0, 2, 1, 3)

    def merge_heads(t):
        return t.transpose(0, 2, 1, 3).reshape(B, S, -1)

    for layer in range(DEPTH):
        proj = jnp.einsum("bsd,de->bse", x, w_in[layer])
        qa, ka, va, za, qf, kf, vf, zf, f_logit, g_logit = jnp.split(proj, IN_SPLITS, axis=-1)
        ya = merge_heads(moba_attention(heads(qa, MOBA_HEADS), heads(ka, MOBA_HEADS),
                                        heads(va, MOBA_HEADS), rel_bias_table))
        ya = ya * jax.nn.silu(za)
        log_f = jax.nn.log_sigmoid(f_logit.astype(jnp.float32) + b_forget[layer]).transpose(0, 2, 1)
        yf = merge_heads(forgetting_attention(heads(qf, FOX_HEADS), heads(kf, FOX_HEADS),
                                              heads(vf, FOX_HEADS), log_f))
        yf = yf * jax.nn.silu(zf)
        ua = jnp.einsum("bsw,wd->bsd", ya, w_branch[layer, 0])
        uf = jnp.einsum("bsw,wd->bsd", yf, w_branch[layer, 1])
        gates = jax.nn.sigmoid(g_logit.reshape(B, S, 2, D) + b_gate[layer])
        merged = gates[:, :, 0] * ua + gates[:, :, 1] * uf
        out = jnp.einsum("bsd,de->bse", merged, w_out[layer])
        x = layer_norm(DEEPNORM_ALPHA * x + out, ln_gain[layer], ln_bias[layer])
    return x
```

```python
import functools
import math

import jax
import jax.numpy as jnp
import numpy as np
from jax import lax
from jax.experimental import pallas as pl
from jax.experimental.pallas import tpu as pltpu

HEAD_DIM = 128
MOBA_HEADS = 16
FOX_HEADS = 16
MOBA_BLOCK = 256
MOBA_TOPK = 3
T5_NUM_BUCKETS = 32
T5_MAX_DISTANCE = 128
LN_EPS = 1e-5
DEPTH = 1
DEEPNORM_ALPHA = (2.0 * DEPTH) ** 0.25

V7X_VMEM_LIMIT_BYTES = 56 * 1024 * 1024
NEG_BIG = -1e30

_NT_DIMS = (((1,), (1,)), ((), ()))


def _params(*semantics):
    return pltpu.CompilerParams(dimension_semantics=semantics,
                                vmem_limit_bytes=V7X_VMEM_LIMIT_BYTES)


def _matmul_kernel(x_ref, w_ref, s_ref, b_ref, o_ref, *, act):
    acc = jnp.dot(x_ref[...], w_ref[...], preferred_element_type=jnp.float32)
    acc = acc * s_ref[...] + b_ref[...]
    if act == "silu":
        acc = acc * jax.nn.sigmoid(acc)
    elif act == "sigmoid":
        acc = jax.nn.sigmoid(acc)
    o_ref[...] = acc.astype(o_ref.dtype)


def _matmul(x, w, col_scale, col_bias, out_dtype, act, tm=1024, tn=1024):
    M, K = x.shape
    N = w.shape[1]
    tm, tn = min(tm, M), min(tn, N)
    return pl.pallas_call(
        functools.partial(_matmul_kernel, act=act),
        grid=(M // tm, N // tn),
        in_specs=[pl.BlockSpec((tm, K), lambda i, j: (i, 0)),
                  pl.BlockSpec((K, tn), lambda i, j: (0, j)),
                  pl.BlockSpec((1, tn), lambda i, j: (0, j)),
                  pl.BlockSpec((1, tn), lambda i, j: (0, j))],
        out_specs=pl.BlockSpec((tm, tn), lambda i, j: (i, j)),
        out_shape=jax.ShapeDtypeStruct((M, N), out_dtype),
        compiler_params=_params("parallel", "parallel"),
        name="proj_" + act,
    )(x, w, col_scale, col_bias)


def _split3_bf16(a):
    hi = a.astype(jnp.bfloat16)
    r = a - hi.astype(jnp.float32)
    mid = r.astype(jnp.bfloat16)
    lo = (r - mid.astype(jnp.float32)).astype(jnp.bfloat16)
    return hi, mid, lo


def _forget_cumsum_kernel(x_ref, wt_ref, b_ref, c_ref, carry_ref):
    @pl.when(pl.program_id(1) == 0)
    def _():
        carry_ref[...] = jnp.zeros_like(carry_ref)

    ts = x_ref.shape[1]
    z = lax.dot_general(wt_ref[...], x_ref[0], _NT_DIMS, preferred_element_type=jnp.float32)
    z = z + b_ref[...]
    log_f = jnp.minimum(z, 0.0) - jnp.log1p(jnp.exp(-jnp.abs(z)))
    row = lax.broadcasted_iota(jnp.int32, (ts, ts), 0)
    col = lax.broadcasted_iota(jnp.int32, (ts, ts), 1)
    upper = jnp.where(row <= col, 1.0, 0.0).astype(jnp.bfloat16)
    hi, mid, lo = _split3_bf16(log_f)
    c = (jnp.dot(hi, upper, preferred_element_type=jnp.float32)
         + jnp.dot(mid, upper, preferred_element_type=jnp.float32)
         + jnp.dot(lo, upper, preferred_element_type=jnp.float32))
    c = c + carry_ref[...]
    c_ref[0] = c
    carry_ref[...] = c[:, ts - 1:ts]


def _forget_cumsum(x, w_f_t, b_forget, ts=512):
    B, S, D = x.shape
    H = w_f_t.shape[0]
    ts = min(ts, S)
    return pl.pallas_call(
        _forget_cumsum_kernel,
        grid=(B, S // ts),
        in_specs=[pl.BlockSpec((1, ts, D), lambda b, i: (b, i, 0)),
                  pl.BlockSpec((H, D), lambda b, i: (0, 0)),
                  pl.BlockSpec((H, 1), lambda b, i: (0, 0))],
        out_specs=pl.BlockSpec((1, H, ts), lambda b, i: (b, 0, i)),
        out_shape=jax.ShapeDtypeStruct((B, H, S), jnp.float32),
        scratch_shapes=[pltpu.VMEM((H, 1), jnp.float32)],
        compiler_params=_params("parallel", "arbitrary"),
        name="forget_cumsum",
    )(x, w_f_t, b_forget)


def _online_softmax_step(s, v, m_ref, l_ref, acc_ref):
    m_prev = m_ref[...]
    m_new = jnp.maximum(m_prev, jnp.max(s, axis=-1, keepdims=True))
    p = jnp.exp(s - m_new)
    alpha = jnp.exp(m_prev - m_new)
    l_ref[...] = alpha * l_ref[...] + jnp.sum(p, axis=-1, keepdims=True)
    acc_ref[...] = alpha * acc_ref[...] + jnp.dot(p.astype(v.dtype), v, preferred_element_type=jnp.float32)
    m_ref[...] = m_new


def _init_softmax_state(m_ref, l_ref, acc_ref):
    m_ref[...] = jnp.full_like(m_ref, NEG_BIG)
    l_ref[...] = jnp.zeros_like(l_ref)
    acc_ref[...] = jnp.zeros_like(acc_ref)


def _fox_kernel(q_ref, k_ref, v_ref, c_ref, z_ref, o_ref, m_ref, l_ref, acc_ref):
    qi = pl.program_id(2)
    tq = q_ref.shape[1]
    q = q_ref[0]
    _init_softmax_state(m_ref, l_ref, acc_ref)

    def logits(j):
        k = k_ref[0, pl.ds(j * tq, tq), :]
        s = lax.dot_general(q, k, _NT_DIMS, preferred_element_type=jnp.float32)
        return s - c_ref[0, :, pl.ds(j * tq, tq)]

    def past_tile(j, carry):
        _online_softmax_step(logits(j), v_ref[0, pl.ds(j * tq, tq), :], m_ref, l_ref, acc_ref)
        return carry

    lax.fori_loop(0, qi, past_tile, 0)

    row = lax.broadcasted_iota(jnp.int32, (tq, tq), 0)
    col = lax.broadcasted_iota(jnp.int32, (tq, tq), 1)
    s = jnp.where(col <= row, logits(qi), NEG_BIG)
    _online_softmax_step(s, v_ref[0, pl.ds(qi * tq, tq), :], m_ref, l_ref, acc_ref)

    o_ref[0] = (acc_ref[...] / l_ref[...] * z_ref[0]).astype(o_ref.dtype)


def _fox_attention(qkv, c, sz, H, col0, zcol0, tq=256):
    B, S, _ = qkv.shape
    tq = min(tq, S)
    Dh = HEAD_DIM
    return pl.pallas_call(
        _fox_kernel,
        grid=(B, H, S // tq),
        in_specs=[pl.BlockSpec((1, tq, Dh), lambda b, h, i: (b, i, col0 + h)),
                  pl.BlockSpec((1, S, Dh), lambda b, h, i: (b, 0, col0 + H + h)),
                  pl.BlockSpec((1, S, Dh), lambda b, h, i: (b, 0, col0 + 2 * H + h)),
                  pl.BlockSpec((1, 1, S), lambda b, h, i: (b * H + h, 0, 0)),
                  pl.BlockSpec((1, tq, Dh), lambda b, h, i: (b, i, zcol0 + h))],
        out_specs=pl.BlockSpec((1, tq, Dh), lambda b, h, i: (b, i, h)),
        out_shape=jax.ShapeDtypeStruct((B, S, H * Dh), jnp.bfloat16),
        scratch_shapes=[pltpu.VMEM((tq, 1), jnp.float32), pltpu.VMEM((tq, 1), jnp.float32),
                        pltpu.VMEM((tq, Dh), jnp.float32)],
        compiler_params=_params("parallel", "parallel", "arbitrary"),
        name="fox_attention",
    )(qkv, qkv, qkv, c, sz)


def _t5_bucket_np(dist):
    max_exact = T5_NUM_BUCKETS // 2
    d = np.maximum(dist, 1).astype(np.float32)
    ratio = (np.log(d / np.float32(max_exact)) / np.float32(math.log(T5_MAX_DISTANCE / max_exact))
             * np.float32(T5_NUM_BUCKETS - max_exact))
    large = np.minimum(max_exact + ratio.astype(np.int32), T5_NUM_BUCKETS - 1)
    return np.where(dist < max_exact, dist, large).astype(np.int32)


def _moba_kernel(q_ref, k_ref, v_ref, t5_ref, far_ref, z_ref, o_ref,
                 kmean_ref, m_ref, l_ref, acc_ref, *, nb):
    own = pl.program_id(2)
    blk = q_ref.shape[1]
    nbp = kmean_ref.shape[0]
    q = q_ref[0]

    @pl.when(own == 0)
    def _():
        kmean_ref[...] = jnp.zeros_like(kmean_ref)
        for n in range(nb):
            kb = k_ref[0, n * blk:(n + 1) * blk, :].astype(jnp.float32)
            kmean_ref[n:n + 1, :] = jnp.sum(kb, axis=0, keepdims=True) * (1.0 / blk)

    km = kmean_ref[...]
    km_hi = km.astype(jnp.bfloat16)
    km_lo = (km - km_hi.astype(jnp.float32)).astype(jnp.bfloat16)
    gate = (lax.dot_general(km_hi, q, _NT_DIMS, preferred_element_type=jnp.float32)
            + lax.dot_general(km_lo, q, _NT_DIMS, preferred_element_type=jnp.float32))
    n_idx = lax.broadcasted_iota(jnp.int32, (nbp, blk), 0)
    past = n_idx < own
    gate = jnp.where(past, gate, NEG_BIG)
    beaten = jnp.zeros((nbp, blk), jnp.int32)
    for n in range(nb):
        g_n = gate[n:n + 1, :]
        wins = (g_n > gate) | ((g_n == gate) & (n < n_idx))
        beaten = beaten + jnp.where(wins & (n < own), 1, 0)
    selected = past & (beaten < MOBA_TOPK)
    penalty_t = jnp.where(selected, 0.0, NEG_BIG)
    if nbp < 128:
        penalty_t = jnp.concatenate([penalty_t, jnp.full((128 - nbp, blk), NEG_BIG, jnp.float32)], axis=0)
    penalty = penalty_t.T.astype(jnp.bfloat16)

    def block_logits(n):
        k = k_ref[0, pl.ds(n * blk, blk), :]
        onehot = jnp.where(lax.broadcasted_iota(jnp.int32, (128, blk), 0) == n, 1.0, 0.0).astype(jnp.bfloat16)
        return (lax.dot_general(q, k, _NT_DIMS, preferred_element_type=jnp.float32)
                + jnp.dot(penalty, onehot, preferred_element_type=jnp.float32))

    _init_softmax_state(m_ref, l_ref, acc_ref)

    row = lax.broadcasted_iota(jnp.int32, (blk, blk), 0)
    col = lax.broadcasted_iota(jnp.int32, (blk, blk), 1)
    k_own = k_ref[0, pl.ds(own * blk, blk), :]
    s = lax.dot_general(q, k_own, _NT_DIMS, preferred_element_type=jnp.float32) + t5_ref[0, :, blk:]
    s = jnp.where(col <= row, s, NEG_BIG)
    _online_softmax_step(s, v_ref[0, pl.ds(own * blk, blk), :], m_ref, l_ref, acc_ref)

    @pl.when(own >= 1)
    def _():
        n = own - 1
        s = block_logits(n) + t5_ref[0, :, :blk]
        _online_softmax_step(s, v_ref[0, pl.ds(n * blk, blk), :], m_ref, l_ref, acc_ref)

    far_bias = far_ref[0]

    def far_block(n, carry):
        s = block_logits(n) + far_bias
        _online_softmax_step(s, v_ref[0, pl.ds(n * blk, blk), :], m_ref, l_ref, acc_ref)
        return carry

    lax.fori_loop(0, own - 1, far_block, 0)

    o_ref[0] = (acc_ref[...] / l_ref[...] * z_ref[0]).astype(o_ref.dtype)


def _moba_attention(qkv, rel_bias_table, sz, H, col0, zcol0):
    B, S, _ = qkv.shape
    blk = MOBA_BLOCK
    assert S % blk == 0
    nb = S // blk
    nbp = -(-nb // 8) * 8
    assert nbp <= 128
    Dh = HEAD_DIM
    t = np.arange(blk)[:, None]
    u = np.arange(2 * blk)[None, :]
    bucket = _t5_bucket_np(np.maximum(t + blk - u, 0))
    table_t = rel_bias_table.T.astype(jnp.float32)
    t5 = table_t[:, bucket]
    far_bucket = _t5_bucket_np(np.full((1, blk), blk + 1))
    assert (_t5_bucket_np(np.arange(blk + 1, 4 * blk)) == far_bucket[0, 0]).all()
    far = table_t[:, far_bucket]
    return pl.pallas_call(
        functools.partial(_moba_kernel, nb=nb),
        grid=(B, H, nb),
        in_specs=[pl.BlockSpec((1, blk, Dh), lambda b, h, i: (b, i, col0 + h)),
                  pl.BlockSpec((1, S, Dh), lambda b, h, i: (b, 0, col0 + H + h)),
                  pl.BlockSpec((1, S, Dh), lambda b, h, i: (b, 0, col0 + 2 * H + h)),
                  pl.BlockSpec((1, blk, 2 * blk), lambda b, h, i: (h, 0, 0)),
                  pl.BlockSpec((1, 1, blk), lambda b, h, i: (h, 0, 0)),
                  pl.BlockSpec((1, blk, Dh), lambda b, h, i: (b, i, zcol0 + h))],
        out_specs=pl.BlockSpec((1, blk, Dh), lambda b, h, i: (b, i, h)),
        out_shape=jax.ShapeDtypeStruct((B, S, H * Dh), jnp.bfloat16),
        scratch_shapes=[pltpu.VMEM((nbp, Dh), jnp.float32),
                        pltpu.VMEM((blk, 1), jnp.float32), pltpu.VMEM((blk, 1), jnp.float32),
                        pltpu.VMEM((blk, Dh), jnp.float32)],
        compiler_params=_params("parallel", "parallel", "arbitrary"),
        name="moba_attention",
    )(qkv, qkv, qkv, t5, far, sz)


def _branch_merge_kernel(ya_ref, yf_ref, wa_ref, wf_ref, ga_ref, gf_ref, o_ref):
    ua = jnp.dot(ya_ref[...], wa_ref[...], preferred_element_type=jnp.float32)
    uf = jnp.dot(yf_ref[...], wf_ref[...], preferred_element_type=jnp.float32)
    o_ref[...] = (ga_ref[...] * ua + gf_ref[...] * uf).astype(o_ref.dtype)


def _branch_merge(ya, yf, wa, wf, gates, tm=512, tn=1024):
    M, W = ya.shape
    D = wa.shape[1]
    tm, tn = min(tm, M), min(tn, D)
    nj = D // tn
    return pl.pallas_call(
        _branch_merge_kernel,
        grid=(M // tm, nj),
        in_specs=[pl.BlockSpec((tm, W), lambda i, j: (i, 0)),
                  pl.BlockSpec((tm, W), lambda i, j: (i, 0)),
                  pl.BlockSpec((W, tn), lambda i, j: (0, j)),
                  pl.BlockSpec((W, tn), lambda i, j: (0, j)),
                  pl.BlockSpec((tm, tn), lambda i, j: (i, j)),
                  pl.BlockSpec((tm, tn), lambda i, j: (i, nj + j))],
        out_specs=pl.BlockSpec((tm, tn), lambda i, j: (i, j)),
        out_shape=jax.ShapeDtypeStruct((M, D), jnp.bfloat16),
        compiler_params=_params("parallel", "parallel"),
        name="branch_merge",
    )(ya, yf, wa, wf, gates, gates)


def _out_layernorm_kernel(h_ref, w_ref, x_ref, g_ref, b_ref, o_ref):
    j = pl.program_id(1)
    tn = w_ref.shape[1]
    out = jnp.dot(h_ref[...], w_ref[...], preferred_element_type=jnp.float32)
    o_ref[:, pl.ds(pl.multiple_of(j * tn, tn), tn)] = DEEPNORM_ALPHA * x_ref[...] + out

    @pl.when(j == pl.num_programs(1) - 1)
    def _():
        rows = min(LN_ROW_CHUNK, o_ref.shape[0])

        def normalize_rows(i, carry):
            sl = pl.ds(pl.multiple_of(i * rows, rows), rows)
            r = o_ref[sl, :]
            mu = jnp.mean(r, axis=-1, keepdims=True)
            d = r - mu
            var = jnp.mean(d * d, axis=-1, keepdims=True)
            o_ref[sl, :] = d * lax.rsqrt(var + LN_EPS) * g_ref[...] + b_ref[...]
            return carry

        lax.fori_loop(0, o_ref.shape[0] // rows, normalize_rows, 0)


LN_ROW_CHUNK = 64


def _out_layernorm(h, w, x, gain, bias, tm=512, tn=1024):
    M, D = x.shape
    tm, tn = min(tm, M), min(tn, D)
    return pl.pallas_call(
        _out_layernorm_kernel,
        grid=(M // tm, D // tn),
        in_specs=[pl.BlockSpec((tm, D), lambda i, j: (i, 0)),
                  pl.BlockSpec((D, tn), lambda i, j: (0, j)),
                  pl.BlockSpec((tm, tn), lambda i, j: (i, j)),
                  pl.BlockSpec((1, D), lambda i, j: (0, 0)),
                  pl.BlockSpec((1, D), lambda i, j: (0, 0))],
        out_specs=pl.BlockSpec((tm, D), lambda i, j: (i, 0)),
        out_shape=jax.ShapeDtypeStruct((M, D), jnp.float32),
        compiler_params=_params("parallel", "arbitrary"),
        name="out_layernorm",
    )(h, w, x, gain, bias)


def kernel(x, w_in, b_forget, b_gate, rel_bias_table, w_branch, w_out, ln_gain, ln_bias):
    B, S, D = x.shape
    Ha, Hf = MOBA_HEADS, FOX_HEADS
    Wa, Wf = Ha * HEAD_DIM, Hf * HEAD_DIM
    bf16, f32 = jnp.bfloat16, jnp.float32
    depth = w_in.shape[0]
    for layer in range(depth):
        w = w_in[layer]
        o = np.cumsum([0, Wa, Wa, Wa, Wa, Wf, Wf, Wf, Wf, Hf, 2 * D])
        w_qkv = jnp.concatenate([w[:, o[0]:o[3]], w[:, o[4]:o[7]]], axis=1).astype(bf16)
        w_z = jnp.concatenate([w[:, o[3]:o[4]], w[:, o[7]:o[8]]], axis=1).astype(bf16)
        w_f_t = w[:, o[8]:o[9]].T.astype(bf16)
        w_g = w[:, o[9]:o[10]].astype(bf16)
        xb = x.reshape(B * S, D).astype(bf16)

        scale = HEAD_DIM ** -0.5
        qkv_scale = jnp.concatenate([jnp.full((Wa,), scale, f32), jnp.ones((2 * Wa,), f32),
                                     jnp.full((Wf,), scale, f32), jnp.ones((2 * Wf,), f32)])[None, :]
        qkv = _matmul(xb, w_qkv, qkv_scale, jnp.zeros_like(qkv_scale), bf16, "none").reshape(B, S, -1)
        ones_z = jnp.ones((1, Wa + Wf), f32)
        sz = _matmul(xb, w_z, ones_z, jnp.zeros_like(ones_z), f32, "silu").reshape(B, S, -1)
        gates = _matmul(xb, w_g, jnp.ones((1, 2 * D), f32), b_gate[layer].reshape(1, 2 * D).astype(f32),
                        f32, "sigmoid")

        c = _forget_cumsum(xb.reshape(B, S, D), w_f_t, b_forget[layer].reshape(Hf, 1).astype(f32))
        ya = _moba_attention(qkv, rel_bias_table, sz, Ha, 0, 0)
        yf = _fox_attention(qkv, c.reshape(B * Hf, 1, S), sz, Hf, 3 * Ha, Ha)

        merged = _branch_merge(ya.reshape(B * S, Wa), yf.reshape(B * S, Wf),
                               w_branch[layer, 0].astype(bf16), w_branch[layer, 1].astype(bf16), gates)
        y = _out_layernorm(merged, w_out[layer].astype(bf16), x.reshape(B * S, D).astype(f32),
                           ln_gain[layer].reshape(1, D).astype(f32), ln_bias[layer].reshape(1, D).astype(f32))
        x = y.reshape(B, S, D).astype(x.dtype)
    return x
```

```python
import functools
import math

import jax
import jax.numpy as jnp
import numpy as np
from jax import lax
from jax.experimental import pallas as pl
from jax.experimental.pallas import tpu as pltpu

HEAD_DIM = 128
LANES = 128
MOBA_HEADS = 16
FOX_HEADS = 16
MOBA_BLOCK = 256
MOBA_TOPK = 3
T5_NUM_BUCKETS = 32
T5_MAX_DISTANCE = 128
LN_EPS = 1e-5
DEPTH = 1
DEEPNORM_ALPHA = (2.0 * DEPTH) ** 0.25
LOG2E = math.log2(math.e)

V7X_VMEM_LIMIT_BYTES = 56 * 1024 * 1024
NEG_BIG = -1e30

_NT_DIMS = (((1,), (1,)), ((), ()))


def _params(*semantics):
    return pltpu.CompilerParams(dimension_semantics=semantics,
                                vmem_limit_bytes=V7X_VMEM_LIMIT_BYTES)


def _split3_bf16(a):
    hi = a.astype(jnp.bfloat16).astype(jnp.float32)
    r = a - hi
    mid = r.astype(jnp.bfloat16).astype(jnp.float32)
    lo = (r - mid).astype(jnp.bfloat16).astype(jnp.float32)
    return hi, mid, lo


def _matmul_kernel(x_ref, w_ref, s_ref, b_ref, o_ref, *, act):
    acc = jnp.dot(x_ref[...], w_ref[...], preferred_element_type=jnp.float32)
    acc = acc * s_ref[...] + b_ref[...]
    if act == "silu":
        acc = acc * jax.nn.sigmoid(acc)
    elif act == "sigmoid":
        acc = jax.nn.sigmoid(acc)
    o_ref[...] = acc.astype(o_ref.dtype)


def _matmul(x, w, col_scale, col_bias, out_dtype, act, tm=1024, tn=1024):
    M, K = x.shape
    N = w.shape[1]
    tm, tn = min(tm, M), min(tn, N)
    return pl.pallas_call(
        functools.partial(_matmul_kernel, act=act),
        grid=(M // tm, N // tn),
        in_specs=[pl.BlockSpec((tm, K), lambda i, j: (i, 0)),
                  pl.BlockSpec((K, tn), lambda i, j: (0, j)),
                  pl.BlockSpec((1, tn), lambda i, j: (0, j)),
                  pl.BlockSpec((1, tn), lambda i, j: (0, j))],
        out_specs=pl.BlockSpec((tm, tn), lambda i, j: (i, j)),
        out_shape=jax.ShapeDtypeStruct((M, N), out_dtype),
        compiler_params=_params("parallel", "parallel"),
        name="proj_" + act,
    )(x, w, col_scale, col_bias)


def _forget_bias_kernel(x_ref, w_ref, b_ref, o_ref, carry_ref, *, heads):
    @pl.when(pl.program_id(1) == 0)
    def _():
        carry_ref[...] = jnp.zeros_like(carry_ref)

    ts = x_ref.shape[1]
    z = jnp.dot(x_ref[0], w_ref[...], preferred_element_type=jnp.float32) + b_ref[...]
    log_f = jnp.minimum(z, 0.0) - jnp.log1p(jnp.exp(-jnp.abs(z)))
    row = lax.broadcasted_iota(jnp.int32, (ts, ts), 0)
    col = lax.broadcasted_iota(jnp.int32, (ts, ts), 1)
    lower = jnp.where(col <= row, 1.0, 0.0).astype(jnp.bfloat16)
    c = carry_ref[...]
    for part in _split3_bf16(log_f):
        c = c + jnp.dot(lower, part.astype(jnp.bfloat16), preferred_element_type=jnp.float32)
    carry_ref[...] = c[ts - 1:ts, :]
    hi, mid, lo = _split3_bf16(c * (-LOG2E))
    lane = lax.broadcasted_iota(jnp.int32, (ts, LANES), 1)
    cols = jnp.where(lane < heads, hi,
                     jnp.where(lane < 2 * heads, pltpu.roll(mid, heads, 1),
                               jnp.where(lane < 3 * heads, pltpu.roll(lo, 2 * heads, 1), 0.0)))
    o_ref[0] = cols.astype(o_ref.dtype)


def _forget_bias_columns(x, w_f, b_forget, heads, ts=512):
    B, S, D = x.shape
    assert 3 * heads <= LANES
    ts = min(ts, S)
    return pl.pallas_call(
        functools.partial(_forget_bias_kernel, heads=heads),
        grid=(B, S // ts),
        in_specs=[pl.BlockSpec((1, ts, D), lambda b, i: (b, i, 0)),
                  pl.BlockSpec((D, LANES), lambda b, i: (0, 0)),
                  pl.BlockSpec((1, LANES), lambda b, i: (0, 0))],
        out_specs=pl.BlockSpec((1, ts, LANES), lambda b, i: (b, i, 0)),
        out_shape=jax.ShapeDtypeStruct((B, S, LANES), jnp.bfloat16),
        scratch_shapes=[pltpu.VMEM((1, LANES), jnp.float32)],
        compiler_params=_params("parallel", "arbitrary"),
        name="forget_bias_columns",
    )(x, w_f, b_forget)


def _softmax_tile(s, v_ext, m_ref, acc_ref):
    tk = s.shape[1]
    chunks = [s[:, c:c + LANES] for c in range(0, tk, LANES)]
    mx = functools.reduce(jnp.maximum, chunks)
    m_prev = m_ref[...]
    m_new = jnp.maximum(m_prev, jnp.max(mx, axis=-1, keepdims=True))
    alpha = jnp.exp2(m_prev - m_new)
    p = jnp.concatenate([jnp.exp2(ch - m_new) for ch in chunks], axis=1).astype(v_ext.dtype)
    pv = jnp.dot(p, v_ext, preferred_element_type=jnp.float32)
    acc_ref[...] = acc_ref[...] * jnp.concatenate([alpha, alpha], axis=1) + pv
    m_ref[...] = m_new


def _pipelined_tiles(first_qk, first_post, first_values, n_rest, qk, values, sa_ref, sb_ref, m_ref, acc_ref):
    sa_ref[...] = first_qk()
    sb_ref[...] = qk(0)
    _softmax_tile(first_post(sa_ref[...]), first_values, m_ref, acc_ref)

    def two_tiles(i, carry):
        sa_ref[...] = qk(2 * i + 1)
        _softmax_tile(sb_ref[...], values(2 * i), m_ref, acc_ref)
        sb_ref[...] = qk(jnp.minimum(2 * i + 2, n_rest - 1))
        _softmax_tile(sa_ref[...], values(2 * i + 1), m_ref, acc_ref)
        return carry

    lax.fori_loop(0, n_rest // 2, two_tiles, 0)

    @pl.when(n_rest % 2 == 1)
    def _():
        _softmax_tile(sb_ref[...], values(n_rest - 1), m_ref, acc_ref)


def _init_softmax_state(m_ref, acc_ref):
    m_ref[...] = jnp.full_like(m_ref, NEG_BIG)
    acc_ref[...] = jnp.zeros_like(acc_ref)


def _finish_attention(acc_ref, z_ref, o_ref):
    dh = o_ref.shape[2]
    o_ref[0] = (acc_ref[:, :dh] / acc_ref[:, dh:] * z_ref[0]).astype(o_ref.dtype)


def _fill_value_scratch(v_ref, vext_ref):
    dh = v_ref.shape[2]
    vext_ref[:, :dh] = v_ref[0]
    vext_ref[:, dh:] = jnp.ones((vext_ref.shape[0], vext_ref.shape[1] - dh), vext_ref.dtype)


def _fox_kernel(q_ref, k_ref, cx_ref, v_ref, z_ref, o_ref, kaug_ref, vext_ref, sa_ref, sb_ref, m_ref, acc_ref,
                *, heads):
    h = pl.program_id(1)
    qi = pl.program_id(2)
    tq = q_ref.shape[1]
    dh = q_ref.shape[2]

    @pl.when(qi == 0)
    def _():
        kaug_ref[:, :dh] = k_ref[0]
        kaug_ref[:, dh:] = cx_ref[0]
        _fill_value_scratch(v_ref, vext_ref)

    lane = lax.broadcasted_iota(jnp.int32, (tq, LANES), 1)
    pick = (lane == h) | (lane == heads + h) | (lane == 2 * heads + h)
    q_aug = jnp.concatenate([q_ref[0], jnp.where(pick, 1.0, 0.0).astype(q_ref.dtype)], axis=1)
    _init_softmax_state(m_ref, acc_ref)

    def rows(j):
        return pl.ds(pl.multiple_of(j * tq, tq), tq)

    def logits(j):
        return lax.dot_general(q_aug, kaug_ref[rows(j), :], _NT_DIMS, preferred_element_type=jnp.float32)

    def causal(s):
        row = lax.broadcasted_iota(jnp.int32, (tq, tq), 0)
        col = lax.broadcasted_iota(jnp.int32, (tq, tq), 1)
        return jnp.where(col <= row, s, NEG_BIG)

    _pipelined_tiles(lambda: logits(qi), causal, vext_ref[rows(qi), :], qi,
                     logits, lambda j: vext_ref[rows(j), :], sa_ref, sb_ref, m_ref, acc_ref)
    _finish_attention(acc_ref, z_ref, o_ref)


def _fox_attention(qkv, cx, sz, H, col0, zcol0, tq=512):
    B, S, _ = qkv.shape
    tq = min(tq, S)
    Dh = HEAD_DIM
    return pl.pallas_call(
        functools.partial(_fox_kernel, heads=H),
        grid=(B, H, S // tq),
        in_specs=[pl.BlockSpec((1, tq, Dh), lambda b, h, i: (b, i, col0 + h)),
                  pl.BlockSpec((1, S, Dh), lambda b, h, i: (b, 0, col0 + H + h)),
                  pl.BlockSpec((1, S, LANES), lambda b, h, i: (b, 0, 0)),
                  pl.BlockSpec((1, S, Dh), lambda b, h, i: (b, 0, col0 + 2 * H + h)),
                  pl.BlockSpec((1, tq, Dh), lambda b, h, i: (b, i, zcol0 + h))],
        out_specs=pl.BlockSpec((1, tq, Dh), lambda b, h, i: (b, i, h)),
        out_shape=jax.ShapeDtypeStruct((B, S, H * Dh), jnp.bfloat16),
        scratch_shapes=[pltpu.VMEM((S, Dh + LANES), jnp.bfloat16), pltpu.VMEM((S, 2 * Dh), jnp.bfloat16),
                        pltpu.VMEM((tq, tq), jnp.float32), pltpu.VMEM((tq, tq), jnp.float32),
                        pltpu.VMEM((tq, LANES), jnp.float32), pltpu.VMEM((tq, 2 * Dh), jnp.float32)],
        compiler_params=_params("parallel", "parallel", "arbitrary"),
        name="fox_attention",
    )(qkv, qkv, cx, qkv, sz)


def _t5_bucket_np(dist):
    max_exact = T5_NUM_BUCKETS // 2
    d = np.maximum(dist, 1).astype(np.float32)
    ratio = (np.log(d / np.float32(max_exact)) / np.float32(math.log(T5_MAX_DISTANCE / max_exact))
             * np.float32(T5_NUM_BUCKETS - max_exact))
    large = np.minimum(max_exact + ratio.astype(np.int32), T5_NUM_BUCKETS - 1)
    return np.where(dist < max_exact, dist, large).astype(np.int32)


def _moba_kernel(q_ref, k_ref, v_ref, rbias_ref, far_ref, z_ref, o_ref,
                 kaug_ref, vext_ref, kmean_ref, t5_ref, sa_ref, sb_ref, m_ref, acc_ref, *, nb):
    own = pl.program_id(2)
    blk = q_ref.shape[1]
    dh = q_ref.shape[2]
    nbp = kmean_ref.shape[0]
    S = k_ref.shape[1]
    q = q_ref[0]

    @pl.when(own == 0)
    def _():
        kaug_ref[:, :dh] = k_ref[0]
        key_block = lax.broadcasted_iota(jnp.int32, (S, LANES), 0) // blk
        lane = lax.broadcasted_iota(jnp.int32, (S, LANES), 1)
        kaug_ref[:, dh:] = jnp.where((lane == key_block) | (lane == key_block + nb), 1.0, 0.0).astype(kaug_ref.dtype)
        _fill_value_scratch(v_ref, vext_ref)
        kmean_ref[...] = jnp.zeros_like(kmean_ref)
        for n in range(nb):
            kb = k_ref[0, n * blk:(n + 1) * blk, :].astype(jnp.float32)
            kmean_ref[n:n + 1, :] = jnp.sum(kb, axis=0, keepdims=True) * (1.0 / blk)
        rb = jnp.broadcast_to(rbias_ref[0], (blk, 2 * blk))
        t5_ref[...] = pltpu.roll(rb, blk + 1, 1, stride=1, stride_axis=0)

    km = kmean_ref[...]
    km_hi = km.astype(jnp.bfloat16)
    km_lo = (km - km_hi.astype(jnp.float32)).astype(jnp.bfloat16)
    gate = (lax.dot_general(km_hi, q, _NT_DIMS, preferred_element_type=jnp.float32)
            + lax.dot_general(km_lo, q, _NT_DIMS, preferred_element_type=jnp.float32))
    n_idx = lax.broadcasted_iota(jnp.int32, (nbp, blk), 0)
    past = n_idx < own
    gate = jnp.where(past, gate, NEG_BIG)
    beaten = jnp.zeros((nbp, blk), jnp.int32)
    for n in range(nb):
        g_n = gate[n:n + 1, :]
        wins = (g_n > gate) | ((g_n == gate) & (n < n_idx))
        beaten = beaten + jnp.where(wins & (n < own), 1, 0)
    selected = past & (beaten < MOBA_TOPK)
    far_bias = jnp.broadcast_to(far_ref[0], (nbp, blk))
    term = jnp.where(selected, jnp.where(n_idx == own - 1, 0.0, far_bias), NEG_BIG)
    term = jnp.where(n_idx == own, 0.0, term)
    term_hi = term.astype(jnp.bfloat16).astype(jnp.float32)
    term_lo = jnp.where(selected, term - term_hi, 0.0)
    pieces = [term_hi[:nb], term_lo[:nb]]
    if 2 * nb < LANES:
        pieces.append(jnp.zeros((LANES - 2 * nb, blk), jnp.float32))
    q_extra = jnp.concatenate(pieces, axis=0).T.astype(q.dtype)
    q_aug = jnp.concatenate([q, q_extra], axis=1)

    def logits(start, size):
        k = kaug_ref[pl.ds(pl.multiple_of(start, blk), size), :]
        return lax.dot_general(q_aug, k, _NT_DIMS, preferred_element_type=jnp.float32)

    def values(start, size):
        return vext_ref[pl.ds(pl.multiple_of(start, blk), size), :]

    _init_softmax_state(m_ref, acc_ref)

    @pl.when(own == 0)
    def _():
        row = lax.broadcasted_iota(jnp.int32, (blk, blk), 0)
        col = lax.broadcasted_iota(jnp.int32, (blk, blk), 1)
        s = jnp.where(col <= row, logits(0, blk) + t5_ref[:, blk:], NEG_BIG)
        _softmax_tile(s, values(0, blk), m_ref, acc_ref)

    @pl.when(own >= 1)
    def _():
        lone = (own + 1) % 2
        n_pairs = (own - 1 - lone) // 2
        near_start = (own - 1) * blk

        def near(s):
            row = lax.broadcasted_iota(jnp.int32, (blk, 2 * blk), 0)
            col = lax.broadcasted_iota(jnp.int32, (blk, 2 * blk), 1)
            return jnp.where(col <= row + blk, s + t5_ref[...], NEG_BIG)

        def pair_start(f):
            return jnp.minimum(lone + 2 * f, nb - 2) * blk

        _pipelined_tiles(lambda: logits(near_start, 2 * blk), near, values(near_start, 2 * blk), n_pairs,
                         lambda f: logits(pair_start(f), 2 * blk), lambda f: values(pair_start(f), 2 * blk),
                         sa_ref, sb_ref, m_ref, acc_ref)

        @pl.when((lone == 1) & (own >= 2))
        def _():
            _softmax_tile(logits(0, blk), values(0, blk), m_ref, acc_ref)

    _finish_attention(acc_ref, z_ref, o_ref)


def _moba_attention(qkv, rel_bias_table, sz, H, col0, zcol0):
    B, S, _ = qkv.shape
    blk = MOBA_BLOCK
    assert S % blk == 0
    nb = S // blk
    nbp = -(-nb // 8) * 8
    assert 3 <= nb and 2 * nb <= LANES
    Dh = HEAD_DIM
    table_t = rel_bias_table.T.astype(jnp.float32) * LOG2E
    rbias = table_t[:, _t5_bucket_np(np.arange(2 * blk)[::-1])][:, None, :]
    far_bucket = int(_t5_bucket_np(np.array([blk + 1]))[0])
    assert (_t5_bucket_np(np.arange(blk + 1, S + blk)) == far_bucket).all()
    far = jnp.broadcast_to(table_t[:, far_bucket][:, None, None], (H, 1, blk))
    return pl.pallas_call(
        functools.partial(_moba_kernel, nb=nb),
        grid=(B, H, nb),
        in_specs=[pl.BlockSpec((1, blk, Dh), lambda b, h, i: (b, i, col0 + h)),
                  pl.BlockSpec((1, S, Dh), lambda b, h, i: (b, 0, col0 + H + h)),
                  pl.BlockSpec((1, S, Dh), lambda b, h, i: (b, 0, col0 + 2 * H + h)),
                  pl.BlockSpec((1, 1, 2 * blk), lambda b, h, i: (h, 0, 0)),
                  pl.BlockSpec((1, 1, blk), lambda b, h, i: (h, 0, 0)),
                  pl.BlockSpec((1, blk, Dh), lambda b, h, i: (b, i, zcol0 + h))],
        out_specs=pl.BlockSpec((1, blk, Dh), lambda b, h, i: (b, i, h)),
        out_shape=jax.ShapeDtypeStruct((B, S, H * Dh), jnp.bfloat16),
        scratch_shapes=[pltpu.VMEM((S, Dh + LANES), jnp.bfloat16), pltpu.VMEM((S, 2 * Dh), jnp.bfloat16),
                        pltpu.VMEM((nbp, Dh), jnp.float32), pltpu.VMEM((blk, 2 * blk), jnp.float32),
                        pltpu.VMEM((blk, 2 * blk), jnp.float32), pltpu.VMEM((blk, 2 * blk), jnp.float32),
                        pltpu.VMEM((blk, LANES), jnp.float32), pltpu.VMEM((blk, 2 * Dh), jnp.float32)],
        compiler_params=_params("parallel", "parallel", "arbitrary"),
        name="moba_attention",
    )(qkv, qkv, qkv, rbias, far, sz)


def _branch_merge_kernel(ya_ref, yf_ref, wa_ref, wf_ref, ga_ref, gf_ref, o_ref):
    ua = jnp.dot(ya_ref[...], wa_ref[...], preferred_element_type=jnp.float32)
    uf = jnp.dot(yf_ref[...], wf_ref[...], preferred_element_type=jnp.float32)
    o_ref[...] = (ga_ref[...] * ua + gf_ref[...] * uf).astype(o_ref.dtype)


def _branch_merge(ya, yf, wa, wf, gates, tm=512, tn=1024):
    M, W = ya.shape
    D = wa.shape[1]
    tm, tn = min(tm, M), min(tn, D)
    nj = D // tn
    return pl.pallas_call(
        _branch_merge_kernel,
        grid=(M // tm, nj),
        in_specs=[pl.BlockSpec((tm, W), lambda i, j: (i, 0)),
                  pl.BlockSpec((tm, W), lambda i, j: (i, 0)),
                  pl.BlockSpec((W, tn), lambda i, j: (0, j)),
                  pl.BlockSpec((W, tn), lambda i, j: (0, j)),
                  pl.BlockSpec((tm, tn), lambda i, j: (i, j)),
                  pl.BlockSpec((tm, tn), lambda i, j: (i, nj + j))],
        out_specs=pl.BlockSpec((tm, tn), lambda i, j: (i, j)),
        out_shape=jax.ShapeDtypeStruct((M, D), jnp.bfloat16),
        compiler_params=_params("parallel", "parallel"),
        name="branch_merge",
    )(ya, yf, wa, wf, gates, gates)


LN_ROW_CHUNK = 64


def _out_layernorm_kernel(h_ref, w_ref, x_ref, g_ref, b_ref, o_ref):
    j = pl.program_id(1)
    tn = w_ref.shape[1]
    out = jnp.dot(h_ref[...], w_ref[...], preferred_element_type=jnp.float32)
    o_ref[:, pl.ds(pl.multiple_of(j * tn, tn), tn)] = DEEPNORM_ALPHA * x_ref[...] + out

    @pl.when(j == pl.num_programs(1) - 1)
    def _():
        rows = min(LN_ROW_CHUNK, o_ref.shape[0])

        def normalize_rows(i, carry):
            sl = pl.ds(pl.multiple_of(i * rows, rows), rows)
            r = o_ref[sl, :]
            mu = jnp.mean(r, axis=-1, keepdims=True)
            d = r - mu
            var = jnp.mean(d * d, axis=-1, keepdims=True)
            o_ref[sl, :] = d * lax.rsqrt(var + LN_EPS) * g_ref[...] + b_ref[...]
            return carry

        lax.fori_loop(0, o_ref.shape[0] // rows, normalize_rows, 0)


def _out_layernorm(h, w, x, gain, bias, tm=512, tn=1024):
    M, D = x.shape
    tm, tn = min(tm, M), min(tn, D)
    return pl.pallas_call(
        _out_layernorm_kernel,
        grid=(M // tm, D // tn),
        in_specs=[pl.BlockSpec((tm, D), lambda i, j: (i, 0)),
                  pl.BlockSpec((D, tn), lambda i, j: (0, j)),
                  pl.BlockSpec((tm, tn), lambda i, j: (i, j)),
                  pl.BlockSpec((1, D), lambda i, j: (0, 0)),
                  pl.BlockSpec((1, D), lambda i, j: (0, 0))],
        out_specs=pl.BlockSpec((tm, D), lambda i, j: (i, 0)),
        out_shape=jax.ShapeDtypeStruct((M, D), jnp.float32),
        compiler_params=_params("parallel", "arbitrary"),
        name="out_layernorm",
    )(h, w, x, gain, bias)


def kernel(x, w_in, b_forget, b_gate, rel_bias_table, w_branch, w_out, ln_gain, ln_bias):
    B, S, D = x.shape
    Ha, Hf = MOBA_HEADS, FOX_HEADS
    Wa, Wf = Ha * HEAD_DIM, Hf * HEAD_DIM
    bf16, f32 = jnp.bfloat16, jnp.float32
    depth = w_in.shape[0]
    for layer in range(depth):
        w = w_in[layer]
        o = np.cumsum([0, Wa, Wa, Wa, Wa, Wf, Wf, Wf, Wf, Hf, 2 * D])
        w_qkv = jnp.concatenate([w[:, o[0]:o[3]], w[:, o[4]:o[7]]], axis=1).astype(bf16)
        w_z = jnp.concatenate([w[:, o[3]:o[4]], w[:, o[7]:o[8]]], axis=1).astype(bf16)
        w_f = jnp.pad(w[:, o[8]:o[9]], ((0, 0), (0, LANES - Hf))).astype(bf16)
        b_f = jnp.pad(b_forget[layer].astype(f32), (0, LANES - Hf))[None, :]
        w_g = w[:, o[9]:o[10]].astype(bf16)
        xb = x.reshape(B * S, D).astype(bf16)

        scale = HEAD_DIM ** -0.5 * LOG2E
        qkv_scale = jnp.concatenate([jnp.full((Wa,), scale, f32), jnp.ones((2 * Wa,), f32),
                                     jnp.full((Wf,), scale, f32), jnp.ones((2 * Wf,), f32)])[None, :]
        qkv = _matmul(xb, w_qkv, qkv_scale, jnp.zeros_like(qkv_scale), bf16, "none").reshape(B, S, -1)
        ones_z = jnp.ones((1, Wa + Wf), f32)
        sz = _matmul(xb, w_z, ones_z, jnp.zeros_like(ones_z), f32, "silu").reshape(B, S, -1)
        gates = _matmul(xb, w_g, jnp.ones((1, 2 * D), f32), b_gate[layer].reshape(1, 2 * D).astype(f32),
                        f32, "sigmoid")

        cx = _forget_bias_columns(xb.reshape(B, S, D), w_f, b_f, Hf)
        ya = _moba_attention(qkv, rel_bias_table, sz, Ha, 0, 0)
        yf = _fox_attention(qkv, cx, sz, Hf, 3 * Ha, Ha)

        merged = _branch_merge(ya.reshape(B * S, Wa), yf.reshape(B * S, Wf),
                               w_branch[layer, 0].astype(bf16), w_branch[layer, 1].astype(bf16), gates)
        y = _out_layernorm(merged, w_out[layer].astype(bf16), x.reshape(B * S, D).astype(f32),
                           ln_gain[layer].reshape(1, D).astype(f32), ln_bias[layer].reshape(1, D).astype(f32))
        x = y.reshape(B, S, D).astype(x.dtype)
    return x
```

```python
import functools
import math
from typing import Any, Callable, NamedTuple

import jax
import jax.numpy as jnp
import numpy as np
from jax import lax
from jax.experimental import pallas as pl
from jax.experimental.pallas import tpu as pltpu

HEAD_DIM = 128
LANES = 128
MOBA_HEADS = 16
FOX_HEADS = 16
MOBA_BLOCK = 256
MOBA_TOPK = 3
T5_NUM_BUCKETS = 32
T5_MAX_DISTANCE = 128
LN_EPS = 1e-5
DEPTH = 1
DEEPNORM_ALPHA = (2.0 * DEPTH) ** 0.25
LOG2E = math.log2(math.e)

V7X_VMEM_LIMIT_BYTES = 56 * 1024 * 1024
PROJ_TN = 1024
NEG_BIG = -1e30

_NT_DIMS = (((1,), (1,)), ((), ()))


def _params(*semantics):
    return pltpu.CompilerParams(dimension_semantics=semantics,
                                vmem_limit_bytes=V7X_VMEM_LIMIT_BYTES)


def _split3_bf16(a):
    hi = a.astype(jnp.bfloat16).astype(jnp.float32)
    r = a - hi
    mid = r.astype(jnp.bfloat16).astype(jnp.float32)
    lo = (r - mid).astype(jnp.bfloat16).astype(jnp.float32)
    return hi, mid, lo


def _matmul_kernel(x_ref, w_ref, s_ref, b_ref, o_ref, *, act):
    acc = jnp.dot(x_ref[...], w_ref[...], preferred_element_type=jnp.float32)
    acc = acc * s_ref[...] + b_ref[...]
    if act == "silu":
        acc = acc * jax.nn.sigmoid(acc)
    elif act == "sigmoid":
        acc = jax.nn.sigmoid(acc)
    o_ref[...] = acc.astype(o_ref.dtype)


def _matmul(x, w, col_scale, col_bias, out_dtype, act, w_col_block=lambda j: j, tm=1024, tn=1024):
    M, K = x.shape
    N = col_scale.shape[1]
    tm, tn = min(tm, M), min(tn, N)
    return pl.pallas_call(
        functools.partial(_matmul_kernel, act=act),
        grid=(M // tm, N // tn),
        in_specs=[pl.BlockSpec((tm, K), lambda i, j: (i, 0)),
                  pl.BlockSpec((K, tn), lambda i, j: (0, w_col_block(j))),
                  pl.BlockSpec((1, tn), lambda i, j: (0, j)),
                  pl.BlockSpec((1, tn), lambda i, j: (0, j))],
        out_specs=pl.BlockSpec((tm, tn), lambda i, j: (i, j)),
        out_shape=jax.ShapeDtypeStruct((M, N), out_dtype),
        compiler_params=_params("parallel", "parallel"),
        name="proj_" + act,
    )(x, w, col_scale, col_bias)


def _forget_bias_kernel(x_ref, w_ref, b_ref, o_ref, carry_ref, *, heads):
    @pl.when(pl.program_id(1) == 0)
    def _():
        carry_ref[...] = jnp.zeros_like(carry_ref)

    ts = x_ref.shape[1]
    z = jnp.dot(x_ref[0], w_ref[...], preferred_element_type=jnp.float32) + b_ref[...]
    log_f = jnp.minimum(z, 0.0) - jnp.log1p(jnp.exp(-jnp.abs(z)))
    row = lax.broadcasted_iota(jnp.int32, (ts, ts), 0)
    col = lax.broadcasted_iota(jnp.int32, (ts, ts), 1)
    lower = jnp.where(col <= row, 1.0, 0.0).astype(jnp.bfloat16)
    c = carry_ref[...]
    for part in _split3_bf16(log_f):
        c = c + jnp.dot(lower, part.astype(jnp.bfloat16), preferred_element_type=jnp.float32)
    carry_ref[...] = c[ts - 1:ts, :]
    hi, mid, lo = _split3_bf16(c * (-LOG2E))
    lane = lax.broadcasted_iota(jnp.int32, (ts, LANES), 1)
    cols = jnp.where(lane < heads, hi,
                     jnp.where(lane < 2 * heads, pltpu.roll(mid, heads, 1),
                               jnp.where(lane < 3 * heads, pltpu.roll(lo, 2 * heads, 1), 0.0)))
    o_ref[0] = cols.astype(o_ref.dtype)


def _forget_bias_columns(x, w_f, b_forget, heads, ts=512):
    B, S, D = x.shape
    assert 3 * heads <= LANES
    ts = min(ts, S)
    return pl.pallas_call(
        functools.partial(_forget_bias_kernel, heads=heads),
        grid=(B, S // ts),
        in_specs=[pl.BlockSpec((1, ts, D), lambda b, i: (b, i, 0)),
                  pl.BlockSpec((D, LANES), lambda b, i: (0, 0)),
                  pl.BlockSpec((1, LANES), lambda b, i: (0, 0))],
        out_specs=pl.BlockSpec((1, ts, LANES), lambda b, i: (b, i, 0)),
        out_shape=jax.ShapeDtypeStruct((B, S, LANES), jnp.bfloat16),
        scratch_shapes=[pltpu.VMEM((1, LANES), jnp.float32)],
        compiler_params=_params("parallel", "arbitrary"),
        name="forget_bias_columns",
    )(x, w_f, b_forget)


ATTN_HEADS_PER_STEP = 2


def _softmax_tile(s, v_ext, m_ref, acc_ref):
    tk = s.shape[1]
    chunks = [s[:, c:c + LANES] for c in range(0, tk, LANES)]
    mx = functools.reduce(jnp.maximum, chunks)
    m_prev = m_ref[...]
    m_new = jnp.maximum(m_prev, jnp.max(mx, axis=-1, keepdims=True))
    alpha = jnp.exp2(m_prev - m_new)
    p = jnp.concatenate([jnp.exp2(ch - m_new) for ch in chunks], axis=1).astype(v_ext.dtype)
    pv = jnp.dot(p, v_ext, preferred_element_type=jnp.float32)
    acc_ref[...] = acc_ref[...] * jnp.concatenate([alpha, alpha], axis=1) + pv
    m_ref[...] = m_new


class _HeadTiles(NamedTuple):
    first_qk: Callable[[], Any]
    first_post: Callable[[Any], Any]
    first_values: Callable[[], Any]
    qk: Callable[[Any], Any]
    values: Callable[[Any], Any]
    sa_ref: Any
    sb_ref: Any
    m_ref: Any
    acc_ref: Any


def _pipelined_tiles(heads, n_rest):
    def fill(buf, logits_of):
        for hd in heads:
            s = logits_of(hd)
            getattr(hd, buf)[:, :s.shape[1]] = s

    def consume(buf, values_of, post=lambda hd, s: s):
        for hd in heads:
            v_ext = values_of(hd)
            s = getattr(hd, buf)[:, :v_ext.shape[0]]
            _softmax_tile(post(hd, s), v_ext, hd.m_ref, hd.acc_ref)

    fill("sa_ref", lambda hd: hd.first_qk())
    fill("sb_ref", lambda hd: hd.qk(0))
    consume("sa_ref", lambda hd: hd.first_values(), lambda hd, s: hd.first_post(s))

    def two_tiles(i, carry):
        fill("sa_ref", lambda hd: hd.qk(2 * i + 1))
        consume("sb_ref", lambda hd: hd.values(2 * i))
        fill("sb_ref", lambda hd: hd.qk(jnp.minimum(2 * i + 2, n_rest - 1)))
        consume("sa_ref", lambda hd: hd.values(2 * i + 1))
        return carry

    lax.fori_loop(0, n_rest // 2, two_tiles, 0)

    @pl.when(n_rest % 2 == 1)
    def _():
        consume("sb_ref", lambda hd: hd.values(n_rest - 1))


def _init_softmax_state(m_ref, acc_ref):
    m_ref[...] = jnp.full_like(m_ref, NEG_BIG)
    acc_ref[...] = jnp.zeros_like(acc_ref)


def _finish_attention(acc_ref, z, dtype):
    dh = z.shape[1]
    return (acc_ref[:, :dh] / acc_ref[:, dh:] * z).astype(dtype)


def _fox_kernel(q_ref, k_ref, cx_ref, v_ref, z_ref, o_ref, kaug_ref, vext_ref, sa_ref, sb_ref, m_ref, acc_ref,
                *, heads):
    qi = pl.program_id(2)
    tq = q_ref.shape[1]
    dh = HEAD_DIM
    G = kaug_ref.shape[0]
    S = k_ref.shape[1]

    @pl.when(qi == 0)
    def _():
        for g in range(G):
            kaug_ref[g, :, :dh] = k_ref[0, :, g * dh:(g + 1) * dh]
            kaug_ref[g, :, dh:] = cx_ref[0]
            vext_ref[g, :, :dh] = v_ref[0, :, g * dh:(g + 1) * dh]
            vext_ref[g, :, dh:] = jnp.ones((S, dh), vext_ref.dtype)

    def rows(j):
        return pl.ds(pl.multiple_of(j * tq, tq), tq)

    def causal(s):
        row = lax.broadcasted_iota(jnp.int32, (tq, tq), 0)
        col = lax.broadcasted_iota(jnp.int32, (tq, tq), 1)
        return jnp.where(col <= row, s, NEG_BIG)

    lane = lax.broadcasted_iota(jnp.int32, (tq, LANES), 1)
    tiles = []
    for g in range(G):
        h = pl.program_id(1) * G + g
        pick = (lane == h) | (lane == heads + h) | (lane == 2 * heads + h)
        q_aug = jnp.concatenate([q_ref[0, :, g * dh:(g + 1) * dh],
                                 jnp.where(pick, 1.0, 0.0).astype(q_ref.dtype)], axis=1)
        _init_softmax_state(m_ref.at[g], acc_ref.at[g])

        def logits(j, g=g, q_aug=q_aug):
            return lax.dot_general(q_aug, kaug_ref[g, rows(j), :], _NT_DIMS, preferred_element_type=jnp.float32)

        def values(j, g=g):
            return vext_ref[g, rows(j), :]

        tiles.append(_HeadTiles(functools.partial(logits, qi), causal, functools.partial(values, qi),
                                logits, values, sa_ref.at[g], sb_ref.at[g], m_ref.at[g], acc_ref.at[g]))
    _pipelined_tiles(tiles, qi)
    for g in range(G):
        o_ref[0, :, g * dh:(g + 1) * dh] = _finish_attention(acc_ref.at[g], z_ref[0, :, g * dh:(g + 1) * dh],
                                                             o_ref.dtype)


def _fox_attention(qkv, cx, sz, H, col0, zcol0, tq=512):
    B, S, _ = qkv.shape
    tq = min(tq, S)
    Dh = HEAD_DIM
    G = ATTN_HEADS_PER_STEP
    assert H % G == 0 and col0 % G == 0 and zcol0 % G == 0
    W = G * Dh
    return pl.pallas_call(
        functools.partial(_fox_kernel, heads=H),
        grid=(B, H // G, S // tq),
        in_specs=[pl.BlockSpec((1, tq, W), lambda b, h, i: (b, i, col0 // G + h)),
                  pl.BlockSpec((1, S, W), lambda b, h, i: (b, 0, (col0 + H) // G + h)),
                  pl.BlockSpec((1, S, LANES), lambda b, h, i: (b, 0, 0)),
                  pl.BlockSpec((1, S, W), lambda b, h, i: (b, 0, (col0 + 2 * H) // G + h)),
                  pl.BlockSpec((1, tq, W), lambda b, h, i: (b, i, zcol0 // G + h))],
        out_specs=pl.BlockSpec((1, tq, W), lambda b, h, i: (b, i, h)),
        out_shape=jax.ShapeDtypeStruct((B, S, H * Dh), jnp.bfloat16),
        scratch_shapes=[pltpu.VMEM((G, S, Dh + LANES), jnp.bfloat16), pltpu.VMEM((G, S, 2 * Dh), jnp.bfloat16),
                        pltpu.VMEM((G, tq, tq), jnp.float32), pltpu.VMEM((G, tq, tq), jnp.float32),
                        pltpu.VMEM((G, tq, LANES), jnp.float32), pltpu.VMEM((G, tq, 2 * Dh), jnp.float32)],
        compiler_params=_params("arbitrary", "arbitrary", "arbitrary"),
        name="fox_attention",
    )(qkv, qkv, cx, qkv, sz)


def _t5_bucket_np(dist):
    max_exact = T5_NUM_BUCKETS // 2
    d = np.maximum(dist, 1).astype(np.float32)
    ratio = (np.log(d / np.float32(max_exact)) / np.float32(math.log(T5_MAX_DISTANCE / max_exact))
             * np.float32(T5_NUM_BUCKETS - max_exact))
    large = np.minimum(max_exact + ratio.astype(np.int32), T5_NUM_BUCKETS - 1)
    return np.where(dist < max_exact, dist, large).astype(np.int32)


def _moba_query_columns(q, kmean, far_row, own, nb):
    nbp, blk = kmean.shape[0], q.shape[0]
    km_hi = kmean.astype(jnp.bfloat16)
    km_lo = (kmean - km_hi.astype(jnp.float32)).astype(jnp.bfloat16)
    gate = (lax.dot_general(km_hi, q, _NT_DIMS, preferred_element_type=jnp.float32)
            + lax.dot_general(km_lo, q, _NT_DIMS, preferred_element_type=jnp.float32))
    n_idx = lax.broadcasted_iota(jnp.int32, (nbp, blk), 0)
    past = n_idx < own
    gate = jnp.where(past, gate, NEG_BIG)
    beaten = jnp.zeros((nbp, blk), jnp.int32)
    for n in range(nb):
        g_n = gate[n:n + 1, :]
        wins = (g_n > gate) | ((g_n == gate) & (n < n_idx))
        beaten = beaten + jnp.where(wins & (n < own), 1, 0)
    selected = past & (beaten < MOBA_TOPK)
    far_bias = jnp.broadcast_to(far_row, (nbp, blk))
    term = jnp.where(selected, jnp.where(n_idx == own - 1, 0.0, far_bias), NEG_BIG)
    term = jnp.where(n_idx == own, 0.0, term)
    term_hi = term.astype(jnp.bfloat16).astype(jnp.float32)
    term_lo = jnp.where(selected, term - term_hi, 0.0)
    pieces = [term_hi[:nb], term_lo[:nb]]
    if 2 * nb < LANES:
        pieces.append(jnp.zeros((LANES - 2 * nb, blk), jnp.float32))
    return jnp.concatenate(pieces, axis=0).T.astype(q.dtype)


def _moba_kernel(q_ref, k_ref, v_ref, hot_ref, rbias_ref, far_ref, z_ref, o_ref,
                 kaug_ref, vext_ref, kmean_ref, t5_ref, sa_ref, sb_ref, m_ref, acc_ref, *, nb):
    own = pl.program_id(2)
    blk = q_ref.shape[1]
    dh = HEAD_DIM
    G = kaug_ref.shape[0]
    S = k_ref.shape[1]

    @pl.when(own == 0)
    def _():
        for g in range(G):
            kaug_ref[g, :, :dh] = k_ref[0, :, g * dh:(g + 1) * dh]
            kaug_ref[g, :, dh:] = hot_ref[...]
            vext_ref[g, :, :dh] = v_ref[0, :, g * dh:(g + 1) * dh]
            vext_ref[g, :, dh:] = jnp.ones((S, dh), vext_ref.dtype)
            kmean_ref[g] = jnp.zeros(kmean_ref.shape[1:], kmean_ref.dtype)
            for n in range(nb):
                kb = k_ref[0, n * blk:(n + 1) * blk, g * dh:(g + 1) * dh].astype(jnp.float32)
                kmean_ref[g, n:n + 1, :] = jnp.sum(kb, axis=0, keepdims=True) * (1.0 / blk)
            rb = jnp.broadcast_to(rbias_ref[g], (blk, 2 * blk))
            t5_ref[g, :, :blk] = jnp.zeros((blk, blk), jnp.float32)
            t5_ref[g, :, blk:] = pltpu.roll(rb, blk + 1, 1, stride=1, stride_axis=0)

    q_augs = []
    for g in range(G):
        q = q_ref[0, :, g * dh:(g + 1) * dh]
        q_augs.append(jnp.concatenate([q, _moba_query_columns(q, kmean_ref[g], far_ref[g], own, nb)], axis=1))
        _init_softmax_state(m_ref.at[g], acc_ref.at[g])

    def head_tiles(g, near_blocks):
        width = near_blocks * blk
        near_start = (own + 1 - near_blocks) * blk

        def logits(start, size):
            k = kaug_ref[g, pl.ds(pl.multiple_of(start, blk), size), :]
            return lax.dot_general(q_augs[g], k, _NT_DIMS, preferred_element_type=jnp.float32)

        def values(start, size):
            return vext_ref[g, pl.ds(pl.multiple_of(start, blk), size), :]

        def near(s):
            row = lax.broadcasted_iota(jnp.int32, (blk, width), 0)
            col = lax.broadcasted_iota(jnp.int32, (blk, width), 1)
            return jnp.where(col <= row + (width - blk), s + t5_ref[g, :, 3 * blk - width:], NEG_BIG)

        def pair_start(f):
            return jnp.minimum(2 * f, nb - 2) * blk

        return _HeadTiles(lambda: logits(near_start, width), near, lambda: values(near_start, width),
                          lambda f: logits(pair_start(f), 2 * blk), lambda f: values(pair_start(f), 2 * blk),
                          sa_ref.at[g], sb_ref.at[g], m_ref.at[g], acc_ref.at[g])

    @pl.when(own == 0)
    def _():
        for g in range(G):
            hd = head_tiles(g, 1)
            _softmax_tile(hd.first_post(hd.first_qk()), hd.first_values(), hd.m_ref, hd.acc_ref)

    @pl.when(own % 2 == 1)
    def _():
        _pipelined_tiles([head_tiles(g, 2) for g in range(G)], (own - 1) // 2)

    @pl.when((own % 2 == 0) & (own >= 2))
    def _():
        _pipelined_tiles([head_tiles(g, 3) for g in range(G)], (own - 1) // 2)

    for g in range(G):
        o_ref[0, :, g * dh:(g + 1) * dh] = _finish_attention(acc_ref.at[g], z_ref[0, :, g * dh:(g + 1) * dh],
                                                             o_ref.dtype)


def _moba_attention(qkv, rel_bias_table, sz, H, col0, zcol0):
    B, S, _ = qkv.shape
    blk = MOBA_BLOCK
    assert S % blk == 0
    nb = S // blk
    nbp = -(-nb // 8) * 8
    assert 3 <= nb and 2 * nb <= LANES
    Dh = HEAD_DIM
    G = ATTN_HEADS_PER_STEP
    assert H % G == 0 and col0 % G == 0 and zcol0 % G == 0
    W = G * Dh
    table_t = rel_bias_table.T.astype(jnp.float32) * LOG2E
    rbias = table_t[:, _t5_bucket_np(np.arange(2 * blk)[::-1])][:, None, :]
    far_bucket = int(_t5_bucket_np(np.array([blk + 1]))[0])
    assert (_t5_bucket_np(np.arange(blk + 1, S + blk)) == far_bucket).all()
    far = jnp.broadcast_to(table_t[:, far_bucket][:, None, None], (H, 1, blk))
    key_block = np.arange(S)[:, None] // blk
    lane = np.arange(LANES)[None, :]
    hot = jnp.asarray((lane == key_block) | (lane == key_block + nb), jnp.bfloat16)
    return pl.pallas_call(
        functools.partial(_moba_kernel, nb=nb),
        grid=(B, H // G, nb),
        in_specs=[pl.BlockSpec((1, blk, W), lambda b, h, i: (b, i, col0 // G + h)),
                  pl.BlockSpec((1, S, W), lambda b, h, i: (b, 0, (col0 + H) // G + h)),
                  pl.BlockSpec((1, S, W), lambda b, h, i: (b, 0, (col0 + 2 * H) // G + h)),
                  pl.BlockSpec((S, LANES), lambda b, h, i: (0, 0)),
                  pl.BlockSpec((G, 1, 2 * blk), lambda b, h, i: (h, 0, 0)),
                  pl.BlockSpec((G, 1, blk), lambda b, h, i: (h, 0, 0)),
                  pl.BlockSpec((1, blk, W), lambda b, h, i: (b, i, zcol0 // G + h))],
        out_specs=pl.BlockSpec((1, blk, W), lambda b, h, i: (b, i, h)),
        out_shape=jax.ShapeDtypeStruct((B, S, H * Dh), jnp.bfloat16),
        scratch_shapes=[pltpu.VMEM((G, S, Dh + LANES), jnp.bfloat16), pltpu.VMEM((G, S, 2 * Dh), jnp.bfloat16),
                        pltpu.VMEM((G, nbp, Dh), jnp.float32), pltpu.VMEM((G, blk, 3 * blk), jnp.float32),
                        pltpu.VMEM((G, blk, 3 * blk), jnp.float32), pltpu.VMEM((G, blk, 3 * blk), jnp.float32),
                        pltpu.VMEM((G, blk, LANES), jnp.float32), pltpu.VMEM((G, blk, 2 * Dh), jnp.float32)],
        compiler_params=_params("arbitrary", "arbitrary", "arbitrary"),
        name="moba_attention",
    )(qkv, qkv, qkv, hot, rbias, far, sz)


def _branch_merge_kernel(ya_ref, yf_ref, wa_ref, wf_ref, ga_ref, gf_ref, o_ref):
    ua = jnp.dot(ya_ref[...], wa_ref[...], preferred_element_type=jnp.float32)
    uf = jnp.dot(yf_ref[...], wf_ref[...], preferred_element_type=jnp.float32)
    o_ref[...] = (ga_ref[...] * ua + gf_ref[...] * uf).astype(o_ref.dtype)


def _branch_merge(ya, yf, wa, wf, gates, tm=512, tn=1024):
    M, W = ya.shape
    D = wa.shape[1]
    tm, tn = min(tm, M), min(tn, D)
    nj = D // tn
    return pl.pallas_call(
        _branch_merge_kernel,
        grid=(M // tm, nj),
        in_specs=[pl.BlockSpec((tm, W), lambda i, j: (i, 0)),
                  pl.BlockSpec((tm, W), lambda i, j: (i, 0)),
                  pl.BlockSpec((W, tn), lambda i, j: (0, j)),
                  pl.BlockSpec((W, tn), lambda i, j: (0, j)),
                  pl.BlockSpec((tm, tn), lambda i, j: (i, j)),
                  pl.BlockSpec((tm, tn), lambda i, j: (i, nj + j))],
        out_specs=pl.BlockSpec((tm, tn), lambda i, j: (i, j)),
        out_shape=jax.ShapeDtypeStruct((M, D), jnp.bfloat16),
        compiler_params=_params("parallel", "parallel"),
        name="branch_merge",
    )(ya, yf, wa, wf, gates, gates)


LN_ROW_CHUNK = 64


def _out_layernorm_kernel(h_ref, w_ref, x_ref, g_ref, b_ref, o_ref):
    j = pl.program_id(1)
    tn = w_ref.shape[1]
    out = jnp.dot(h_ref[...], w_ref[...], preferred_element_type=jnp.float32)
    o_ref[:, pl.ds(pl.multiple_of(j * tn, tn), tn)] = DEEPNORM_ALPHA * x_ref[...] + out

    @pl.when(j == pl.num_programs(1) - 1)
    def _():
        rows = min(LN_ROW_CHUNK, o_ref.shape[0])

        def normalize_rows(i, carry):
            sl = pl.ds(pl.multiple_of(i * rows, rows), rows)
            r = o_ref[sl, :]
            mu = jnp.mean(r, axis=-1, keepdims=True)
            d = r - mu
            var = jnp.mean(d * d, axis=-1, keepdims=True)
            o_ref[sl, :] = d * lax.rsqrt(var + LN_EPS) * g_ref[...] + b_ref[...]
            return carry

        lax.fori_loop(0, o_ref.shape[0] // rows, normalize_rows, 0)


def _out_layernorm(h, w, x, gain, bias, tm=512, tn=1024):
    M, D = x.shape
    tm, tn = min(tm, M), min(tn, D)
    return pl.pallas_call(
        _out_layernorm_kernel,
        grid=(M // tm, D // tn),
        in_specs=[pl.BlockSpec((tm, D), lambda i, j: (i, 0)),
                  pl.BlockSpec((D, tn), lambda i, j: (0, j)),
                  pl.BlockSpec((tm, tn), lambda i, j: (i, j)),
                  pl.BlockSpec((1, D), lambda i, j: (0, 0)),
                  pl.BlockSpec((1, D), lambda i, j: (0, 0))],
        out_specs=pl.BlockSpec((tm, D), lambda i, j: (i, 0)),
        out_shape=jax.ShapeDtypeStruct((M, D), jnp.float32),
        compiler_params=_params("parallel", "arbitrary"),
        name="out_layernorm",
    )(h, w, x, gain, bias)


def kernel(x, w_in, b_forget, b_gate, rel_bias_table, w_branch, w_out, ln_gain, ln_bias):
    B, S, D = x.shape
    Ha, Hf = MOBA_HEADS, FOX_HEADS
    Wa, Wf = Ha * HEAD_DIM, Hf * HEAD_DIM
    bf16, f32 = jnp.bfloat16, jnp.float32
    depth = w_in.shape[0]
    for layer in range(depth):
        w = w_in[layer]
        n_attn = 4 * Wa + 4 * Wf
        w_attn = w[:, :n_attn].astype(bf16)
        w_f = jnp.pad(w[:, n_attn:n_attn + Hf], ((0, 0), (0, LANES - Hf))).astype(bf16)
        b_f = jnp.pad(b_forget[layer].astype(f32), (0, LANES - Hf))[None, :]
        w_g = w[:, n_attn + Hf:].astype(bf16)
        xb = x.reshape(B * S, D).astype(bf16)

        scale = HEAD_DIM ** -0.5 * LOG2E
        qkv_scale = jnp.concatenate([jnp.full((Wa,), scale, f32), jnp.ones((2 * Wa,), f32),
                                     jnp.full((Wf,), scale, f32), jnp.ones((2 * Wf,), f32)])[None, :]
        tn = PROJ_TN
        assert Wa % tn == 0 and Wf % tn == 0
        a_blocks, f_blocks = Wa // tn, Wf // tn
        qkv = _matmul(xb, w_attn, qkv_scale, jnp.zeros_like(qkv_scale), bf16, "none",
                      lambda j: jnp.where(j < 3 * a_blocks, j, j + a_blocks), tn=tn).reshape(B, S, -1)
        ones_z = jnp.ones((1, Wa + Wf), f32)
        sz = _matmul(xb, w_attn, ones_z, jnp.zeros_like(ones_z), f32, "silu",
                     lambda j: jnp.where(j < a_blocks, j + 3 * a_blocks, j + 3 * a_blocks + 3 * f_blocks),
                     tn=tn).reshape(B, S, -1)
        gates = _matmul(xb, w_g, jnp.ones((1, 2 * D), f32), b_gate[layer].reshape(1, 2 * D).astype(f32),
                        f32, "sigmoid", tn=tn)

        cx = _forget_bias_columns(xb.reshape(B, S, D), w_f, b_f, Hf)
        ya = _moba_attention(qkv, rel_bias_table, sz, Ha, 0, 0)
        yf = _fox_attention(qkv, cx, sz, Hf, 3 * Ha, Ha)

        merged = _branch_merge(ya.reshape(B * S, Wa), yf.reshape(B * S, Wf),
                               w_branch[layer, 0].astype(bf16), w_branch[layer, 1].astype(bf16), gates)
        y = _out_layernorm(merged, w_out[layer].astype(bf16), x.reshape(B * S, D).astype(f32),
                           ln_gain[layer].reshape(1, D).astype(f32), ln_bias[layer].reshape(1, D).astype(f32))
        x = y.reshape(B, S, D).astype(x.dtype)
    return x
```

```python
import functools
import math
from typing import Any, Callable, NamedTuple, Sequence

import jax
import jax.numpy as jnp
import numpy as np
from jax import lax
from jax.experimental import pallas as pl
from jax.experimental.pallas import tpu as pltpu

HEAD_DIM = 128
LANES = 128
MOBA_HEADS = 16
FOX_HEADS = 16
MOBA_BLOCK = 256
MOBA_TOPK = 3
T5_NUM_BUCKETS = 32
T5_MAX_DISTANCE = 128
LN_EPS = 1e-5
DEPTH = 1
DEEPNORM_ALPHA = (2.0 * DEPTH) ** 0.25
LOG2E = math.log2(math.e)

V7X_VMEM_LIMIT_BYTES = 56 * 1024 * 1024
PROJ_TN = 1024
NEG_BIG = -1e30

_NT_DIMS = (((1,), (1,)), ((), ()))


def _params(*semantics):
    return pltpu.CompilerParams(dimension_semantics=semantics,
                                vmem_limit_bytes=V7X_VMEM_LIMIT_BYTES)


def _split3_bf16(a):
    hi = a.astype(jnp.bfloat16).astype(jnp.float32)
    r = a - hi
    mid = r.astype(jnp.bfloat16).astype(jnp.float32)
    lo = (r - mid).astype(jnp.bfloat16).astype(jnp.float32)
    return hi, mid, lo


def _matmul_kernel(x_ref, w_ref, s_ref, b_ref, o_ref, *, act):
    acc = jnp.dot(x_ref[...], w_ref[...], preferred_element_type=jnp.float32)
    acc = acc * s_ref[...] + b_ref[...]
    if act == "silu":
        acc = acc * jax.nn.sigmoid(acc)
    elif act == "sigmoid":
        acc = jax.nn.sigmoid(acc)
    o_ref[...] = acc.astype(o_ref.dtype)


def _matmul(x, w, col_scale, col_bias, out_dtype, act, w_col_block=lambda j: j, tm=1024, tn=1024):
    M, K = x.shape
    N = col_scale.shape[1]
    tm, tn = min(tm, M), min(tn, N)
    return pl.pallas_call(
        functools.partial(_matmul_kernel, act=act),
        grid=(M // tm, N // tn),
        in_specs=[pl.BlockSpec((tm, K), lambda i, j: (i, 0)),
                  pl.BlockSpec((K, tn), lambda i, j: (0, w_col_block(j))),
                  pl.BlockSpec((1, tn), lambda i, j: (0, j)),
                  pl.BlockSpec((1, tn), lambda i, j: (0, j))],
        out_specs=pl.BlockSpec((tm, tn), lambda i, j: (i, j)),
        out_shape=jax.ShapeDtypeStruct((M, N), out_dtype),
        compiler_params=_params("parallel", "parallel"),
        name="proj_" + act,
    )(x, w, col_scale, col_bias)


def _forget_bias_kernel(x_ref, w_ref, b_ref, o_ref, carry_ref, *, heads):
    @pl.when(pl.program_id(1) == 0)
    def _():
        carry_ref[...] = jnp.zeros_like(carry_ref)

    ts = x_ref.shape[0]
    z = jnp.dot(x_ref[...], w_ref[...], preferred_element_type=jnp.float32) + b_ref[...]
    log_f = jnp.minimum(z, 0.0) - jnp.log1p(jnp.exp(-jnp.abs(z)))
    row = lax.broadcasted_iota(jnp.int32, (ts, ts), 0)
    col = lax.broadcasted_iota(jnp.int32, (ts, ts), 1)
    lower = jnp.where(col <= row, 1.0, 0.0).astype(jnp.bfloat16)
    c = carry_ref[...]
    for part in _split3_bf16(log_f):
        c = c + jnp.dot(lower, part.astype(jnp.bfloat16), preferred_element_type=jnp.float32)
    carry_ref[...] = c[ts - 1:ts, :]
    hi, mid, lo = _split3_bf16(c * (-LOG2E))
    lane = lax.broadcasted_iota(jnp.int32, (ts, LANES), 1)
    cols = jnp.where(lane < heads, hi,
                     jnp.where(lane < 2 * heads, pltpu.roll(mid, heads, 1),
                               jnp.where(lane < 3 * heads, pltpu.roll(lo, 2 * heads, 1), 0.0)))
    o_ref[0] = cols.astype(o_ref.dtype)


def _forget_bias_columns(x, w_f, b_forget, heads, batch, ts=512):
    M, D = x.shape
    S = M // batch
    assert 3 * heads <= LANES
    ts = min(ts, S)
    steps = S // ts
    return pl.pallas_call(
        functools.partial(_forget_bias_kernel, heads=heads),
        grid=(batch, steps),
        in_specs=[pl.BlockSpec((ts, D), lambda b, i: (b * steps + i, 0)),
                  pl.BlockSpec((D, LANES), lambda b, i: (0, 0)),
                  pl.BlockSpec((1, LANES), lambda b, i: (0, 0))],
        out_specs=pl.BlockSpec((1, ts, LANES), lambda b, i: (b, i, 0)),
        out_shape=jax.ShapeDtypeStruct((batch, S, LANES), jnp.bfloat16),
        scratch_shapes=[pltpu.VMEM((1, LANES), jnp.float32)],
        compiler_params=_params("parallel", "arbitrary"),
        name="forget_bias_columns",
    )(x, w_f, b_forget)


FOX_HEADS_PER_STEP = 4
MOBA_HEADS_PER_STEP = 2


def _softmax_tile(s, v_ext, m_ref, acc_ref):
    tk = s.shape[1]
    chunks = [s[:, c:c + LANES] for c in range(0, tk, LANES)]
    mx = functools.reduce(jnp.maximum, chunks)
    m_prev = m_ref[...]
    m_new = jnp.maximum(m_prev, jnp.max(mx, axis=-1, keepdims=True))
    alpha = jnp.exp2(m_prev - m_new)
    p = jnp.concatenate([jnp.exp2(ch - m_new) for ch in chunks], axis=1).astype(v_ext.dtype)
    pv = jnp.dot(p, v_ext, preferred_element_type=jnp.float32)
    acc_ref[...] = acc_ref[...] * jnp.concatenate([alpha, alpha], axis=1) + pv
    m_ref[...] = m_new


class _Tile(NamedTuple):
    qk: Callable[[], Any]
    post: Callable[[Any], Any]
    values: Callable[[], Any]


class _HeadTiles(NamedTuple):
    specials: Sequence[_Tile]
    qk: Callable[[Any], Any]
    values: Callable[[Any], Any]
    s_refs: Sequence[Any]
    m_ref: Any
    acc_ref: Any


def _pipelined_tiles(heads, n_rest):
    def fill(buf, logits_of):
        for hd in heads:
            s = logits_of(hd)
            hd.s_refs[buf][:, :s.shape[1]] = s

    def consume(buf, values_of, post=lambda hd, s: s):
        for hd in heads:
            v_ext = values_of(hd)
            s = hd.s_refs[buf][:, :v_ext.shape[0]]
            _softmax_tile(post(hd, s), v_ext, hd.m_ref, hd.acc_ref)

    n_special = len(heads[0].specials)
    fill(0, lambda hd: hd.specials[0].qk())
    for t in range(n_special):
        if t + 1 < n_special:
            fill((t + 1) % 2, lambda hd: hd.specials[t + 1].qk())
        else:
            fill((t + 1) % 2, lambda hd: hd.qk(0))
        consume(t % 2, lambda hd: hd.specials[t].values(), lambda hd, s: hd.specials[t].post(s))
    cur, nxt = n_special % 2, (n_special + 1) % 2

    def two_tiles(i, carry):
        fill(nxt, lambda hd: hd.qk(2 * i + 1))
        consume(cur, lambda hd: hd.values(2 * i))
        fill(cur, lambda hd: hd.qk(jnp.minimum(2 * i + 2, n_rest - 1)))
        consume(nxt, lambda hd: hd.values(2 * i + 1))
        return carry

    lax.fori_loop(0, n_rest // 2, two_tiles, 0)

    @pl.when(n_rest % 2 == 1)
    def _():
        consume(cur, lambda hd: hd.values(n_rest - 1))


def _init_softmax_state(m_ref, acc_ref):
    m_ref[...] = jnp.full_like(m_ref, NEG_BIG)
    acc_ref[...] = jnp.zeros_like(acc_ref)


def _finish_attention(acc_ref, z, dtype):
    dh = z.shape[1]
    return (acc_ref[:, :dh] / acc_ref[:, dh:] * z).astype(dtype)


def _fox_kernel(q_ref, k_ref, cx_ref, v_ref, z_ref, o_ref, kaug_ref, vext_ref, sa_ref, sb_ref, m_ref, acc_ref,
                *, heads):
    qi = pl.program_id(2)
    tq = q_ref.shape[1]
    dh = HEAD_DIM
    G = kaug_ref.shape[0]
    S = k_ref.shape[1]

    @pl.when(qi == 0)
    def _():
        for g in range(G):
            kaug_ref[g, :, :dh] = k_ref[0, :, g * dh:(g + 1) * dh]
            kaug_ref[g, :, dh:] = cx_ref[0]
            vext_ref[g, :, :dh] = v_ref[0, :, g * dh:(g + 1) * dh]
            vext_ref[g, :, dh:] = jnp.ones((S, dh), vext_ref.dtype)

    def rows(j):
        return pl.ds(pl.multiple_of(j * tq, tq), tq)

    def causal(s):
        row = lax.broadcasted_iota(jnp.int32, (tq, tq), 0)
        col = lax.broadcasted_iota(jnp.int32, (tq, tq), 1)
        return jnp.where(col <= row, s, NEG_BIG)

    lane = lax.broadcasted_iota(jnp.int32, (tq, LANES), 1)
    tiles = []
    for g in range(G):
        h = pl.program_id(1) * G + g
        pick = (lane == h) | (lane == heads + h) | (lane == 2 * heads + h)
        q_aug = jnp.concatenate([q_ref[0, :, g * dh:(g + 1) * dh],
                                 jnp.where(pick, 1.0, 0.0).astype(q_ref.dtype)], axis=1)
        _init_softmax_state(m_ref.at[g], acc_ref.at[g])

        def logits(j, g=g, q_aug=q_aug):
            return lax.dot_general(q_aug, kaug_ref[g, rows(j), :], _NT_DIMS, preferred_element_type=jnp.float32)

        def values(j, g=g):
            return vext_ref[g, rows(j), :]

        diagonal = _Tile(functools.partial(logits, qi), causal, functools.partial(values, qi))
        tiles.append(_HeadTiles([diagonal], logits, values, (sa_ref.at[g], sb_ref.at[g]), m_ref.at[g], acc_ref.at[g]))
    _pipelined_tiles(tiles, qi)
    for g in range(G):
        o_ref[0, :, g * dh:(g + 1) * dh] = _finish_attention(acc_ref.at[g], z_ref[0, :, g * dh:(g + 1) * dh],
                                                             o_ref.dtype)


def _fox_attention(qkv, cx, sz, H, col0, zcol0, tq=512):
    B, S, _ = qkv.shape
    tq = min(tq, S)
    Dh = HEAD_DIM
    G = FOX_HEADS_PER_STEP
    assert H % G == 0 and col0 % G == 0 and zcol0 % G == 0
    W = G * Dh
    return pl.pallas_call(
        functools.partial(_fox_kernel, heads=H),
        grid=(B, H // G, S // tq),
        in_specs=[pl.BlockSpec((1, tq, W), lambda b, h, i: (b, i, col0 // G + h)),
                  pl.BlockSpec((1, S, W), lambda b, h, i: (b, 0, (col0 + H) // G + h)),
                  pl.BlockSpec((1, S, LANES), lambda b, h, i: (b, 0, 0)),
                  pl.BlockSpec((1, S, W), lambda b, h, i: (b, 0, (col0 + 2 * H) // G + h)),
                  pl.BlockSpec((1, tq, W), lambda b, h, i: (b, i, zcol0 // G + h))],
        out_specs=pl.BlockSpec((1, tq, W), lambda b, h, i: (b, i, h)),
        out_shape=jax.ShapeDtypeStruct((B, S, H * Dh), jnp.bfloat16),
        scratch_shapes=[pltpu.VMEM((G, S, Dh + LANES), jnp.bfloat16), pltpu.VMEM((G, S, 2 * Dh), jnp.bfloat16),
                        pltpu.VMEM((G, tq, tq), jnp.float32), pltpu.VMEM((G, tq, tq), jnp.float32),
                        pltpu.VMEM((G, tq, LANES), jnp.float32), pltpu.VMEM((G, tq, 2 * Dh), jnp.float32)],
        compiler_params=_params("arbitrary", "arbitrary", "arbitrary"),
        name="fox_attention",
    )(qkv, qkv, cx, qkv, sz)


def _t5_bucket_np(dist):
    max_exact = T5_NUM_BUCKETS // 2
    d = np.maximum(dist, 1).astype(np.float32)
    ratio = (np.log(d / np.float32(max_exact)) / np.float32(math.log(T5_MAX_DISTANCE / max_exact))
             * np.float32(T5_NUM_BUCKETS - max_exact))
    large = np.minimum(max_exact + ratio.astype(np.int32), T5_NUM_BUCKETS - 1)
    return np.where(dist < max_exact, dist, large).astype(np.int32)


MOBA_Q_BLOCKS = 2


def _moba_query_columns(q, kmean, far_row, first_block, blk, nb):
    nbp, tq = kmean.shape[0], q.shape[0]
    km_hi = kmean.astype(jnp.bfloat16)
    km_lo = (kmean - km_hi.astype(jnp.float32)).astype(jnp.bfloat16)
    gate = (lax.dot_general(km_hi, q, _NT_DIMS, preferred_element_type=jnp.float32)
            + lax.dot_general(km_lo, q, _NT_DIMS, preferred_element_type=jnp.float32))
    n_idx = lax.broadcasted_iota(jnp.int32, (nbp, tq), 0)
    own = first_block + lax.broadcasted_iota(jnp.int32, (nbp, tq), 1) // blk
    past = n_idx < own
    gate = jnp.where(past, gate, NEG_BIG)
    beaten = jnp.zeros((nbp, tq), jnp.int32)
    for n in range(nb):
        g_n = gate[n:n + 1, :]
        wins = (g_n > gate) | ((g_n == gate) & (n < n_idx))
        beaten = beaten + jnp.where(wins & (n < own), 1, 0)
    selected = past & (beaten < MOBA_TOPK)
    far_bias = jnp.broadcast_to(far_row, (nbp, tq))
    term = jnp.where(selected, jnp.where(n_idx == own - 1, 0.0, far_bias), NEG_BIG)
    term = jnp.where(n_idx == own, 0.0, term)
    term_hi = term.astype(jnp.bfloat16).astype(jnp.float32)
    term_lo = jnp.where(selected, term - term_hi, 0.0)
    pieces = [term_hi[:nb], term_lo[:nb]]
    if 2 * nb < LANES:
        pieces.append(jnp.zeros((LANES - 2 * nb, tq), jnp.float32))
    return jnp.concatenate(pieces, axis=0).T.astype(q.dtype)


def _moba_kernel(q_ref, k_ref, v_ref, hot_ref, rbias_ref, far_ref, z_ref, o_ref,
                 kaug_ref, vext_ref, kmean_ref, t5_ref, sa_ref, sb_ref, m_ref, acc_ref, *, nb, blk):
    qi = pl.program_id(2)
    tq = q_ref.shape[1]
    dh = HEAD_DIM
    G = kaug_ref.shape[0]
    S = k_ref.shape[1]

    @pl.when(qi == 0)
    def _():
        for g in range(G):
            kaug_ref[g, :, :dh] = k_ref[0, :, g * dh:(g + 1) * dh]
            kaug_ref[g, :, dh:] = hot_ref[...]
            vext_ref[g, :, :dh] = v_ref[0, :, g * dh:(g + 1) * dh]
            vext_ref[g, :, dh:] = jnp.ones((S, dh), vext_ref.dtype)
            kmean_ref[g] = jnp.zeros(kmean_ref.shape[1:], kmean_ref.dtype)
            for n in range(nb):
                kb = k_ref[0, n * blk:(n + 1) * blk, g * dh:(g + 1) * dh].astype(jnp.float32)
                kmean_ref[g, n:n + 1, :] = jnp.sum(kb, axis=0, keepdims=True) * (1.0 / blk)
            rb = jnp.broadcast_to(rbias_ref[g], (tq, tq))
            t5_ref[g] = pltpu.roll(rb, 1, 1, stride=1, stride_axis=0)

    def rows(j):
        return pl.ds(pl.multiple_of(j * tq, tq), tq)

    def causal_with_bias(s, g):
        row = lax.broadcasted_iota(jnp.int32, (tq, tq), 0)
        col = lax.broadcasted_iota(jnp.int32, (tq, tq), 1)
        return jnp.where(col <= row, s + t5_ref[g], NEG_BIG)

    def previous_with_bias(s, g):
        top = jnp.concatenate([s[:blk, :tq - blk], s[:blk, tq - blk:] + t5_ref[g, tq - blk:, :blk]], axis=1)
        return jnp.concatenate([top, s[blk:, :]], axis=0)

    tiles = []
    for g in range(G):
        q = q_ref[0, :, g * dh:(g + 1) * dh]
        q_extra = _moba_query_columns(q, kmean_ref[g], far_ref[g], qi * (tq // blk), blk, nb)
        q_aug = jnp.concatenate([q, q_extra], axis=1)
        _init_softmax_state(m_ref.at[g], acc_ref.at[g])

        def logits(j, g=g, q_aug=q_aug):
            return lax.dot_general(q_aug, kaug_ref[g, rows(j), :], _NT_DIMS, preferred_element_type=jnp.float32)

        def values(j, g=g):
            return vext_ref[g, rows(j), :]

        diagonal = _Tile(functools.partial(logits, qi), functools.partial(causal_with_bias, g=g),
                         functools.partial(values, qi))
        previous = _Tile(functools.partial(logits, qi - 1), functools.partial(previous_with_bias, g=g),
                         functools.partial(values, qi - 1))
        tiles.append((diagonal, previous, logits, values, (sa_ref.at[g], sb_ref.at[g]), m_ref.at[g], acc_ref.at[g]))

    @pl.when(qi == 0)
    def _():
        _pipelined_tiles([_HeadTiles([t[0]], *t[2:]) for t in tiles], 0)

    @pl.when(qi >= 1)
    def _():
        _pipelined_tiles([_HeadTiles([t[0], t[1]], *t[2:]) for t in tiles], qi - 1)

    for g in range(G):
        o_ref[0, :, g * dh:(g + 1) * dh] = _finish_attention(acc_ref.at[g], z_ref[0, :, g * dh:(g + 1) * dh],
                                                             o_ref.dtype)


def _moba_attention(qkv, rel_bias_table, sz, H, col0, zcol0):
    B, S, _ = qkv.shape
    blk = MOBA_BLOCK
    tq = MOBA_Q_BLOCKS * blk
    assert MOBA_Q_BLOCKS == 2 and S % tq == 0
    nb = S // blk
    nbp = -(-nb // 8) * 8
    assert 2 * nb <= LANES
    Dh = HEAD_DIM
    G = MOBA_HEADS_PER_STEP
    assert H % G == 0 and col0 % G == 0 and zcol0 % G == 0
    W = G * Dh
    table_t = rel_bias_table.T.astype(jnp.float32) * LOG2E
    rbias = table_t[:, _t5_bucket_np(np.arange(tq)[::-1])][:, None, :]
    far_bucket = int(_t5_bucket_np(np.array([blk + 1]))[0])
    assert (_t5_bucket_np(np.arange(blk + 1, S + blk)) == far_bucket).all()
    far = jnp.broadcast_to(table_t[:, far_bucket][:, None, None], (H, 1, tq))
    key_block = np.arange(S)[:, None] // blk
    lane = np.arange(LANES)[None, :]
    hot = jnp.asarray((lane == key_block) | (lane == key_block + nb), jnp.bfloat16)
    return pl.pallas_call(
        functools.partial(_moba_kernel, nb=nb, blk=blk),
        grid=(B, H // G, S // tq),
        in_specs=[pl.BlockSpec((1, tq, W), lambda b, h, i: (b, i, col0 // G + h)),
                  pl.BlockSpec((1, S, W), lambda b, h, i: (b, 0, (col0 + H) // G + h)),
                  pl.BlockSpec((1, S, W), lambda b, h, i: (b, 0, (col0 + 2 * H) // G + h)),
                  pl.BlockSpec((S, LANES), lambda b, h, i: (0, 0)),
                  pl.BlockSpec((G, 1, tq), lambda b, h, i: (h, 0, 0)),
                  pl.BlockSpec((G, 1, tq), lambda b, h, i: (h, 0, 0)),
                  pl.BlockSpec((1, tq, W), lambda b, h, i: (b, i, zcol0 // G + h))],
        out_specs=pl.BlockSpec((1, tq, W), lambda b, h, i: (b, i, h)),
        out_shape=jax.ShapeDtypeStruct((B, S, H * Dh), jnp.bfloat16),
        scratch_shapes=[pltpu.VMEM((G, S, Dh + LANES), jnp.bfloat16), pltpu.VMEM((G, S, 2 * Dh), jnp.bfloat16),
                        pltpu.VMEM((G, nbp, Dh), jnp.float32), pltpu.VMEM((G, tq, tq), jnp.float32),
                        pltpu.VMEM((G, tq, tq), jnp.float32), pltpu.VMEM((G, tq, tq), jnp.float32),
                        pltpu.VMEM((G, tq, LANES), jnp.float32), pltpu.VMEM((G, tq, 2 * Dh), jnp.float32)],
        compiler_params=_params("arbitrary", "arbitrary", "arbitrary"),
        name="moba_attention",
    )(qkv, qkv, qkv, hot, rbias, far, sz)


def _branch_merge_kernel(ya_ref, yf_ref, wa_ref, wf_ref, ga_ref, gf_ref, o_ref):
    ua = jnp.dot(ya_ref[...], wa_ref[...], preferred_element_type=jnp.float32)
    uf = jnp.dot(yf_ref[...], wf_ref[...], preferred_element_type=jnp.float32)
    o_ref[...] = (ga_ref[...] * ua + gf_ref[...] * uf).astype(o_ref.dtype)


def _branch_merge(ya, yf, wa, wf, gates, tm=512, tn=1024):
    M, W = ya.shape
    D = wa.shape[1]
    tm, tn = min(tm, M), min(tn, D)
    nj = D // tn
    return pl.pallas_call(
        _branch_merge_kernel,
        grid=(M // tm, nj),
        in_specs=[pl.BlockSpec((tm, W), lambda i, j: (i, 0)),
                  pl.BlockSpec((tm, W), lambda i, j: (i, 0)),
                  pl.BlockSpec((W, tn), lambda i, j: (0, j)),
                  pl.BlockSpec((W, tn), lambda i, j: (0, j)),
                  pl.BlockSpec((tm, tn), lambda i, j: (i, j)),
                  pl.BlockSpec((tm, tn), lambda i, j: (i, nj + j))],
        out_specs=pl.BlockSpec((tm, tn), lambda i, j: (i, j)),
        out_shape=jax.ShapeDtypeStruct((M, D), jnp.bfloat16),
        compiler_params=_params("parallel", "parallel"),
        name="branch_merge",
    )(ya, yf, wa, wf, gates, gates)


LN_ROW_CHUNK = 64


def _out_layernorm_kernel(h_ref, w_ref, x_ref, g_ref, b_ref, o_ref):
    j = pl.program_id(1)
    tn = w_ref.shape[1]
    out = jnp.dot(h_ref[...], w_ref[...], preferred_element_type=jnp.float32)
    o_ref[:, pl.ds(pl.multiple_of(j * tn, tn), tn)] = DEEPNORM_ALPHA * x_ref[...] + out

    @pl.when(j == pl.num_programs(1) - 1)
    def _():
        rows = min(LN_ROW_CHUNK, o_ref.shape[0])

        def normalize_rows(i, carry):
            sl = pl.ds(pl.multiple_of(i * rows, rows), rows)
            r = o_ref[sl, :]
            mu = jnp.mean(r, axis=-1, keepdims=True)
            d = r - mu
            var = jnp.mean(d * d, axis=-1, keepdims=True)
            o_ref[sl, :] = d * lax.rsqrt(var + LN_EPS) * g_ref[...] + b_ref[...]
            return carry

        lax.fori_loop(0, o_ref.shape[0] // rows, normalize_rows, 0)


def _out_layernorm(h, w, x, gain, bias, tm=512, tn=1024):
    M, D = x.shape
    tm, tn = min(tm, M), min(tn, D)
    return pl.pallas_call(
        _out_layernorm_kernel,
        grid=(M // tm, D // tn),
        in_specs=[pl.BlockSpec((tm, D), lambda i, j: (i, 0)),
                  pl.BlockSpec((D, tn), lambda i, j: (0, j)),
                  pl.BlockSpec((tm, tn), lambda i, j: (i, j)),
                  pl.BlockSpec((1, D), lambda i, j: (0, 0)),
                  pl.BlockSpec((1, D), lambda i, j: (0, 0))],
        out_specs=pl.BlockSpec((tm, D), lambda i, j: (i, 0)),
        out_shape=jax.ShapeDtypeStruct((M, D), jnp.float32),
        compiler_params=_params("parallel", "arbitrary"),
        name="out_layernorm",
    )(h, w, x, gain, bias)


def kernel(x, w_in, b_forget, b_gate, rel_bias_table, w_branch, w_out, ln_gain, ln_bias):
    B, S, D = x.shape
    Ha, Hf = MOBA_HEADS, FOX_HEADS
    Wa, Wf = Ha * HEAD_DIM, Hf * HEAD_DIM
    bf16, f32 = jnp.bfloat16, jnp.float32
    depth = w_in.shape[0]
    for layer in range(depth):
        w = w_in[layer]
        n_attn = 4 * Wa + 4 * Wf
        w_attn = w[:, :n_attn].astype(bf16)
        w_f = jnp.pad(w[:, n_attn:n_attn + Hf], ((0, 0), (0, LANES - Hf))).astype(bf16)
        b_f = jnp.pad(b_forget[layer].astype(f32), (0, LANES - Hf))[None, :]
        w_g = w[:, n_attn + Hf:].astype(bf16)
        xb = x.reshape(B * S, D).astype(bf16)

        scale = HEAD_DIM ** -0.5 * LOG2E
        qkv_scale = jnp.concatenate([jnp.full((Wa,), scale, f32), jnp.ones((2 * Wa,), f32),
                                     jnp.full((Wf,), scale, f32), jnp.ones((2 * Wf,), f32)])[None, :]
        tn = PROJ_TN
        assert Wa % tn == 0 and Wf % tn == 0
        a_blocks, f_blocks = Wa // tn, Wf // tn
        qkv = _matmul(xb, w_attn, qkv_scale, jnp.zeros_like(qkv_scale), bf16, "none",
                      lambda j: jnp.where(j < 3 * a_blocks, j, j + a_blocks), tn=tn).reshape(B, S, -1)
        ones_z = jnp.ones((1, Wa + Wf), f32)
        sz = _matmul(xb, w_attn, ones_z, jnp.zeros_like(ones_z), f32, "silu",
                     lambda j: jnp.where(j < a_blocks, j + 3 * a_blocks, j + 3 * a_blocks + 3 * f_blocks),
                     tn=tn).reshape(B, S, -1)
        gates = _matmul(xb, w_g, jnp.ones((1, 2 * D), f32), b_gate[layer].reshape(1, 2 * D).astype(f32),
                        f32, "sigmoid", tn=tn)

        cx = _forget_bias_columns(xb, w_f, b_f, Hf, B)
        ya = _moba_attention(qkv, rel_bias_table, sz, Ha, 0, 0)
        yf = _fox_attention(qkv, cx, sz, Hf, 3 * Ha, Ha)

        merged = _branch_merge(ya.reshape(B * S, Wa), yf.reshape(B * S, Wf),
                               w_branch[layer, 0].astype(bf16), w_branch[layer, 1].astype(bf16), gates)
        y = _out_layernorm(merged, w_out[layer].astype(bf16), x.reshape(B * S, D).astype(f32),
                           ln_gain[layer].reshape(1, D).astype(f32), ln_bias[layer].reshape(1, D).astype(f32))
        x = y.reshape(B, S, D).astype(x.dtype)
    return x
```

```python
import functools
import math
from typing import Any, Callable, NamedTuple, Sequence

import jax
import jax.numpy as jnp
import numpy as np
from jax import lax
from jax.experimental import pallas as pl
from jax.experimental.pallas import tpu as pltpu

HEAD_DIM = 128
LANES = 128
MOBA_HEADS = 16
FOX_HEADS = 16
MOBA_BLOCK = 256
MOBA_TOPK = 3
T5_NUM_BUCKETS = 32
T5_MAX_DISTANCE = 128
LN_EPS = 1e-5
DEPTH = 1
DEEPNORM_ALPHA = (2.0 * DEPTH) ** 0.25
LOG2E = math.log2(math.e)

V7X_VMEM_LIMIT_BYTES = 56 * 1024 * 1024
PROJ_TN = 1024
NEG_BIG = -1e30

_NT_DIMS = (((1,), (1,)), ((), ()))


def _params(*semantics):
    return pltpu.CompilerParams(dimension_semantics=semantics,
                                vmem_limit_bytes=V7X_VMEM_LIMIT_BYTES)


def _split3_bf16(a):
    hi = a.astype(jnp.bfloat16).astype(jnp.float32)
    r = a - hi
    mid = r.astype(jnp.bfloat16).astype(jnp.float32)
    lo = (r - mid).astype(jnp.bfloat16).astype(jnp.float32)
    return hi, mid, lo


def _matmul_kernel(x_ref, w_ref, s_ref, b_ref, o_ref, *, act):
    acc = jnp.dot(x_ref[...], w_ref[...], preferred_element_type=jnp.float32)
    acc = acc * s_ref[...] + b_ref[...]
    if act == "silu":
        acc = acc * jax.nn.sigmoid(acc)
    elif act == "sigmoid":
        acc = jax.nn.sigmoid(acc)
    o_ref[...] = acc.astype(o_ref.dtype)


def _matmul(x, w, col_scale, col_bias, out_dtype, act, w_col_block=lambda j: j, tm=1024, tn=1024):
    M, K = x.shape
    N = col_scale.shape[1]
    tm, tn = min(tm, M), min(tn, N)
    return pl.pallas_call(
        functools.partial(_matmul_kernel, act=act),
        grid=(M // tm, N // tn),
        in_specs=[pl.BlockSpec((tm, K), lambda i, j: (i, 0)),
                  pl.BlockSpec((K, tn), lambda i, j: (0, w_col_block(j))),
                  pl.BlockSpec((1, tn), lambda i, j: (0, j)),
                  pl.BlockSpec((1, tn), lambda i, j: (0, j))],
        out_specs=pl.BlockSpec((tm, tn), lambda i, j: (i, j)),
        out_shape=jax.ShapeDtypeStruct((M, N), out_dtype),
        compiler_params=_params("parallel", "parallel"),
        name="proj_" + act,
    )(x, w, col_scale, col_bias)


def _cast_weight_kernel(w_ref, tail_ref, o_ref, *, aligned_blocks, shift):
    j = pl.program_id(1)
    tc = o_ref.shape[1]

    @pl.when(j < aligned_blocks)
    def _():
        o_ref[...] = w_ref[...].astype(o_ref.dtype)

    @pl.when(j >= aligned_blocks)
    def _():
        both = jnp.concatenate([w_ref[...], tail_ref[...]], axis=1)
        o_ref[...] = both[:, shift:shift + tc].astype(o_ref.dtype)


def _cast_projection_weight(w, n_aligned, shift, tr=2048, tc=512):
    K, n_total = w.shape
    n_out = n_total - shift
    tr = min(tr, K)
    assert n_aligned % tc == 0 and n_out % tc == 0 and tc % LANES == 0 and 0 < shift < LANES
    lane_blocks = tc // LANES
    return pl.pallas_call(
        functools.partial(_cast_weight_kernel, aligned_blocks=n_aligned // tc, shift=shift),
        grid=(K // tr, n_out // tc),
        in_specs=[pl.BlockSpec((tr, tc), lambda i, j: (i, j)),
                  pl.BlockSpec((tr, LANES), lambda i, j: (i, (jnp.maximum(j, n_aligned // tc) + 1) * lane_blocks))],
        out_specs=pl.BlockSpec((tr, tc), lambda i, j: (i, j)),
        out_shape=jax.ShapeDtypeStruct((K, n_out), jnp.bfloat16),
        compiler_params=_params("parallel", "parallel"),
        name="cast_projection_weight",
    )(w, w)


def _forget_bias_kernel(x_ref, w_ref, b_ref, o_ref, carry_ref, *, heads):
    @pl.when(pl.program_id(1) == 0)
    def _():
        carry_ref[...] = jnp.zeros_like(carry_ref)

    ts = x_ref.shape[0]
    z = jnp.dot(x_ref[...], w_ref[...], preferred_element_type=jnp.float32) + b_ref[...]
    log_f = jnp.minimum(z, 0.0) - jnp.log1p(jnp.exp(-jnp.abs(z)))
    row = lax.broadcasted_iota(jnp.int32, (ts, ts), 0)
    col = lax.broadcasted_iota(jnp.int32, (ts, ts), 1)
    lower = jnp.where(col <= row, 1.0, 0.0).astype(jnp.bfloat16)
    c = carry_ref[...]
    for part in _split3_bf16(log_f):
        c = c + jnp.dot(lower, part.astype(jnp.bfloat16), preferred_element_type=jnp.float32)
    carry_ref[...] = c[ts - 1:ts, :]
    hi, mid, lo = _split3_bf16(c * (-LOG2E))
    lane = lax.broadcasted_iota(jnp.int32, (ts, LANES), 1)
    cols = jnp.where(lane < heads, hi,
                     jnp.where(lane < 2 * heads, pltpu.roll(mid, heads, 1),
                               jnp.where(lane < 3 * heads, pltpu.roll(lo, 2 * heads, 1), 0.0)))
    o_ref[0] = cols.astype(o_ref.dtype)


def _forget_bias_columns(x, w_f, b_forget, heads, batch, ts=512):
    M, D = x.shape
    S = M // batch
    assert 3 * heads <= LANES
    ts = min(ts, S)
    steps = S // ts
    return pl.pallas_call(
        functools.partial(_forget_bias_kernel, heads=heads),
        grid=(batch, steps),
        in_specs=[pl.BlockSpec((ts, D), lambda b, i: (b * steps + i, 0)),
                  pl.BlockSpec((D, LANES), lambda b, i: (0, 0)),
                  pl.BlockSpec((1, LANES), lambda b, i: (0, 0))],
        out_specs=pl.BlockSpec((1, ts, LANES), lambda b, i: (b, i, 0)),
        out_shape=jax.ShapeDtypeStruct((batch, S, LANES), jnp.bfloat16),
        scratch_shapes=[pltpu.VMEM((1, LANES), jnp.float32)],
        compiler_params=_params("parallel", "arbitrary"),
        name="forget_bias_columns",
    )(x, w_f, b_forget)


FOX_HEADS_PER_STEP = 4
MOBA_HEADS_PER_STEP = 4


def _softmax_tile(s, v_ext, m_ref, acc_ref):
    tk = s.shape[1]
    chunks = [s[:, c:c + LANES] for c in range(0, tk, LANES)]
    mx = functools.reduce(jnp.maximum, chunks)
    m_prev = m_ref[...]
    m_new = jnp.maximum(m_prev, jnp.max(mx, axis=-1, keepdims=True))
    alpha = jnp.exp2(m_prev - m_new)
    p = jnp.concatenate([jnp.exp2(ch - m_new) for ch in chunks], axis=1).astype(v_ext.dtype)
    pv = jnp.dot(p, v_ext, preferred_element_type=jnp.float32)
    acc_ref[...] = acc_ref[...] * jnp.concatenate([alpha, alpha], axis=1) + pv
    m_ref[...] = m_new


class _Tile(NamedTuple):
    qk: Callable[[], Any]
    post: Callable[[Any], Any]
    values: Callable[[], Any]


class _HeadTiles(NamedTuple):
    specials: Sequence[_Tile]
    qk: Callable[[Any], Any]
    values: Callable[[Any], Any]
    s_refs: Sequence[Any]
    m_ref: Any
    acc_ref: Any


def _pipelined_tiles(heads, n_rest):
    def fill(buf, logits_of):
        for hd in heads:
            s = logits_of(hd)
            hd.s_refs[buf][:, :s.shape[1]] = s

    def consume(buf, values_of, post=lambda hd, s: s):
        for hd in heads:
            v_ext = values_of(hd)
            s = hd.s_refs[buf][:, :v_ext.shape[0]]
            _softmax_tile(post(hd, s), v_ext, hd.m_ref, hd.acc_ref)

    n_special = len(heads[0].specials)
    fill(0, lambda hd: hd.specials[0].qk())
    for t in range(n_special):
        if t + 1 < n_special:
            fill((t + 1) % 2, lambda hd: hd.specials[t + 1].qk())
        else:
            fill((t + 1) % 2, lambda hd: hd.qk(0))
        consume(t % 2, lambda hd: hd.specials[t].values(), lambda hd, s: hd.specials[t].post(s))
    cur, nxt = n_special % 2, (n_special + 1) % 2

    def two_tiles(i, carry):
        fill(nxt, lambda hd: hd.qk(2 * i + 1))
        consume(cur, lambda hd: hd.values(2 * i))
        fill(cur, lambda hd: hd.qk(jnp.minimum(2 * i + 2, n_rest - 1)))
        consume(nxt, lambda hd: hd.values(2 * i + 1))
        return carry

    lax.fori_loop(0, n_rest // 2, two_tiles, 0)

    @pl.when(n_rest % 2 == 1)
    def _():
        consume(cur, lambda hd: hd.values(n_rest - 1))


def _init_softmax_state(m_ref, acc_ref):
    m_ref[...] = jnp.full_like(m_ref, NEG_BIG)
    acc_ref[...] = jnp.zeros_like(acc_ref)


def _finish_attention(acc_ref, z, dtype):
    dh = z.shape[1]
    return (acc_ref[:, :dh] / acc_ref[:, dh:] * z).astype(dtype)


def _fox_kernel(q_ref, k_ref, cx_ref, v_ref, z_ref, o_ref, kaug_ref, vext_ref, sa_ref, sb_ref, m_ref, acc_ref,
                *, heads):
    qi = pl.program_id(2)
    tq = q_ref.shape[1]
    dh = HEAD_DIM
    G = kaug_ref.shape[0]
    S = k_ref.shape[1]

    @pl.when(qi == 0)
    def _():
        for g in range(G):
            kaug_ref[g, :, :dh] = k_ref[0, :, g * dh:(g + 1) * dh]
            kaug_ref[g, :, dh:] = cx_ref[0]
            vext_ref[g, :, :dh] = v_ref[0, :, g * dh:(g + 1) * dh]
            vext_ref[g, :, dh:] = jnp.ones((S, dh), vext_ref.dtype)

    def rows(j):
        return pl.ds(pl.multiple_of(j * tq, tq), tq)

    def causal(s):
        row = lax.broadcasted_iota(jnp.int32, (tq, tq), 0)
        col = lax.broadcasted_iota(jnp.int32, (tq, tq), 1)
        return jnp.where(col <= row, s, NEG_BIG)

    lane = lax.broadcasted_iota(jnp.int32, (tq, LANES), 1)
    tiles = []
    for g in range(G):
        h = pl.program_id(1) * G + g
        pick = (lane == h) | (lane == heads + h) | (lane == 2 * heads + h)
        q_aug = jnp.concatenate([q_ref[0, :, g * dh:(g + 1) * dh],
                                 jnp.where(pick, 1.0, 0.0).astype(q_ref.dtype)], axis=1)
        _init_softmax_state(m_ref.at[g], acc_ref.at[g])

        def logits(j, g=g, q_aug=q_aug):
            return lax.dot_general(q_aug, kaug_ref[g, rows(j), :], _NT_DIMS, preferred_element_type=jnp.float32)

        def values(j, g=g):
            return vext_ref[g, rows(j), :]

        diagonal = _Tile(functools.partial(logits, qi), causal, functools.partial(values, qi))
        tiles.append(_HeadTiles([diagonal], logits, values, (sa_ref.at[g], sb_ref.at[g]), m_ref.at[g], acc_ref.at[g]))
    _pipelined_tiles(tiles, qi)
    for g in range(G):
        o_ref[0, :, g * dh:(g + 1) * dh] = _finish_attention(acc_ref.at[g], z_ref[0, :, g * dh:(g + 1) * dh],
                                                             o_ref.dtype)


def _fox_attention(qkv, cx, sz, H, col0, zcol0, tq=512):
    B, S, _ = qkv.shape
    tq = min(tq, S)
    Dh = HEAD_DIM
    G = FOX_HEADS_PER_STEP
    assert H % G == 0 and col0 % G == 0 and zcol0 % G == 0
    W = G * Dh
    return pl.pallas_call(
        functools.partial(_fox_kernel, heads=H),
        grid=(B, H // G, S // tq),
        in_specs=[pl.BlockSpec((1, tq, W), lambda b, h, i: (b, i, col0 // G + h)),
                  pl.BlockSpec((1, S, W), lambda b, h, i: (b, 0, (col0 + H) // G + h)),
                  pl.BlockSpec((1, S, LANES), lambda b, h, i: (b, 0, 0)),
                  pl.BlockSpec((1, S, W), lambda b, h, i: (b, 0, (col0 + 2 * H) // G + h)),
                  pl.BlockSpec((1, tq, W), lambda b, h, i: (b, i, zcol0 // G + h))],
        out_specs=pl.BlockSpec((1, tq, W), lambda b, h, i: (b, i, h)),
        out_shape=jax.ShapeDtypeStruct((B, S, H * Dh), jnp.bfloat16),
        scratch_shapes=[pltpu.VMEM((G, S, Dh + LANES), jnp.bfloat16), pltpu.VMEM((G, S, 2 * Dh), jnp.bfloat16),
                        pltpu.VMEM((G, tq, tq), jnp.float32), pltpu.VMEM((G, tq, tq), jnp.float32),
                        pltpu.VMEM((G, tq, LANES), jnp.float32), pltpu.VMEM((G, tq, 2 * Dh), jnp.float32)],
        compiler_params=_params("arbitrary", "arbitrary", "arbitrary"),
        name="fox_attention",
    )(qkv, qkv, cx, qkv, sz)


def _t5_bucket_np(dist):
    max_exact = T5_NUM_BUCKETS // 2
    d = np.maximum(dist, 1).astype(np.float32)
    ratio = (np.log(d / np.float32(max_exact)) / np.float32(math.log(T5_MAX_DISTANCE / max_exact))
             * np.float32(T5_NUM_BUCKETS - max_exact))
    large = np.minimum(max_exact + ratio.astype(np.int32), T5_NUM_BUCKETS - 1)
    return np.where(dist < max_exact, dist, large).astype(np.int32)


MOBA_Q_BLOCKS = 2


def _moba_query_columns(q, kmean, far_row, first_block, blk, nb):
    nbp, tq = kmean.shape[0], q.shape[0]
    km_hi = kmean.astype(jnp.bfloat16)
    km_lo = (kmean - km_hi.astype(jnp.float32)).astype(jnp.bfloat16)
    gate = (lax.dot_general(km_hi, q, _NT_DIMS, preferred_element_type=jnp.float32)
            + lax.dot_general(km_lo, q, _NT_DIMS, preferred_element_type=jnp.float32))
    n_idx = lax.broadcasted_iota(jnp.int32, (nbp, tq), 0)
    own = first_block + lax.broadcasted_iota(jnp.int32, (nbp, tq), 1) // blk
    past = n_idx < own
    gate = jnp.where(past, gate, NEG_BIG)
    beaten = jnp.zeros((nbp, tq), jnp.int32)
    for n in range(nb):
        g_n = gate[n:n + 1, :]
        wins = (g_n > gate) | ((g_n == gate) & (n < n_idx))
        beaten = beaten + jnp.where(wins & (n < own), 1, 0)
    selected = past & (beaten < MOBA_TOPK)
    far_bias = jnp.broadcast_to(far_row, (nbp, tq))
    term = jnp.where(selected, jnp.where(n_idx == own - 1, 0.0, far_bias), NEG_BIG)
    term = jnp.where(n_idx == own, 0.0, term)
    term_hi = term.astype(jnp.bfloat16).astype(jnp.float32)
    term_lo = jnp.where(selected, term - term_hi, 0.0)
    pieces = [term_hi[:nb], term_lo[:nb]]
    if 2 * nb < LANES:
        pieces.append(jnp.zeros((LANES - 2 * nb, tq), jnp.float32))
    return jnp.concatenate(pieces, axis=0).T.astype(q.dtype)


def _moba_kernel(q_ref, k_ref, v_ref, hot_ref, rbias_ref, far_ref, z_ref, o_ref,
                 kaug_ref, vext_ref, kmean_ref, t5_ref, sa_ref, sb_ref, m_ref, acc_ref, *, nb, blk):
    qi = pl.program_id(2)
    tq = q_ref.shape[1]
    dh = HEAD_DIM
    G = kaug_ref.shape[0]
    S = k_ref.shape[1]

    @pl.when(qi == 0)
    def _():
        for g in range(G):
            kaug_ref[g, :, :dh] = k_ref[0, :, g * dh:(g + 1) * dh]
            kaug_ref[g, :, dh:] = hot_ref[...]
            vext_ref[g, :, :dh] = v_ref[0, :, g * dh:(g + 1) * dh]
            vext_ref[g, :, dh:] = jnp.ones((S, dh), vext_ref.dtype)
            kmean_ref[g] = jnp.zeros(kmean_ref.shape[1:], kmean_ref.dtype)
            for n in range(nb):
                kb = k_ref[0, n * blk:(n + 1) * blk, g * dh:(g + 1) * dh].astype(jnp.float32)
                kmean_ref[g, n:n + 1, :] = jnp.sum(kb, axis=0, keepdims=True) * (1.0 / blk)
            rb = jnp.broadcast_to(rbias_ref[g], (tq, tq))
            t5_ref[g] = pltpu.roll(rb, 1, 1, stride=1, stride_axis=0)

    def rows(j):
        return pl.ds(pl.multiple_of(j * tq, tq), tq)

    def causal_with_bias(s, g):
        row = lax.broadcasted_iota(jnp.int32, (tq, tq), 0)
        col = lax.broadcasted_iota(jnp.int32, (tq, tq), 1)
        return jnp.where(col <= row, s + t5_ref[g], NEG_BIG)

    def previous_with_bias(s, g):
        top = jnp.concatenate([s[:blk, :tq - blk], s[:blk, tq - blk:] + t5_ref[g, tq - blk:, :blk]], axis=1)
        return jnp.concatenate([top, s[blk:, :]], axis=0)

    tiles = []
    for g in range(G):
        q = q_ref[0, :, g * dh:(g + 1) * dh]
        q_extra = _moba_query_columns(q, kmean_ref[g], far_ref[g], qi * (tq // blk), blk, nb)
        q_aug = jnp.concatenate([q, q_extra], axis=1)
        _init_softmax_state(m_ref.at[g], acc_ref.at[g])

        def logits(j, g=g, q_aug=q_aug):
            return lax.dot_general(q_aug, kaug_ref[g, rows(j), :], _NT_DIMS, preferred_element_type=jnp.float32)

        def values(j, g=g):
            return vext_ref[g, rows(j), :]

        diagonal = _Tile(functools.partial(logits, qi), functools.partial(causal_with_bias, g=g),
                         functools.partial(values, qi))
        previous = _Tile(functools.partial(logits, qi - 1), functools.partial(previous_with_bias, g=g),
                         functools.partial(values, qi - 1))
        tiles.append((diagonal, previous, logits, values, (sa_ref.at[g], sb_ref.at[g]), m_ref.at[g], acc_ref.at[g]))

    @pl.when(qi == 0)
    def _():
        _pipelined_tiles([_HeadTiles([t[0]], *t[2:]) for t in tiles], 0)

    @pl.when(qi >= 1)
    def _():
        _pipelined_tiles([_HeadTiles([t[0], t[1]], *t[2:]) for t in tiles], qi - 1)

    for g in range(G):
        o_ref[0, :, g * dh:(g + 1) * dh] = _finish_attention(acc_ref.at[g], z_ref[0, :, g * dh:(g + 1) * dh],
                                                             o_ref.dtype)


def _moba_attention(qkv, rel_bias_table, sz, H, col0, zcol0):
    B, S, _ = qkv.shape
    blk = MOBA_BLOCK
    tq = MOBA_Q_BLOCKS * blk
    assert MOBA_Q_BLOCKS == 2 and S % tq == 0
    nb = S // blk
    nbp = -(-nb // 8) * 8
    assert 2 * nb <= LANES
    Dh = HEAD_DIM
    G = MOBA_HEADS_PER_STEP
    assert H % G == 0 and col0 % G == 0 and zcol0 % G == 0
    W = G * Dh
    table_t = rel_bias_table.T.astype(jnp.float32) * LOG2E
    rbias = table_t[:, _t5_bucket_np(np.arange(tq)[::-1])][:, None, :]
    far_bucket = int(_t5_bucket_np(np.array([blk + 1]))[0])
    assert (_t5_bucket_np(np.arange(blk + 1, S + blk)) == far_bucket).all()
    far = jnp.broadcast_to(table_t[:, far_bucket][:, None, None], (H, 1, tq))
    key_block = np.arange(S)[:, None] // blk
    lane = np.arange(LANES)[None, :]
    hot = jnp.asarray((lane == key_block) | (lane == key_block + nb), jnp.bfloat16)
    once = pl.Buffered(1)
    return pl.pallas_call(
        functools.partial(_moba_kernel, nb=nb, blk=blk),
        grid=(B, H // G, S // tq),
        in_specs=[pl.BlockSpec((1, tq, W), lambda b, h, i: (b, i, col0 // G + h)),
                  pl.BlockSpec((1, S, W), lambda b, h, i: (b, 0, (col0 + H) // G + h), pipeline_mode=once),
                  pl.BlockSpec((1, S, W), lambda b, h, i: (b, 0, (col0 + 2 * H) // G + h), pipeline_mode=once),
                  pl.BlockSpec((S, LANES), lambda b, h, i: (0, 0), pipeline_mode=once),
                  pl.BlockSpec((G, 1, tq), lambda b, h, i: (h, 0, 0)),
                  pl.BlockSpec((G, 1, tq), lambda b, h, i: (h, 0, 0)),
                  pl.BlockSpec((1, tq, W), lambda b, h, i: (b, i, zcol0 // G + h))],
        out_specs=pl.BlockSpec((1, tq, W), lambda b, h, i: (b, i, h)),
        out_shape=jax.ShapeDtypeStruct((B, S, H * Dh), jnp.bfloat16),
        scratch_shapes=[pltpu.VMEM((G, S, Dh + LANES), jnp.bfloat16), pltpu.VMEM((G, S, 2 * Dh), jnp.bfloat16),
                        pltpu.VMEM((G, nbp, Dh), jnp.float32), pltpu.VMEM((G, tq, tq), jnp.float32),
                        pltpu.VMEM((G, tq, tq), jnp.float32), pltpu.VMEM((G, tq, tq), jnp.float32),
                        pltpu.VMEM((G, tq, LANES), jnp.float32), pltpu.VMEM((G, tq, 2 * Dh), jnp.float32)],
        compiler_params=_params("arbitrary", "arbitrary", "arbitrary"),
        name="moba_attention",
    )(qkv, qkv, qkv, hot, rbias, far, sz)


def _branch_merge_kernel(ya_ref, yf_ref, wa_ref, wf_ref, ga_ref, gf_ref, o_ref):
    ua = jnp.dot(ya_ref[...], wa_ref[0], preferred_element_type=jnp.float32)
    uf = jnp.dot(yf_ref[...], wf_ref[0], preferred_element_type=jnp.float32)
    o_ref[...] = (ga_ref[...] * ua + gf_ref[...] * uf).astype(o_ref.dtype)


def _branch_merge(ya, yf, w, gates, tm=512, tn=1024):
    M, W = ya.shape
    D = w.shape[2]
    tm, tn = min(tm, M), min(tn, D)
    nj = D // tn
    return pl.pallas_call(
        _branch_merge_kernel,
        grid=(M // tm, nj),
        in_specs=[pl.BlockSpec((tm, W), lambda i, j: (i, 0)),
                  pl.BlockSpec((tm, W), lambda i, j: (i, 0)),
                  pl.BlockSpec((1, W, tn), lambda i, j: (0, 0, j)),
                  pl.BlockSpec((1, W, tn), lambda i, j: (1, 0, j)),
                  pl.BlockSpec((tm, tn), lambda i, j: (i, j)),
                  pl.BlockSpec((tm, tn), lambda i, j: (i, nj + j))],
        out_specs=pl.BlockSpec((tm, tn), lambda i, j: (i, j)),
        out_shape=jax.ShapeDtypeStruct((M, D), jnp.bfloat16),
        compiler_params=_params("parallel", "parallel"),
        name="branch_merge",
    )(ya, yf, w, w, gates, gates)


LN_ROW_CHUNK = 64


def _out_layernorm_kernel(h_ref, w_ref, x_ref, g_ref, b_ref, o_ref):
    j = pl.program_id(1)
    tn = w_ref.shape[1]
    out = jnp.dot(h_ref[...], w_ref[...], preferred_element_type=jnp.float32)
    o_ref[:, pl.ds(pl.multiple_of(j * tn, tn), tn)] = DEEPNORM_ALPHA * x_ref[...] + out

    @pl.when(j == pl.num_programs(1) - 1)
    def _():
        rows = min(LN_ROW_CHUNK, o_ref.shape[0])

        def normalize_rows(i, carry):
            sl = pl.ds(pl.multiple_of(i * rows, rows), rows)
            r = o_ref[sl, :]
            mu = jnp.mean(r, axis=-1, keepdims=True)
            d = r - mu
            var = jnp.mean(d * d, axis=-1, keepdims=True)
            o_ref[sl, :] = d * lax.rsqrt(var + LN_EPS) * g_ref[...] + b_ref[...]
            return carry

        lax.fori_loop(0, o_ref.shape[0] // rows, normalize_rows, 0)


def _out_layernorm(h, w, x, gain, bias, tm=512, tn=1024):
    M, D = x.shape
    tm, tn = min(tm, M), min(tn, D)
    return pl.pallas_call(
        _out_layernorm_kernel,
        grid=(M // tm, D // tn),
        in_specs=[pl.BlockSpec((tm, D), lambda i, j: (i, 0)),
                  pl.BlockSpec((D, tn), lambda i, j: (0, j)),
                  pl.BlockSpec((tm, tn), lambda i, j: (i, j)),
                  pl.BlockSpec((1, D), lambda i, j: (0, 0)),
                  pl.BlockSpec((1, D), lambda i, j: (0, 0))],
        out_specs=pl.BlockSpec((tm, D), lambda i, j: (i, 0)),
        out_shape=jax.ShapeDtypeStruct((M, D), jnp.float32),
        compiler_params=_params("parallel", "arbitrary"),
        name="out_layernorm",
    )(h, w, x, gain, bias)


def kernel(x, w_in, b_forget, b_gate, rel_bias_table, w_branch, w_out, ln_gain, ln_bias):
    B, S, D = x.shape
    Ha, Hf = MOBA_HEADS, FOX_HEADS
    Wa, Wf = Ha * HEAD_DIM, Hf * HEAD_DIM
    bf16, f32 = jnp.bfloat16, jnp.float32
    depth = w_in.shape[0]
    for layer in range(depth):
        w = w_in[layer]
        n_attn = 4 * Wa + 4 * Wf
        w_all = _cast_projection_weight(w, n_attn, Hf)
        w_f = jnp.pad(w[:, n_attn:n_attn + Hf], ((0, 0), (0, LANES - Hf))).astype(bf16)
        b_f = jnp.pad(b_forget[layer].astype(f32), (0, LANES - Hf))[None, :]
        xb = x.reshape(B * S, D).astype(bf16)

        scale = HEAD_DIM ** -0.5 * LOG2E
        qkv_scale = jnp.concatenate([jnp.full((Wa,), scale, f32), jnp.ones((2 * Wa,), f32),
                                     jnp.full((Wf,), scale, f32), jnp.ones((2 * Wf,), f32)])[None, :]
        tn = PROJ_TN
        assert Wa % tn == 0 and Wf % tn == 0
        a_blocks, f_blocks = Wa // tn, Wf // tn
        qkv = _matmul(xb, w_all, qkv_scale, jnp.zeros_like(qkv_scale), bf16, "none",
                      lambda j: jnp.where(j < 3 * a_blocks, j, j + a_blocks), tn=tn).reshape(B, S, -1)
        ones_z = jnp.ones((1, Wa + Wf), f32)
        sz = _matmul(xb, w_all, ones_z, jnp.zeros_like(ones_z), f32, "silu",
                     lambda j: jnp.where(j < a_blocks, j + 3 * a_blocks, j + 3 * a_blocks + 3 * f_blocks),
                     tn=tn).reshape(B, S, -1)
        gates = _matmul(xb, w_all, jnp.ones((1, 2 * D), f32), b_gate[layer].reshape(1, 2 * D).astype(f32),
                        f32, "sigmoid", lambda j: j + n_attn // tn, tn=tn)

        cx = _forget_bias_columns(xb, w_f, b_f, Hf, B)
        ya = _moba_attention(qkv, rel_bias_table, sz, Ha, 0, 0)
        yf = _fox_attention(qkv, cx, sz, Hf, 3 * Ha, Ha)

        assert Wa == Wf
        merged = _branch_merge(ya.reshape(B * S, Wa), yf.reshape(B * S, Wf), w_branch[layer].astype(bf16), gates)
        y = _out_layernorm(merged, w_out[layer].astype(bf16), x.reshape(B * S, D).astype(f32),
                           ln_gain[layer].reshape(1, D).astype(f32), ln_bias[layer].reshape(1, D).astype(f32))
        x = y.reshape(B, S, D).astype(x.dtype)
    return x
```

```python
import functools
import math
from typing import Any, Callable, NamedTuple, Sequence

import jax
import jax.numpy as jnp
import numpy as np
from jax import lax
from jax.experimental import pallas as pl
from jax.experimental.pallas import tpu as pltpu

HEAD_DIM = 128
LANES = 128
MOBA_HEADS = 16
FOX_HEADS = 16
MOBA_BLOCK = 256
MOBA_TOPK = 3
T5_NUM_BUCKETS = 32
T5_MAX_DISTANCE = 128
LN_EPS = 1e-5
DEPTH = 1
DEEPNORM_ALPHA = (2.0 * DEPTH) ** 0.25
LOG2E = math.log2(math.e)

V7X_VMEM_LIMIT_BYTES = 56 * 1024 * 1024
PROJ_TN = 1024
NEG_BIG = -1e30

_NT_DIMS = (((1,), (1,)), ((), ()))


def _params(*semantics):
    return pltpu.CompilerParams(dimension_semantics=semantics,
                                vmem_limit_bytes=V7X_VMEM_LIMIT_BYTES)


def _split3_bf16(a):
    hi = a.astype(jnp.bfloat16).astype(jnp.float32)
    r = a - hi
    mid = r.astype(jnp.bfloat16).astype(jnp.float32)
    lo = (r - mid).astype(jnp.bfloat16).astype(jnp.float32)
    return hi, mid, lo


def _matmul_kernel(x_ref, wt_ref, s_ref, b_ref, o_ref, *, act):
    acc = lax.dot_general(x_ref[...], wt_ref[...], _NT_DIMS, preferred_element_type=jnp.float32)
    acc = acc * s_ref[...] + b_ref[...]
    if act == "silu":
        acc = acc * jax.nn.sigmoid(acc)
    elif act == "sigmoid":
        acc = jax.nn.sigmoid(acc)
    o_ref[...] = acc.astype(o_ref.dtype)


def _matmul(x, w_t, col_scale, col_bias, out_dtype, act, w_row_block=lambda j: j, w_row_start=None,
            tm=1024, tn=1024):
    M, K = x.shape
    N = col_scale.shape[1]
    tm, tn = min(tm, M), min(tn, N)
    if w_row_start is None:
        w_spec = pl.BlockSpec((tn, K), lambda i, j: (w_row_block(j), 0))
    else:
        align = math.gcd(w_row_start, tn)
        w_spec = pl.BlockSpec((pl.Element(tn), pl.Element(K)),
                              lambda i, j: (pl.multiple_of(w_row_start + j * tn, align), 0))
    return pl.pallas_call(
        functools.partial(_matmul_kernel, act=act),
        grid=(M // tm, N // tn),
        in_specs=[pl.BlockSpec((tm, K), lambda i, j: (i, 0)),
                  w_spec,
                  pl.BlockSpec((1, tn), lambda i, j: (0, j)),
                  pl.BlockSpec((1, tn), lambda i, j: (0, j))],
        out_specs=pl.BlockSpec((tm, tn), lambda i, j: (i, j)),
        out_shape=jax.ShapeDtypeStruct((M, N), out_dtype),
        compiler_params=_params("parallel", "parallel"),
        name="proj_" + act,
    )(x, w_t, col_scale, col_bias)


def _forget_bias_kernel(x_ref, w_ref, b_ref, o_ref, carry_ref, *, heads):
    @pl.when(pl.program_id(1) == 0)
    def _():
        carry_ref[...] = jnp.zeros_like(carry_ref)

    ts = x_ref.shape[0]
    z = lax.dot_general(x_ref[...], w_ref[...], _NT_DIMS, preferred_element_type=jnp.float32) + b_ref[...]
    log_f = jnp.minimum(z, 0.0) - jnp.log1p(jnp.exp(-jnp.abs(z)))
    row = lax.broadcasted_iota(jnp.int32, (ts, ts), 0)
    col = lax.broadcasted_iota(jnp.int32, (ts, ts), 1)
    lower = jnp.where(col <= row, 1.0, 0.0).astype(jnp.bfloat16)
    c = carry_ref[...]
    for part in _split3_bf16(log_f):
        c = c + jnp.dot(lower, part.astype(jnp.bfloat16), preferred_element_type=jnp.float32)
    carry_ref[...] = c[ts - 1:ts, :]
    hi, mid, lo = _split3_bf16(c * (-LOG2E))
    lane = lax.broadcasted_iota(jnp.int32, (ts, LANES), 1)
    cols = jnp.where(lane < heads, hi,
                     jnp.where(lane < 2 * heads, pltpu.roll(mid, heads, 1),
                               jnp.where(lane < 3 * heads, pltpu.roll(lo, 2 * heads, 1), 0.0)))
    o_ref[0] = cols.astype(o_ref.dtype)


def _forget_bias_columns(x, w_f, b_forget, heads, batch, ts=512):
    M, D = x.shape
    S = M // batch
    assert 3 * heads <= LANES
    ts = min(ts, S)
    steps = S // ts
    return pl.pallas_call(
        functools.partial(_forget_bias_kernel, heads=heads),
        grid=(batch, steps),
        in_specs=[pl.BlockSpec((ts, D), lambda b, i: (b * steps + i, 0)),
                  pl.BlockSpec((LANES, D), lambda b, i: (0, 0)),
                  pl.BlockSpec((1, LANES), lambda b, i: (0, 0))],
        out_specs=pl.BlockSpec((1, ts, LANES), lambda b, i: (b, i, 0)),
        out_shape=jax.ShapeDtypeStruct((batch, S, LANES), jnp.bfloat16),
        scratch_shapes=[pltpu.VMEM((1, LANES), jnp.float32)],
        compiler_params=_params("parallel", "arbitrary"),
        name="forget_bias_columns",
    )(x, w_f, b_forget)


FOX_HEADS_PER_STEP = 4
MOBA_HEADS_PER_STEP = 4


def _softmax_tile(s, v_ext, m_ref, acc_ref):
    tk = s.shape[1]
    chunks = [s[:, c:c + LANES] for c in range(0, tk, LANES)]
    mx = functools.reduce(jnp.maximum, chunks)
    m_prev = m_ref[...]
    m_new = jnp.maximum(m_prev, jnp.max(mx, axis=-1, keepdims=True))
    alpha = jnp.exp2(m_prev - m_new)
    p = jnp.concatenate([jnp.exp2(ch - m_new) for ch in chunks], axis=1).astype(v_ext.dtype)
    pv = jnp.dot(p, v_ext, preferred_element_type=jnp.float32)
    acc_ref[...] = acc_ref[...] * jnp.concatenate([alpha, alpha], axis=1) + pv
    m_ref[...] = m_new


class _Tile(NamedTuple):
    qk: Callable[[], Any]
    post: Callable[[Any], Any]
    values: Callable[[], Any]


class _HeadTiles(NamedTuple):
    specials: Sequence[_Tile]
    qk: Callable[[Any], Any]
    values: Callable[[Any], Any]
    s_refs: Sequence[Any]
    m_ref: Any
    acc_ref: Any


def _pipelined_tiles(heads, n_rest):
    def fill(buf, logits_of):
        for hd in heads:
            s = logits_of(hd)
            hd.s_refs[buf][:, :s.shape[1]] = s

    def consume(buf, values_of, post=lambda hd, s: s):
        for hd in heads:
            v_ext = values_of(hd)
            s = hd.s_refs[buf][:, :v_ext.shape[0]]
            _softmax_tile(post(hd, s), v_ext, hd.m_ref, hd.acc_ref)

    n_special = len(heads[0].specials)
    fill(0, lambda hd: hd.specials[0].qk())
    for t in range(n_special):
        if t + 1 < n_special:
            fill((t + 1) % 2, lambda hd: hd.specials[t + 1].qk())
        else:
            fill((t + 1) % 2, lambda hd: hd.qk(0))
        consume(t % 2, lambda hd: hd.specials[t].values(), lambda hd, s: hd.specials[t].post(s))
    cur, nxt = n_special % 2, (n_special + 1) % 2

    def two_tiles(i, carry):
        fill(nxt, lambda hd: hd.qk(2 * i + 1))
        consume(cur, lambda hd: hd.values(2 * i))
        fill(cur, lambda hd: hd.qk(jnp.minimum(2 * i + 2, n_rest - 1)))
        consume(nxt, lambda hd: hd.values(2 * i + 1))
        return carry

    lax.fori_loop(0, n_rest // 2, two_tiles, 0)

    @pl.when(n_rest % 2 == 1)
    def _():
        consume(cur, lambda hd: hd.values(n_rest - 1))


def _init_softmax_state(m_ref, acc_ref):
    m_ref[...] = jnp.full_like(m_ref, NEG_BIG)
    acc_ref[...] = jnp.zeros_like(acc_ref)


def _finish_attention(acc_ref, z, dtype):
    dh = z.shape[1]
    return (acc_ref[:, :dh] / acc_ref[:, dh:] * z).astype(dtype)


def _fox_kernel(q_ref, k_ref, cx_ref, v_ref, z_ref, o_ref, kaug_ref, vext_ref, sa_ref, sb_ref, m_ref, acc_ref,
                *, heads):
    qi = pl.program_id(2)
    tq = q_ref.shape[1]
    dh = HEAD_DIM
    G = kaug_ref.shape[0]
    S = k_ref.shape[1]

    @pl.when(qi == 0)
    def _():
        for g in range(G):
            kaug_ref[g, :, :dh] = k_ref[0, :, g * dh:(g + 1) * dh]
            kaug_ref[g, :, dh:] = cx_ref[0]
            vext_ref[g, :, :dh] = v_ref[0, :, g * dh:(g + 1) * dh]
            vext_ref[g, :, dh:] = jnp.ones((S, dh), vext_ref.dtype)

    def rows(j):
        return pl.ds(pl.multiple_of(j * tq, tq), tq)

    def causal(s):
        row = lax.broadcasted_iota(jnp.int32, (tq, tq), 0)
        col = lax.broadcasted_iota(jnp.int32, (tq, tq), 1)
        return jnp.where(col <= row, s, NEG_BIG)

    lane = lax.broadcasted_iota(jnp.int32, (tq, LANES), 1)
    tiles = []
    for g in range(G):
        h = pl.program_id(1) * G + g
        pick = (lane == h) | (lane == heads + h) | (lane == 2 * heads + h)
        q_aug = jnp.concatenate([q_ref[0, :, g * dh:(g + 1) * dh],
                                 jnp.where(pick, 1.0, 0.0).astype(q_ref.dtype)], axis=1)
        _init_softmax_state(m_ref.at[g], acc_ref.at[g])

        def logits(j, g=g, q_aug=q_aug):
            return lax.dot_general(q_aug, kaug_ref[g, rows(j), :], _NT_DIMS, preferred_element_type=jnp.float32)

        def values(j, g=g):
            return vext_ref[g, rows(j), :]

        diagonal = _Tile(functools.partial(logits, qi), causal, functools.partial(values, qi))
        tiles.append(_HeadTiles([diagonal], logits, values, (sa_ref.at[g], sb_ref.at[g]), m_ref.at[g], acc_ref.at[g]))
    _pipelined_tiles(tiles, qi)
    for g in range(G):
        o_ref[0, :, g * dh:(g + 1) * dh] = _finish_attention(acc_ref.at[g], z_ref[0, :, g * dh:(g + 1) * dh],
                                                             o_ref.dtype)


def _fox_attention(qkv, cx, sz, H, col0, zcol0, tq=512):
    B, S, _ = qkv.shape
    tq = min(tq, S)
    Dh = HEAD_DIM
    G = FOX_HEADS_PER_STEP
    assert H % G == 0 and col0 % G == 0 and zcol0 % G == 0
    W = G * Dh
    return pl.pallas_call(
        functools.partial(_fox_kernel, heads=H),
        grid=(B, H // G, S // tq),
        in_specs=[pl.BlockSpec((1, tq, W), lambda b, h, i: (b, i, col0 // G + h)),
                  pl.BlockSpec((1, S, W), lambda b, h, i: (b, 0, (col0 + H) // G + h)),
                  pl.BlockSpec((1, S, LANES), lambda b, h, i: (b, 0, 0)),
                  pl.BlockSpec((1, S, W), lambda b, h, i: (b, 0, (col0 + 2 * H) // G + h)),
                  pl.BlockSpec((1, tq, W), lambda b, h, i: (b, i, zcol0 // G + h))],
        out_specs=pl.BlockSpec((1, tq, W), lambda b, h, i: (b, i, h)),
        out_shape=jax.ShapeDtypeStruct((B, S, H * Dh), jnp.bfloat16),
        scratch_shapes=[pltpu.VMEM((G, S, Dh + LANES), jnp.bfloat16), pltpu.VMEM((G, S, 2 * Dh), jnp.bfloat16),
                        pltpu.VMEM((G, tq, tq), jnp.float32), pltpu.VMEM((G, tq, tq), jnp.float32),
                        pltpu.VMEM((G, tq, LANES), jnp.float32), pltpu.VMEM((G, tq, 2 * Dh), jnp.float32)],
        compiler_params=_params("arbitrary", "arbitrary", "arbitrary"),
        name="fox_attention",
    )(qkv, qkv, cx, qkv, sz)


def _t5_bucket_np(dist):
    max_exact = T5_NUM_BUCKETS // 2
    d = np.maximum(dist, 1).astype(np.float32)
    ratio = (np.log(d / np.float32(max_exact)) / np.float32(math.log(T5_MAX_DISTANCE / max_exact))
             * np.float32(T5_NUM_BUCKETS - max_exact))
    large = np.minimum(max_exact + ratio.astype(np.int32), T5_NUM_BUCKETS - 1)
    return np.where(dist < max_exact, dist, large).astype(np.int32)


MOBA_Q_BLOCKS = 2


def _moba_query_columns(q, kmean, far_row, first_block, blk, nb):
    nbp, tq = kmean.shape[0], q.shape[0]
    km_hi = kmean.astype(jnp.bfloat16)
    km_lo = (kmean - km_hi.astype(jnp.float32)).astype(jnp.bfloat16)
    gate = (lax.dot_general(km_hi, q, _NT_DIMS, preferred_element_type=jnp.float32)
            + lax.dot_general(km_lo, q, _NT_DIMS, preferred_element_type=jnp.float32))
    n_idx = lax.broadcasted_iota(jnp.int32, (nbp, tq), 0)
    own = first_block + lax.broadcasted_iota(jnp.int32, (nbp, tq), 1) // blk
    past = n_idx < own
    gate = jnp.where(past, gate, NEG_BIG)
    beaten = jnp.zeros((nbp, tq), jnp.int32)
    for n in range(nb):
        g_n = gate[n:n + 1, :]
        wins = (g_n > gate) | ((g_n == gate) & (n < n_idx))
        beaten = beaten + jnp.where(wins & (n < own), 1, 0)
    selected = past & (beaten < MOBA_TOPK)
    far_bias = jnp.broadcast_to(far_row, (nbp, tq))
    term = jnp.where(selected, jnp.where(n_idx == own - 1, 0.0, far_bias), NEG_BIG)
    term = jnp.where(n_idx == own, 0.0, term)
    term_hi = term.astype(jnp.bfloat16).astype(jnp.float32)
    term_lo = jnp.where(selected, term - term_hi, 0.0)
    pieces = [term_hi[:nb], term_lo[:nb]]
    if 2 * nb < LANES:
        pieces.append(jnp.zeros((LANES - 2 * nb, tq), jnp.float32))
    return jnp.concatenate(pieces, axis=0).T.astype(q.dtype)


def _moba_kernel(q_ref, k_ref, v_ref, hot_ref, rbias_ref, far_ref, z_ref, o_ref,
                 kaug_ref, vext_ref, kmean_ref, t5_ref, sa_ref, sb_ref, m_ref, acc_ref, *, nb, blk):
    qi = pl.program_id(2)
    tq = q_ref.shape[1]
    dh = HEAD_DIM
    G = kaug_ref.shape[0]
    S = k_ref.shape[1]

    @pl.when(qi == 0)
    def _():
        for g in range(G):
            kaug_ref[g, :, :dh] = k_ref[0, :, g * dh:(g + 1) * dh]
            kaug_ref[g, :, dh:] = hot_ref[...]
            vext_ref[g, :, :dh] = v_ref[0, :, g * dh:(g + 1) * dh]
            vext_ref[g, :, dh:] = jnp.ones((S, dh), vext_ref.dtype)
            kmean_ref[g] = jnp.zeros(kmean_ref.shape[1:], kmean_ref.dtype)
            for n in range(nb):
                kb = k_ref[0, n * blk:(n + 1) * blk, g * dh:(g + 1) * dh].astype(jnp.float32)
                kmean_ref[g, n:n + 1, :] = jnp.sum(kb, axis=0, keepdims=True) * (1.0 / blk)
            rb = jnp.broadcast_to(rbias_ref[g], (tq, tq))
            t5_ref[g] = pltpu.roll(rb, 1, 1, stride=1, stride_axis=0)

    def rows(j):
        return pl.ds(pl.multiple_of(j * tq, tq), tq)

    def causal_with_bias(s, g):
        row = lax.broadcasted_iota(jnp.int32, (tq, tq), 0)
        col = lax.broadcasted_iota(jnp.int32, (tq, tq), 1)
        return jnp.where(col <= row, s + t5_ref[g], NEG_BIG)

    def previous_with_bias(s, g):
        top = jnp.concatenate([s[:blk, :tq - blk], s[:blk, tq - blk:] + t5_ref[g, tq - blk:, :blk]], axis=1)
        return jnp.concatenate([top, s[blk:, :]], axis=0)

    tiles = []
    for g in range(G):
        q = q_ref[0, :, g * dh:(g + 1) * dh]
        q_extra = _moba_query_columns(q, kmean_ref[g], far_ref[g], qi * (tq // blk), blk, nb)
        q_aug = jnp.concatenate([q, q_extra], axis=1)
        _init_softmax_state(m_ref.at[g], acc_ref.at[g])

        def logits(j, g=g, q_aug=q_aug):
            return lax.dot_general(q_aug, kaug_ref[g, rows(j), :], _NT_DIMS, preferred_element_type=jnp.float32)

        def values(j, g=g):
            return vext_ref[g, rows(j), :]

        diagonal = _Tile(functools.partial(logits, qi), functools.partial(causal_with_bias, g=g),
                         functools.partial(values, qi))
        previous = _Tile(functools.partial(logits, qi - 1), functools.partial(previous_with_bias, g=g),
                         functools.partial(values, qi - 1))
        tiles.append((diagonal, previous, logits, values, (sa_ref.at[g], sb_ref.at[g]), m_ref.at[g], acc_ref.at[g]))

    @pl.when(qi == 0)
    def _():
        _pipelined_tiles([_HeadTiles([t[0]], *t[2:]) for t in tiles], 0)

    @pl.when(qi >= 1)
    def _():
        _pipelined_tiles([_HeadTiles([t[0], t[1]], *t[2:]) for t in tiles], qi - 1)

    for g in range(G):
        o_ref[0, :, g * dh:(g + 1) * dh] = _finish_attention(acc_ref.at[g], z_ref[0, :, g * dh:(g + 1) * dh],
                                                             o_ref.dtype)


def _moba_attention(qkv, rel_bias_table, sz, H, col0, zcol0):
    B, S, _ = qkv.shape
    blk = MOBA_BLOCK
    tq = MOBA_Q_BLOCKS * blk
    assert MOBA_Q_BLOCKS == 2 and S % tq == 0
    nb = S // blk
    nbp = -(-nb // 8) * 8
    assert 2 * nb <= LANES
    Dh = HEAD_DIM
    G = MOBA_HEADS_PER_STEP
    assert H % G == 0 and col0 % G == 0 and zcol0 % G == 0
    W = G * Dh
    table_t = rel_bias_table.T.astype(jnp.float32) * LOG2E
    rbias = table_t[:, _t5_bucket_np(np.arange(tq)[::-1])][:, None, :]
    far_bucket = int(_t5_bucket_np(np.array([blk + 1]))[0])
    assert (_t5_bucket_np(np.arange(blk + 1, S + blk)) == far_bucket).all()
    far = jnp.broadcast_to(table_t[:, far_bucket][:, None, None], (H, 1, tq))
    key_block = np.arange(S)[:, None] // blk
    lane = np.arange(LANES)[None, :]
    hot = jnp.asarray((lane == key_block) | (lane == key_block + nb), jnp.bfloat16)
    once = pl.Buffered(1)
    return pl.pallas_call(
        functools.partial(_moba_kernel, nb=nb, blk=blk),
        grid=(B, H // G, S // tq),
        in_specs=[pl.BlockSpec((1, tq, W), lambda b, h, i: (b, i, col0 // G + h)),
                  pl.BlockSpec((1, S, W), lambda b, h, i: (b, 0, (col0 + H) // G + h), pipeline_mode=once),
                  pl.BlockSpec((1, S, W), lambda b, h, i: (b, 0, (col0 + 2 * H) // G + h), pipeline_mode=once),
                  pl.BlockSpec((S, LANES), lambda b, h, i: (0, 0), pipeline_mode=once),
                  pl.BlockSpec((G, 1, tq), lambda b, h, i: (h, 0, 0)),
                  pl.BlockSpec((G, 1, tq), lambda b, h, i: (h, 0, 0)),
                  pl.BlockSpec((1, tq, W), lambda b, h, i: (b, i, zcol0 // G + h))],
        out_specs=pl.BlockSpec((1, tq, W), lambda b, h, i: (b, i, h)),
        out_shape=jax.ShapeDtypeStruct((B, S, H * Dh), jnp.bfloat16),
        scratch_shapes=[pltpu.VMEM((G, S, Dh + LANES), jnp.bfloat16), pltpu.VMEM((G, S, 2 * Dh), jnp.bfloat16),
                        pltpu.VMEM((G, nbp, Dh), jnp.float32), pltpu.VMEM((G, tq, tq), jnp.float32),
                        pltpu.VMEM((G, tq, tq), jnp.float32), pltpu.VMEM((G, tq, tq), jnp.float32),
                        pltpu.VMEM((G, tq, LANES), jnp.float32), pltpu.VMEM((G, tq, 2 * Dh), jnp.float32)],
        compiler_params=_params("arbitrary", "arbitrary", "arbitrary"),
        name="moba_attention",
    )(qkv, qkv, qkv, hot, rbias, far, sz)


def _branch_merge_kernel(ya_ref, yf_ref, wa_ref, wf_ref, ga_ref, gf_ref, o_ref):
    ua = jnp.dot(ya_ref[...], wa_ref[0], preferred_element_type=jnp.float32)
    uf = jnp.dot(yf_ref[...], wf_ref[0], preferred_element_type=jnp.float32)
    o_ref[...] = (ga_ref[...] * ua + gf_ref[...] * uf).astype(o_ref.dtype)


def _branch_merge(ya, yf, w, gates, tm=512, tn=1024):
    M, W = ya.shape
    D = w.shape[2]
    tm, tn = min(tm, M), min(tn, D)
    nj = D // tn
    return pl.pallas_call(
        _branch_merge_kernel,
        grid=(M // tm, nj),
        in_specs=[pl.BlockSpec((tm, W), lambda i, j: (i, 0)),
                  pl.BlockSpec((tm, W), lambda i, j: (i, 0)),
                  pl.BlockSpec((1, W, tn), lambda i, j: (0, 0, j)),
                  pl.BlockSpec((1, W, tn), lambda i, j: (1, 0, j)),
                  pl.BlockSpec((tm, tn), lambda i, j: (i, j)),
                  pl.BlockSpec((tm, tn), lambda i, j: (i, nj + j))],
        out_specs=pl.BlockSpec((tm, tn), lambda i, j: (i, j)),
        out_shape=jax.ShapeDtypeStruct((M, D), jnp.bfloat16),
        compiler_params=_params("parallel", "parallel"),
        name="branch_merge",
    )(ya, yf, w, w, gates, gates)


LN_ROW_CHUNK = 64


def _out_layernorm_kernel(h_ref, w_ref, x_ref, g_ref, b_ref, o_ref):
    j = pl.program_id(1)
    tn = w_ref.shape[1]
    out = jnp.dot(h_ref[...], w_ref[...], preferred_element_type=jnp.float32)
    o_ref[:, pl.ds(pl.multiple_of(j * tn, tn), tn)] = DEEPNORM_ALPHA * x_ref[...] + out

    @pl.when(j == pl.num_programs(1) - 1)
    def _():
        rows = min(LN_ROW_CHUNK, o_ref.shape[0])

        def normalize_rows(i, carry):
            sl = pl.ds(pl.multiple_of(i * rows, rows), rows)
            r = o_ref[sl, :]
            mu = jnp.mean(r, axis=-1, keepdims=True)
            d = r - mu
            var = jnp.mean(d * d, axis=-1, keepdims=True)
            o_ref[sl, :] = d * lax.rsqrt(var + LN_EPS) * g_ref[...] + b_ref[...]
            return carry

        lax.fori_loop(0, o_ref.shape[0] // rows, normalize_rows, 0)


def _out_layernorm(h, w, x, gain, bias, tm=512, tn=1024):
    M, D = x.shape
    tm, tn = min(tm, M), min(tn, D)
    return pl.pallas_call(
        _out_layernorm_kernel,
        grid=(M // tm, D // tn),
        in_specs=[pl.BlockSpec((tm, D), lambda i, j: (i, 0)),
                  pl.BlockSpec((D, tn), lambda i, j: (0, j)),
                  pl.BlockSpec((tm, tn), lambda i, j: (i, j)),
                  pl.BlockSpec((1, D), lambda i, j: (0, 0)),
                  pl.BlockSpec((1, D), lambda i, j: (0, 0))],
        out_specs=pl.BlockSpec((tm, D), lambda i, j: (i, 0)),
        out_shape=jax.ShapeDtypeStruct((M, D), jnp.float32),
        compiler_params=_params("parallel", "arbitrary"),
        name="out_layernorm",
    )(h, w, x, gain, bias)


def kernel(x, w_in, b_forget, b_gate, rel_bias_table, w_branch, w_out, ln_gain, ln_bias):
    B, S, D = x.shape
    Ha, Hf = MOBA_HEADS, FOX_HEADS
    Wa, Wf = Ha * HEAD_DIM, Hf * HEAD_DIM
    bf16, f32 = jnp.bfloat16, jnp.float32
    depth = w_in.shape[0]
    for layer in range(depth):
        w_t = jnp.swapaxes(w_in[layer], 0, 1).astype(bf16)
        n_attn = 4 * Wa + 4 * Wf
        w_f = jnp.pad(w_t[n_attn:n_attn + Hf], ((0, LANES - Hf), (0, 0)))
        b_f = jnp.pad(b_forget[layer].astype(f32), (0, LANES - Hf))[None, :]
        xb = x.reshape(B * S, D).astype(bf16)

        scale = HEAD_DIM ** -0.5 * LOG2E
        qkv_scale = jnp.concatenate([jnp.full((Wa,), scale, f32), jnp.ones((2 * Wa,), f32),
                                     jnp.full((Wf,), scale, f32), jnp.ones((2 * Wf,), f32)])[None, :]
        tn = PROJ_TN
        assert Wa % tn == 0 and Wf % tn == 0
        a_blocks, f_blocks = Wa // tn, Wf // tn
        qkv = _matmul(xb, w_t, qkv_scale, jnp.zeros_like(qkv_scale), bf16, "none",
                      lambda j: jnp.where(j < 3 * a_blocks, j, j + a_blocks), tn=tn).reshape(B, S, -1)
        ones_z = jnp.ones((1, Wa + Wf), f32)
        sz = _matmul(xb, w_t, ones_z, jnp.zeros_like(ones_z), f32, "silu",
                     lambda j: jnp.where(j < a_blocks, j + 3 * a_blocks, j + 3 * a_blocks + 3 * f_blocks),
                     tn=tn).reshape(B, S, -1)
        gates = _matmul(xb, w_t, jnp.ones((1, 2 * D), f32), b_gate[layer].reshape(1, 2 * D).astype(f32),
                        f32, "sigmoid", w_row_start=n_attn + Hf, tn=tn)

        cx = _forget_bias_columns(xb, w_f, b_f, Hf, B)
        ya = _moba_attention(qkv, rel_bias_table, sz, Ha, 0, 0)
        yf = _fox_attention(qkv, cx, sz, Hf, 3 * Ha, Ha)

        assert Wa == Wf
        merged = _branch_merge(ya.reshape(B * S, Wa), yf.reshape(B * S, Wf), w_branch[layer].astype(bf16), gates)
        y = _out_layernorm(merged, w_out[layer].astype(bf16), x.reshape(B * S, D).astype(f32),
                           ln_gain[layer].reshape(1, D).astype(f32), ln_bias[layer].reshape(1, D).astype(f32))
        x = y.reshape(B, S, D).astype(x.dtype)
    return x
```

```python
import functools
import math
from typing import Any, Callable, NamedTuple, Sequence

import jax
import jax.numpy as jnp
import numpy as np
from jax import lax
from jax.experimental import pallas as pl
from jax.experimental.pallas import tpu as pltpu

HEAD_DIM = 128
LANES = 128
MOBA_HEADS = 16
FOX_HEADS = 16
MOBA_BLOCK = 256
MOBA_TOPK = 3
T5_NUM_BUCKETS = 32
T5_MAX_DISTANCE = 128
LN_EPS = 1e-5
DEPTH = 1
DEEPNORM_ALPHA = (2.0 * DEPTH) ** 0.25
LOG2E = math.log2(math.e)

V7X_VMEM_LIMIT_BYTES = 56 * 1024 * 1024
PROJ_TN = 1024
NEG_BIG = -1e30

_NT_DIMS = (((1,), (1,)), ((), ()))


def _params(*semantics):
    return pltpu.CompilerParams(dimension_semantics=semantics,
                                vmem_limit_bytes=V7X_VMEM_LIMIT_BYTES)


def _split3_bf16(a):
    hi = a.astype(jnp.bfloat16).astype(jnp.float32)
    r = a - hi
    mid = r.astype(jnp.bfloat16).astype(jnp.float32)
    lo = (r - mid).astype(jnp.bfloat16).astype(jnp.float32)
    return hi, mid, lo


def _matmul_kernel(x_ref, wt_ref, s_ref, b_ref, o_ref, *, act):
    acc = lax.dot_general(x_ref[...], wt_ref[...], _NT_DIMS, preferred_element_type=jnp.float32)
    acc = acc * s_ref[...] + b_ref[...]
    if act == "silu":
        acc = acc * jax.nn.sigmoid(acc)
    elif act == "sigmoid":
        acc = jax.nn.sigmoid(acc)
    o_ref[...] = acc.astype(o_ref.dtype)


def _matmul(x, w_t, col_scale, col_bias, out_dtype, act, w_row_block=lambda j: j, w_row_start=None,
            tm=1024, tn=1024):
    M, K = x.shape
    N = col_scale.shape[1]
    tm, tn = min(tm, M), min(tn, N)
    if w_row_start is None:
        w_spec = pl.BlockSpec((tn, K), lambda i, j: (w_row_block(j), 0))
    else:
        align = math.gcd(w_row_start, tn)
        w_spec = pl.BlockSpec((pl.Element(tn), pl.Element(K)),
                              lambda i, j: (pl.multiple_of(w_row_start + j * tn, align), 0))
    return pl.pallas_call(
        functools.partial(_matmul_kernel, act=act),
        grid=(M // tm, N // tn),
        in_specs=[pl.BlockSpec((tm, K), lambda i, j: (i, 0)),
                  w_spec,
                  pl.BlockSpec((1, tn), lambda i, j: (0, j)),
                  pl.BlockSpec((1, tn), lambda i, j: (0, j))],
        out_specs=pl.BlockSpec((tm, tn), lambda i, j: (i, j)),
        out_shape=jax.ShapeDtypeStruct((M, N), out_dtype),
        compiler_params=_params("parallel", "parallel"),
        name="proj_" + act,
    )(x, w_t, col_scale, col_bias)


def _forget_bias_kernel(x_ref, w_ref, b_ref, xb_ref, o_ref, carry_ref, *, heads):
    @pl.when(pl.program_id(1) == 0)
    def _():
        carry_ref[...] = jnp.zeros_like(carry_ref)

    ts = x_ref.shape[0]
    xb = x_ref[...].astype(xb_ref.dtype)
    xb_ref[...] = xb
    z = lax.dot_general(xb, w_ref[...], _NT_DIMS, preferred_element_type=jnp.float32) + b_ref[...]
    log_f = jnp.minimum(z, 0.0) - jnp.log1p(jnp.exp(-jnp.abs(z)))
    row = lax.broadcasted_iota(jnp.int32, (ts, ts), 0)
    col = lax.broadcasted_iota(jnp.int32, (ts, ts), 1)
    lower = jnp.where(col <= row, 1.0, 0.0).astype(jnp.bfloat16)
    c = carry_ref[...]
    for part in _split3_bf16(log_f):
        c = c + jnp.dot(lower, part.astype(jnp.bfloat16), preferred_element_type=jnp.float32)
    carry_ref[...] = c[ts - 1:ts, :]
    hi, mid, lo = _split3_bf16(c * (-LOG2E))
    lane = lax.broadcasted_iota(jnp.int32, (ts, LANES), 1)
    cols = jnp.where(lane < heads, hi,
                     jnp.where(lane < 2 * heads, pltpu.roll(mid, heads, 1),
                               jnp.where(lane < 3 * heads, pltpu.roll(lo, 2 * heads, 1), 0.0)))
    o_ref[0] = cols.astype(o_ref.dtype)


def _forget_bias_columns(x, w_f, b_forget, heads, batch, ts=512):
    M, D = x.shape
    S = M // batch
    assert 3 * heads <= LANES
    ts = min(ts, S)
    steps = S // ts
    return pl.pallas_call(
        functools.partial(_forget_bias_kernel, heads=heads),
        grid=(batch, steps),
        in_specs=[pl.BlockSpec((ts, D), lambda b, i: (b * steps + i, 0)),
                  pl.BlockSpec((LANES, D), lambda b, i: (0, 0)),
                  pl.BlockSpec((1, LANES), lambda b, i: (0, 0))],
        out_specs=[pl.BlockSpec((ts, D), lambda b, i: (b * steps + i, 0)),
                   pl.BlockSpec((1, ts, LANES), lambda b, i: (b, i, 0))],
        out_shape=[jax.ShapeDtypeStruct((M, D), jnp.bfloat16),
                   jax.ShapeDtypeStruct((batch, S, LANES), jnp.bfloat16)],
        scratch_shapes=[pltpu.VMEM((1, LANES), jnp.float32)],
        compiler_params=_params("parallel", "arbitrary"),
        name="forget_bias_columns",
    )(x, w_f, b_forget)


FOX_HEADS_PER_STEP = 4
MOBA_HEADS_PER_STEP = 4


def _softmax_tile(s, v_ext, m_ref, acc_ref):
    tk = s.shape[1]
    chunks = [s[:, c:c + LANES] for c in range(0, tk, LANES)]
    mx = functools.reduce(jnp.maximum, chunks)
    m_prev = m_ref[...]
    m_new = jnp.maximum(m_prev, jnp.max(mx, axis=-1, keepdims=True))
    alpha = jnp.exp2(m_prev - m_new)
    p = jnp.concatenate([jnp.exp2(ch - m_new) for ch in chunks], axis=1).astype(v_ext.dtype)
    pv = jnp.dot(p, v_ext, preferred_element_type=jnp.float32)
    acc_ref[...] = acc_ref[...] * jnp.concatenate([alpha, alpha], axis=1) + pv
    m_ref[...] = m_new


class _Tile(NamedTuple):
    qk: Callable[[], Any]
    post: Callable[[Any], Any]
    values: Callable[[], Any]


class _HeadTiles(NamedTuple):
    specials: Sequence[_Tile]
    qk: Callable[[Any], Any]
    values: Callable[[Any], Any]
    s_refs: Sequence[Any]
    m_ref: Any
    acc_ref: Any


def _pipelined_tiles(heads, n_rest):
    def fill(buf, logits_of):
        for hd in heads:
            s = logits_of(hd)
            hd.s_refs[buf][:, :s.shape[1]] = s

    def consume(buf, values_of, post=lambda hd, s: s):
        for hd in heads:
            v_ext = values_of(hd)
            s = hd.s_refs[buf][:, :v_ext.shape[0]]
            _softmax_tile(post(hd, s), v_ext, hd.m_ref, hd.acc_ref)

    n_special = len(heads[0].specials)
    no_uniform_tiles = isinstance(n_rest, int) and n_rest == 0
    fill(0, lambda hd: hd.specials[0].qk())
    for t in range(n_special):
        if t + 1 < n_special:
            fill((t + 1) % 2, lambda hd: hd.specials[t + 1].qk())
        elif not no_uniform_tiles:
            fill((t + 1) % 2, lambda hd: hd.qk(0))
        consume(t % 2, lambda hd: hd.specials[t].values(), lambda hd, s: hd.specials[t].post(s))
    if no_uniform_tiles:
        return
    cur, nxt = n_special % 2, (n_special + 1) % 2

    def two_tiles(i, carry):
        fill(nxt, lambda hd: hd.qk(2 * i + 1))
        consume(cur, lambda hd: hd.values(2 * i))
        fill(cur, lambda hd: hd.qk(jnp.minimum(2 * i + 2, n_rest - 1)))
        consume(nxt, lambda hd: hd.values(2 * i + 1))
        return carry

    lax.fori_loop(0, n_rest // 2, two_tiles, 0)

    @pl.when(n_rest % 2 == 1)
    def _():
        consume(cur, lambda hd: hd.values(n_rest - 1))


def _init_softmax_state(m_ref, acc_ref):
    m_ref[...] = jnp.full_like(m_ref, NEG_BIG)
    acc_ref[...] = jnp.zeros_like(acc_ref)


def _finish_attention(acc_ref, z, dtype):
    dh = z.shape[1]
    return (acc_ref[:, :dh] / acc_ref[:, dh:] * z).astype(dtype)


def _fox_kernel(q_ref, k_ref, cx_ref, v_ref, z_ref, o_ref, kaug_ref, vext_ref, sa_ref, sb_ref, m_ref, acc_ref,
                *, heads):
    qi = pl.program_id(2)
    tq = q_ref.shape[1]
    dh = HEAD_DIM
    G = kaug_ref.shape[0]
    S = k_ref.shape[1]

    @pl.when(qi == 0)
    def _():
        for g in range(G):
            kaug_ref[g, :, :dh] = k_ref[0, :, g * dh:(g + 1) * dh]
            kaug_ref[g, :, dh:] = cx_ref[0]
            vext_ref[g, :, :dh] = v_ref[0, :, g * dh:(g + 1) * dh]
            vext_ref[g, :, dh:] = jnp.ones((S, dh), vext_ref.dtype)

    def rows(j):
        return pl.ds(pl.multiple_of(j * tq, tq), tq)

    def causal(s):
        row = lax.broadcasted_iota(jnp.int32, (tq, tq), 0)
        col = lax.broadcasted_iota(jnp.int32, (tq, tq), 1)
        return jnp.where(col <= row, s, NEG_BIG)

    lane = lax.broadcasted_iota(jnp.int32, (tq, LANES), 1)
    tiles = []
    for g in range(G):
        h = pl.program_id(1) * G + g
        pick = (lane == h) | (lane == heads + h) | (lane == 2 * heads + h)
        q_aug = jnp.concatenate([q_ref[0, :, g * dh:(g + 1) * dh],
                                 jnp.where(pick, 1.0, 0.0).astype(q_ref.dtype)], axis=1)
        _init_softmax_state(m_ref.at[g], acc_ref.at[g])

        def logits(j, g=g, q_aug=q_aug):
            return lax.dot_general(q_aug, kaug_ref[g, rows(j), :], _NT_DIMS, preferred_element_type=jnp.float32)

        def values(j, g=g):
            return vext_ref[g, rows(j), :]

        diagonal = _Tile(functools.partial(logits, qi), causal, functools.partial(values, qi))
        tiles.append(_HeadTiles([diagonal], logits, values, (sa_ref.at[g], sb_ref.at[g]), m_ref.at[g], acc_ref.at[g]))
    @pl.when(qi == 0)
    def _():
        _pipelined_tiles(tiles, 0)

    @pl.when(qi >= 1)
    def _():
        _pipelined_tiles(tiles, qi)

    for g in range(G):
        o_ref[0, :, g * dh:(g + 1) * dh] = _finish_attention(acc_ref.at[g], z_ref[0, :, g * dh:(g + 1) * dh],
                                                             o_ref.dtype)


def _fox_attention(qkv, cx, sz, H, col0, zcol0, tq=512):
    B, S, _ = qkv.shape
    tq = min(tq, S)
    Dh = HEAD_DIM
    G = FOX_HEADS_PER_STEP
    assert H % G == 0 and col0 % G == 0 and zcol0 % G == 0
    W = G * Dh
    once = pl.Buffered(1)
    return pl.pallas_call(
        functools.partial(_fox_kernel, heads=H),
        grid=(B, H // G, S // tq),
        in_specs=[pl.BlockSpec((1, tq, W), lambda b, h, i: (b, i, col0 // G + h)),
                  pl.BlockSpec((1, S, W), lambda b, h, i: (b, 0, (col0 + H) // G + h), pipeline_mode=once),
                  pl.BlockSpec((1, S, LANES), lambda b, h, i: (b, 0, 0)),
                  pl.BlockSpec((1, S, W), lambda b, h, i: (b, 0, (col0 + 2 * H) // G + h), pipeline_mode=once),
                  pl.BlockSpec((1, tq, W), lambda b, h, i: (b, i, zcol0 // G + h))],
        out_specs=pl.BlockSpec((1, tq, W), lambda b, h, i: (b, i, h)),
        out_shape=jax.ShapeDtypeStruct((B, S, H * Dh), jnp.bfloat16),
        scratch_shapes=[pltpu.VMEM((G, S, Dh + LANES), jnp.bfloat16), pltpu.VMEM((G, S, 2 * Dh), jnp.bfloat16),
                        pltpu.VMEM((G, tq, tq), jnp.float32), pltpu.VMEM((G, tq, tq), jnp.float32),
                        pltpu.VMEM((G, tq, LANES), jnp.float32), pltpu.VMEM((G, tq, 2 * Dh), jnp.float32)],
        compiler_params=_params("arbitrary", "arbitrary", "arbitrary"),
        name="fox_attention",
    )(qkv, qkv, cx, qkv, sz)


def _t5_bucket_np(dist):
    max_exact = T5_NUM_BUCKETS // 2
    d = np.maximum(dist, 1).astype(np.float32)
    ratio = (np.log(d / np.float32(max_exact)) / np.float32(math.log(T5_MAX_DISTANCE / max_exact))
             * np.float32(T5_NUM_BUCKETS - max_exact))
    large = np.minimum(max_exact + ratio.astype(np.int32), T5_NUM_BUCKETS - 1)
    return np.where(dist < max_exact, dist, large).astype(np.int32)


MOBA_Q_BLOCKS = 2


def _moba_query_columns(q, kmean, far_row, first_block, blk, nb):
    nbp, tq = kmean.shape[0], q.shape[0]
    km_hi = kmean.astype(jnp.bfloat16)
    km_lo = (kmean - km_hi.astype(jnp.float32)).astype(jnp.bfloat16)
    gate = (lax.dot_general(km_hi, q, _NT_DIMS, preferred_element_type=jnp.float32)
            + lax.dot_general(km_lo, q, _NT_DIMS, preferred_element_type=jnp.float32))
    n_idx = lax.broadcasted_iota(jnp.int32, (nbp, tq), 0)
    own = first_block + lax.broadcasted_iota(jnp.int32, (nbp, tq), 1) // blk
    past = n_idx < own
    gate = jnp.where(past, gate, NEG_BIG)
    beaten = jnp.zeros((nbp, tq), jnp.int32)
    for n in range(nb):
        g_n = gate[n:n + 1, :]
        wins = (g_n > gate) | ((g_n == gate) & (n < n_idx))
        beaten = beaten + jnp.where(wins & (n < own), 1, 0)
    selected = past & (beaten < MOBA_TOPK)
    far_bias = jnp.broadcast_to(far_row, (nbp, tq))
    term = jnp.where(selected, jnp.where(n_idx == own - 1, 0.0, far_bias), NEG_BIG)
    term = jnp.where(n_idx == own, 0.0, term)
    term_hi = term.astype(jnp.bfloat16).astype(jnp.float32)
    term_lo = jnp.where(selected, term - term_hi, 0.0)
    pieces = [term_hi[:nb], term_lo[:nb]]
    if 2 * nb < LANES:
        pieces.append(jnp.zeros((LANES - 2 * nb, tq), jnp.float32))
    return jnp.concatenate(pieces, axis=0).T.astype(q.dtype)


def _moba_kernel(q_ref, k_ref, v_ref, hot_ref, rbias_ref, far_ref, z_ref, o_ref,
                 kaug_ref, vext_ref, kmean_ref, t5_ref, sa_ref, sb_ref, m_ref, acc_ref, *, nb, blk):
    qi = pl.program_id(2)
    tq = q_ref.shape[1]
    dh = HEAD_DIM
    G = kaug_ref.shape[0]
    S = k_ref.shape[1]

    @pl.when(qi == 0)
    def _():
        for g in range(G):
            kaug_ref[g, :, :dh] = k_ref[0, :, g * dh:(g + 1) * dh]
            kaug_ref[g, :, dh:] = hot_ref[...]
            vext_ref[g, :, :dh] = v_ref[0, :, g * dh:(g + 1) * dh]
            vext_ref[g, :, dh:] = jnp.ones((S, dh), vext_ref.dtype)
            kmean_ref[g] = jnp.zeros(kmean_ref.shape[1:], kmean_ref.dtype)
            for n in range(nb):
                kb = k_ref[0, n * blk:(n + 1) * blk, g * dh:(g + 1) * dh].astype(jnp.float32)
                kmean_ref[g, n:n + 1, :] = jnp.sum(kb, axis=0, keepdims=True) * (1.0 / blk)
            rb = jnp.broadcast_to(rbias_ref[g], (tq, tq))
            t5_ref[g] = pltpu.roll(rb, 1, 1, stride=1, stride_axis=0)

    def rows(j):
        return pl.ds(pl.multiple_of(j * tq, tq), tq)

    def causal_with_bias(s, g):
        row = lax.broadcasted_iota(jnp.int32, (tq, tq), 0)
        col = lax.broadcasted_iota(jnp.int32, (tq, tq), 1)
        return jnp.where(col <= row, s + t5_ref[g], NEG_BIG)

    def previous_with_bias(s, g):
        top = jnp.concatenate([s[:blk, :tq - blk], s[:blk, tq - blk:] + t5_ref[g, tq - blk:, :blk]], axis=1)
        return jnp.concatenate([top, s[blk:, :]], axis=0)

    tiles = []
    for g in range(G):
        q = q_ref[0, :, g * dh:(g + 1) * dh]
        q_extra = _moba_query_columns(q, kmean_ref[g], far_ref[g], qi * (tq // blk), blk, nb)
        q_aug = jnp.concatenate([q, q_extra], axis=1)
        _init_softmax_state(m_ref.at[g], acc_ref.at[g])

        def logits(j, g=g, q_aug=q_aug):
            return lax.dot_general(q_aug, kaug_ref[g, rows(j), :], _NT_DIMS, preferred_element_type=jnp.float32)

        def values(j, g=g):
            return vext_ref[g, rows(j), :]

        diagonal = _Tile(functools.partial(logits, qi), functools.partial(causal_with_bias, g=g),
                         functools.partial(values, qi))
        previous = _Tile(functools.partial(logits, qi - 1), functools.partial(previous_with_bias, g=g),
                         functools.partial(values, qi - 1))
        tiles.append((diagonal, previous, logits, values, (sa_ref.at[g], sb_ref.at[g]), m_ref.at[g], acc_ref.at[g]))

    @pl.when(qi == 0)
    def _():
        _pipelined_tiles([_HeadTiles([t[0]], *t[2:]) for t in tiles], 0)

    @pl.when(qi >= 1)
    def _():
        _pipelined_tiles([_HeadTiles([t[0], t[1]], *t[2:]) for t in tiles], qi - 1)

    for g in range(G):
        o_ref[0, :, g * dh:(g + 1) * dh] = _finish_attention(acc_ref.at[g], z_ref[0, :, g * dh:(g + 1) * dh],
                                                             o_ref.dtype)


def _moba_attention(qkv, rel_bias_table, sz, H, col0, zcol0):
    B, S, _ = qkv.shape
    blk = MOBA_BLOCK
    tq = MOBA_Q_BLOCKS * blk
    assert MOBA_Q_BLOCKS == 2 and S % tq == 0
    nb = S // blk
    nbp = -(-nb // 8) * 8
    assert 2 * nb <= LANES
    Dh = HEAD_DIM
    G = MOBA_HEADS_PER_STEP
    assert H % G == 0 and col0 % G == 0 and zcol0 % G == 0
    W = G * Dh
    table_t = rel_bias_table.T.astype(jnp.float32) * LOG2E
    rbias = table_t[:, _t5_bucket_np(np.arange(tq)[::-1])][:, None, :]
    far_bucket = int(_t5_bucket_np(np.array([blk + 1]))[0])
    assert (_t5_bucket_np(np.arange(blk + 1, S + blk)) == far_bucket).all()
    far = jnp.broadcast_to(table_t[:, far_bucket][:, None, None], (H, 1, tq))
    key_block = np.arange(S)[:, None] // blk
    lane = np.arange(LANES)[None, :]
    hot = jnp.asarray((lane == key_block) | (lane == key_block + nb), jnp.bfloat16)
    once = pl.Buffered(1)
    return pl.pallas_call(
        functools.partial(_moba_kernel, nb=nb, blk=blk),
        grid=(B, H // G, S // tq),
        in_specs=[pl.BlockSpec((1, tq, W), lambda b, h, i: (b, i, col0 // G + h)),
                  pl.BlockSpec((1, S, W), lambda b, h, i: (b, 0, (col0 + H) // G + h), pipeline_mode=once),
                  pl.BlockSpec((1, S, W), lambda b, h, i: (b, 0, (col0 + 2 * H) // G + h), pipeline_mode=once),
                  pl.BlockSpec((S, LANES), lambda b, h, i: (0, 0), pipeline_mode=once),
                  pl.BlockSpec((G, 1, tq), lambda b, h, i: (h, 0, 0)),
                  pl.BlockSpec((G, 1, tq), lambda b, h, i: (h, 0, 0)),
                  pl.BlockSpec((1, tq, W), lambda b, h, i: (b, i, zcol0 // G + h))],
        out_specs=pl.BlockSpec((1, tq, W), lambda b, h, i: (b, i, h)),
        out_shape=jax.ShapeDtypeStruct((B, S, H * Dh), jnp.bfloat16),
        scratch_shapes=[pltpu.VMEM((G, S, Dh + LANES), jnp.bfloat16), pltpu.VMEM((G, S, 2 * Dh), jnp.bfloat16),
                        pltpu.VMEM((G, nbp, Dh), jnp.float32), pltpu.VMEM((G, tq, tq), jnp.float32),
                        pltpu.VMEM((G, tq, tq), jnp.float32), pltpu.VMEM((G, tq, tq), jnp.float32),
                        pltpu.VMEM((G, tq, LANES), jnp.float32), pltpu.VMEM((G, tq, 2 * Dh), jnp.float32)],
        compiler_params=_params("arbitrary", "arbitrary", "arbitrary"),
        name="moba_attention",
    )(qkv, qkv, qkv, hot, rbias, far, sz)


def _branch_merge_kernel(ya_ref, yf_ref, wa_ref, wf_ref, ga_ref, gf_ref, o_ref):
    ua = jnp.dot(ya_ref[...], wa_ref[0], preferred_element_type=jnp.float32)
    uf = jnp.dot(yf_ref[...], wf_ref[0], preferred_element_type=jnp.float32)
    o_ref[...] = (ga_ref[...] * ua + gf_ref[...] * uf).astype(o_ref.dtype)


def _branch_merge(ya, yf, w, gates, tm=1024, tn=512):
    M, W = ya.shape
    D = w.shape[2]
    tm, tn = min(tm, M), min(tn, D)
    nj = D // tn
    return pl.pallas_call(
        _branch_merge_kernel,
        grid=(M // tm, nj),
        in_specs=[pl.BlockSpec((tm, W), lambda i, j: (i, 0)),
                  pl.BlockSpec((tm, W), lambda i, j: (i, 0)),
                  pl.BlockSpec((1, W, tn), lambda i, j: (0, 0, j)),
                  pl.BlockSpec((1, W, tn), lambda i, j: (1, 0, j)),
                  pl.BlockSpec((tm, tn), lambda i, j: (i, j)),
                  pl.BlockSpec((tm, tn), lambda i, j: (i, nj + j))],
        out_specs=pl.BlockSpec((tm, tn), lambda i, j: (i, j)),
        out_shape=jax.ShapeDtypeStruct((M, D), jnp.bfloat16),
        compiler_params=_params("parallel", "parallel"),
        name="branch_merge",
    )(ya, yf, w, w, gates, gates)


LN_ROW_CHUNK = 64


def _out_layernorm_kernel(h_ref, w_ref, x_ref, g_ref, b_ref, o_ref):
    j = pl.program_id(1)
    tn = w_ref.shape[1]
    out = jnp.dot(h_ref[...], w_ref[...], preferred_element_type=jnp.float32)
    o_ref[:, pl.ds(pl.multiple_of(j * tn, tn), tn)] = DEEPNORM_ALPHA * x_ref[...] + out

    @pl.when(j == pl.num_programs(1) - 1)
    def _():
        rows = min(LN_ROW_CHUNK, o_ref.shape[0])

        def normalize_rows(i, carry):
            sl = pl.ds(pl.multiple_of(i * rows, rows), rows)
            r = o_ref[sl, :]
            mu = jnp.mean(r, axis=-1, keepdims=True)
            d = r - mu
            var = jnp.mean(d * d, axis=-1, keepdims=True)
            o_ref[sl, :] = d * lax.rsqrt(var + LN_EPS) * g_ref[...] + b_ref[...]
            return carry

        lax.fori_loop(0, o_ref.shape[0] // rows, normalize_rows, 0)


def _out_layernorm(h, w, x, gain, bias, tm=512, tn=1024):
    M, D = x.shape
    tm, tn = min(tm, M), min(tn, D)
    return pl.pallas_call(
        _out_layernorm_kernel,
        grid=(M // tm, D // tn),
        in_specs=[pl.BlockSpec((tm, D), lambda i, j: (i, 0)),
                  pl.BlockSpec((D, tn), lambda i, j: (0, j)),
                  pl.BlockSpec((tm, tn), lambda i, j: (i, j)),
                  pl.BlockSpec((1, D), lambda i, j: (0, 0)),
                  pl.BlockSpec((1, D), lambda i, j: (0, 0))],
        out_specs=pl.BlockSpec((tm, D), lambda i, j: (i, 0)),
        out_shape=jax.ShapeDtypeStruct((M, D), jnp.float32),
        compiler_params=_params("parallel", "arbitrary"),
        name="out_layernorm",
    )(h, w, x, gain, bias)


def kernel(x, w_in, b_forget, b_gate, rel_bias_table, w_branch, w_out, ln_gain, ln_bias):
    B, S, D = x.shape
    Ha, Hf = MOBA_HEADS, FOX_HEADS
    Wa, Wf = Ha * HEAD_DIM, Hf * HEAD_DIM
    bf16, f32 = jnp.bfloat16, jnp.float32
    depth = w_in.shape[0]
    for layer in range(depth):
        w_t = jnp.swapaxes(w_in[layer], 0, 1).astype(bf16)
        n_attn = 4 * Wa + 4 * Wf
        w_f = jnp.pad(w_t[n_attn:n_attn + Hf], ((0, LANES - Hf), (0, 0)))
        b_f = jnp.pad(b_forget[layer].astype(f32), (0, LANES - Hf))[None, :]
        x2 = x.reshape(B * S, D).astype(f32)
        xb, cx = _forget_bias_columns(x2, w_f, b_f, Hf, B)

        scale = HEAD_DIM ** -0.5 * LOG2E
        qkv_scale = jnp.concatenate([jnp.full((Wa,), scale, f32), jnp.ones((2 * Wa,), f32),
                                     jnp.full((Wf,), scale, f32), jnp.ones((2 * Wf,), f32)])[None, :]
        tn = PROJ_TN
        assert Wa % tn == 0 and Wf % tn == 0
        a_blocks, f_blocks = Wa // tn, Wf // tn
        qkv = _matmul(xb, w_t, qkv_scale, jnp.zeros_like(qkv_scale), bf16, "none",
                      lambda j: jnp.where(j < 3 * a_blocks, j, j + a_blocks), tn=tn).reshape(B, S, -1)
        ones_z = jnp.ones((1, Wa + Wf), f32)
        sz = _matmul(xb, w_t, ones_z, jnp.zeros_like(ones_z), f32, "silu",
                     lambda j: jnp.where(j < a_blocks, j + 3 * a_blocks, j + 3 * a_blocks + 3 * f_blocks),
                     tn=tn).reshape(B, S, -1)
        gates = _matmul(xb, w_t, jnp.ones((1, 2 * D), f32), b_gate[layer].reshape(1, 2 * D).astype(f32),
                        f32, "sigmoid", w_row_start=n_attn + Hf, tn=tn)

        ya = _moba_attention(qkv, rel_bias_table, sz, Ha, 0, 0)
        yf = _fox_attention(qkv, cx, sz, Hf, 3 * Ha, Ha)

        assert Wa == Wf
        merged = _branch_merge(ya.reshape(B * S, Wa), yf.reshape(B * S, Wf), w_branch[layer].astype(bf16), gates)
        y = _out_layernorm(merged, w_out[layer].astype(bf16), x2,
                           ln_gain[layer].reshape(1, D).astype(f32), ln_bias[layer].reshape(1, D).astype(f32))
        x = y.reshape(B, S, D).astype(x.dtype)
    return x
```

```python
import functools
import math
from typing import Any, Callable, NamedTuple, Sequence

import jax
import jax.numpy as jnp
import numpy as np
from jax import lax
from jax.experimental import pallas as pl
from jax.experimental.pallas import tpu as pltpu

HEAD_DIM = 128
LANES = 128
MOBA_HEADS = 16
FOX_HEADS = 16
MOBA_BLOCK = 256
MOBA_TOPK = 3
T5_NUM_BUCKETS = 32
T5_MAX_DISTANCE = 128
LN_EPS = 1e-5
DEPTH = 1
DEEPNORM_ALPHA = (2.0 * DEPTH) ** 0.25
LOG2E = math.log2(math.e)

V7X_VMEM_LIMIT_BYTES = 56 * 1024 * 1024
PROJ_TN = 1024
NEG_BIG = -1e30

_NT_DIMS = (((1,), (1,)), ((), ()))


def _params(*semantics):
    return pltpu.CompilerParams(dimension_semantics=semantics,
                                vmem_limit_bytes=V7X_VMEM_LIMIT_BYTES)


def _split3_bf16(a):
    hi = a.astype(jnp.bfloat16).astype(jnp.float32)
    r = a - hi
    mid = r.astype(jnp.bfloat16).astype(jnp.float32)
    lo = (r - mid).astype(jnp.bfloat16).astype(jnp.float32)
    return hi, mid, lo


def _matmul_kernel(x_ref, wt_ref, s_ref, b_ref, o_ref, *, act):
    acc = lax.dot_general(x_ref[...], wt_ref[...], _NT_DIMS, preferred_element_type=jnp.float32)
    acc = acc * s_ref[...] + b_ref[...]
    if act == "silu":
        acc = acc * jax.nn.sigmoid(acc)
    elif act == "sigmoid":
        acc = jax.nn.sigmoid(acc)
    o_ref[...] = acc.astype(o_ref.dtype)


def _matmul(x, w_t, col_scale, col_bias, out_dtype, act, w_row_block=lambda j: j, w_row_start=None,
            tm=1024, tn=1024):
    M, K = x.shape
    N = col_scale.shape[1]
    tm, tn = min(tm, M), min(tn, N)
    if w_row_start is None:
        w_spec = pl.BlockSpec((tn, K), lambda i, j: (w_row_block(j), 0))
    else:
        align = math.gcd(w_row_start, tn)
        w_spec = pl.BlockSpec((pl.Element(tn), pl.Element(K)),
                              lambda i, j: (pl.multiple_of(w_row_start + j * tn, align), 0))
    return pl.pallas_call(
        functools.partial(_matmul_kernel, act=act),
        grid=(M // tm, N // tn),
        in_specs=[pl.BlockSpec((tm, K), lambda i, j: (i, 0)),
                  w_spec,
                  pl.BlockSpec((1, tn), lambda i, j: (0, j)),
                  pl.BlockSpec((1, tn), lambda i, j: (0, j))],
        out_specs=pl.BlockSpec((tm, tn), lambda i, j: (i, j)),
        out_shape=jax.ShapeDtypeStruct((M, N), out_dtype),
        compiler_params=_params("parallel", "parallel"),
        name="proj_" + act,
    )(x, w_t, col_scale, col_bias)


def _forget_bias_kernel(x_ref, w_ref, b_ref, xb_ref, o_ref, carry_ref, *, heads):
    @pl.when(pl.program_id(1) == 0)
    def _():
        carry_ref[...] = jnp.zeros_like(carry_ref)

    ts = x_ref.shape[0]
    xb = x_ref[...].astype(xb_ref.dtype)
    xb_ref[...] = xb
    z = lax.dot_general(xb, w_ref[...], _NT_DIMS, preferred_element_type=jnp.float32) + b_ref[...]
    log_f = jnp.minimum(z, 0.0) - jnp.log1p(jnp.exp(-jnp.abs(z)))
    row = lax.broadcasted_iota(jnp.int32, (ts, ts), 0)
    col = lax.broadcasted_iota(jnp.int32, (ts, ts), 1)
    lower = jnp.where(col <= row, 1.0, 0.0).astype(jnp.bfloat16)
    c = carry_ref[...]
    for part in _split3_bf16(log_f):
        c = c + jnp.dot(lower, part.astype(jnp.bfloat16), preferred_element_type=jnp.float32)
    carry_ref[...] = c[ts - 1:ts, :]
    hi, mid, lo = _split3_bf16(c * (-LOG2E))
    lane = lax.broadcasted_iota(jnp.int32, (ts, LANES), 1)
    cols = jnp.where(lane < heads, hi,
                     jnp.where(lane < 2 * heads, pltpu.roll(mid, heads, 1),
                               jnp.where(lane < 3 * heads, pltpu.roll(lo, 2 * heads, 1), 0.0)))
    o_ref[0] = cols.astype(o_ref.dtype)


def _forget_bias_columns(x, w_f, b_forget, heads, batch, ts=512):
    M, D = x.shape
    S = M // batch
    assert 3 * heads <= LANES
    ts = min(ts, S)
    steps = S // ts
    return pl.pallas_call(
        functools.partial(_forget_bias_kernel, heads=heads),
        grid=(batch, steps),
        in_specs=[pl.BlockSpec((ts, D), lambda b, i: (b * steps + i, 0)),
                  pl.BlockSpec((LANES, D), lambda b, i: (0, 0)),
                  pl.BlockSpec((1, LANES), lambda b, i: (0, 0))],
        out_specs=[pl.BlockSpec((ts, D), lambda b, i: (b * steps + i, 0)),
                   pl.BlockSpec((1, ts, LANES), lambda b, i: (b, i, 0))],
        out_shape=[jax.ShapeDtypeStruct((M, D), jnp.bfloat16),
                   jax.ShapeDtypeStruct((batch, S, LANES), jnp.bfloat16)],
        scratch_shapes=[pltpu.VMEM((1, LANES), jnp.float32)],
        compiler_params=_params("parallel", "arbitrary"),
        name="forget_bias_columns",
    )(x, w_f, b_forget)


ATTN_HEADS_PER_STEP = 4


def _softmax_tile(s, v_ext, m_ref, acc_ref):
    tk = s.shape[1]
    chunks = [s[:, c:c + LANES] for c in range(0, tk, LANES)]
    mx = functools.reduce(jnp.maximum, chunks)
    m_prev = m_ref[...]
    m_new = jnp.maximum(m_prev, jnp.max(mx, axis=-1, keepdims=True))
    alpha = jnp.exp2(m_prev - m_new)
    p = jnp.concatenate([jnp.exp2(ch - m_new) for ch in chunks], axis=1).astype(v_ext.dtype)
    pv = jnp.dot(p, v_ext, preferred_element_type=jnp.float32)
    acc_ref[...] = acc_ref[...] * jnp.concatenate([alpha, alpha], axis=1) + pv
    m_ref[...] = m_new


class _Tile(NamedTuple):
    qk: Callable[[], Any]
    post: Callable[[Any], Any]
    values: Callable[[], Any]


class _HeadTiles(NamedTuple):
    specials: Sequence[_Tile]
    qk: Callable[[Any], Any]
    values: Callable[[Any], Any]
    s_refs: Sequence[Any]
    m_ref: Any
    acc_ref: Any


def _pipelined_tiles(heads, n_rest):
    def fill(buf, logits_of):
        for hd in heads:
            s = logits_of(hd)
            hd.s_refs[buf][:, :s.shape[1]] = s

    def consume(buf, values_of, post=lambda hd, s: s):
        for hd in heads:
            v_ext = values_of(hd)
            s = hd.s_refs[buf][:, :v_ext.shape[0]]
            _softmax_tile(post(hd, s), v_ext, hd.m_ref, hd.acc_ref)

    n_special = len(heads[0].specials)
    no_uniform_tiles = isinstance(n_rest, int) and n_rest == 0
    fill(0, lambda hd: hd.specials[0].qk())
    for t in range(n_special):
        if t + 1 < n_special:
            fill((t + 1) % 2, lambda hd: hd.specials[t + 1].qk())
        elif not no_uniform_tiles:
            fill((t + 1) % 2, lambda hd: hd.qk(0))
        consume(t % 2, lambda hd: hd.specials[t].values(), lambda hd, s: hd.specials[t].post(s))
    if no_uniform_tiles:
        return
    cur, nxt = n_special % 2, (n_special + 1) % 2

    def two_tiles(i, carry):
        fill(nxt, lambda hd: hd.qk(2 * i + 1))
        consume(cur, lambda hd: hd.values(2 * i))
        fill(cur, lambda hd: hd.qk(jnp.minimum(2 * i + 2, n_rest - 1)))
        consume(nxt, lambda hd: hd.values(2 * i + 1))
        return carry

    lax.fori_loop(0, n_rest // 2, two_tiles, 0)

    @pl.when(n_rest % 2 == 1)
    def _():
        consume(cur, lambda hd: hd.values(n_rest - 1))


def _init_softmax_state(m_ref, acc_ref):
    m_ref[...] = jnp.full_like(m_ref, NEG_BIG)
    acc_ref[...] = jnp.zeros_like(acc_ref)


def _finish_attention(acc_ref, z, dtype):
    dh = z.shape[1]
    return (acc_ref[:, :dh] / acc_ref[:, dh:] * z).astype(dtype)


def _fox_kernel(q_ref, k_ref, cx_ref, v_ref, z_ref, o_ref, kaug_ref, vext_ref, sa_ref, sb_ref, m_ref, acc_ref,
                *, heads):
    qi = pl.program_id(2)
    tq = q_ref.shape[1]
    dh = HEAD_DIM
    G = kaug_ref.shape[0]

    def rows(j):
        return pl.ds(pl.multiple_of(j * tq, tq), tq)

    for g in range(G):
        kaug_ref[g, rows(qi), :dh] = k_ref[0, :, g * dh:(g + 1) * dh]
        kaug_ref[g, rows(qi), dh:] = cx_ref[0]
        vext_ref[g, rows(qi), :dh] = v_ref[0, :, g * dh:(g + 1) * dh]
        vext_ref[g, rows(qi), dh:] = jnp.ones((tq, dh), vext_ref.dtype)

    def causal(s):
        row = lax.broadcasted_iota(jnp.int32, (tq, tq), 0)
        col = lax.broadcasted_iota(jnp.int32, (tq, tq), 1)
        return jnp.where(col <= row, s, NEG_BIG)

    lane = lax.broadcasted_iota(jnp.int32, (tq, LANES), 1)
    tiles = []
    for g in range(G):
        h = pl.program_id(1) * G + g
        pick = (lane == h) | (lane == heads + h) | (lane == 2 * heads + h)
        q_aug = jnp.concatenate([q_ref[0, :, g * dh:(g + 1) * dh],
                                 jnp.where(pick, 1.0, 0.0).astype(q_ref.dtype)], axis=1)
        _init_softmax_state(m_ref.at[g], acc_ref.at[g])

        def logits(j, g=g, q_aug=q_aug):
            return lax.dot_general(q_aug, kaug_ref[g, rows(j), :], _NT_DIMS, preferred_element_type=jnp.float32)

        def values(j, g=g):
            return vext_ref[g, rows(j), :]

        diagonal = _Tile(functools.partial(logits, qi), causal, functools.partial(values, qi))
        tiles.append(_HeadTiles([diagonal], logits, values, (sa_ref.at[g], sb_ref.at[g]), m_ref.at[g], acc_ref.at[g]))

    @pl.when(qi == 0)
    def _():
        _pipelined_tiles(tiles, 0)

    @pl.when(qi >= 1)
    def _():
        _pipelined_tiles(tiles, qi)

    for g in range(G):
        o_ref[0, :, g * dh:(g + 1) * dh] = _finish_attention(acc_ref.at[g], z_ref[0, :, g * dh:(g + 1) * dh],
                                                             o_ref.dtype)


def _fox_attention(qkv, cx, sz, H, col0, zcol0, tq=512):
    B, S, _ = qkv.shape
    tq = min(tq, S)
    Dh = HEAD_DIM
    G = ATTN_HEADS_PER_STEP
    assert H % G == 0 and col0 % G == 0 and zcol0 % G == 0
    W = G * Dh
    return pl.pallas_call(
        functools.partial(_fox_kernel, heads=H),
        grid=(B, H // G, S // tq),
        in_specs=[pl.BlockSpec((1, tq, W), lambda b, h, i: (b, i, col0 // G + h)),
                  pl.BlockSpec((1, tq, W), lambda b, h, i: (b, i, (col0 + H) // G + h)),
                  pl.BlockSpec((1, tq, LANES), lambda b, h, i: (b, i, 0)),
                  pl.BlockSpec((1, tq, W), lambda b, h, i: (b, i, (col0 + 2 * H) // G + h)),
                  pl.BlockSpec((1, tq, W), lambda b, h, i: (b, i, zcol0 // G + h))],
        out_specs=pl.BlockSpec((1, tq, W), lambda b, h, i: (b, i, h)),
        out_shape=jax.ShapeDtypeStruct((B, S, H * Dh), jnp.bfloat16),
        scratch_shapes=[pltpu.VMEM((G, S, Dh + LANES), jnp.bfloat16), pltpu.VMEM((G, S, 2 * Dh), jnp.bfloat16),
                        pltpu.VMEM((G, tq, tq), jnp.float32), pltpu.VMEM((G, tq, tq), jnp.float32),
                        pltpu.VMEM((G, tq, LANES), jnp.float32), pltpu.VMEM((G, tq, 2 * Dh), jnp.float32)],
        compiler_params=_params("arbitrary", "arbitrary", "arbitrary"),
        name="fox_attention",
    )(qkv, qkv, cx, qkv, sz)


def _t5_bucket_np(dist):
    max_exact = T5_NUM_BUCKETS // 2
    d = np.maximum(dist, 1).astype(np.float32)
    ratio = (np.log(d / np.float32(max_exact)) / np.float32(math.log(T5_MAX_DISTANCE / max_exact))
             * np.float32(T5_NUM_BUCKETS - max_exact))
    large = np.minimum(max_exact + ratio.astype(np.int32), T5_NUM_BUCKETS - 1)
    return np.where(dist < max_exact, dist, large).astype(np.int32)


MOBA_Q_BLOCKS = 2


def _moba_query_columns(q, kmean, far_row, first_block, blk, nb):
    nbp, tq = kmean.shape[0], q.shape[0]
    km_hi = kmean.astype(jnp.bfloat16)
    km_lo = (kmean - km_hi.astype(jnp.float32)).astype(jnp.bfloat16)
    gate = (lax.dot_general(km_hi, q, _NT_DIMS, preferred_element_type=jnp.float32)
            + lax.dot_general(km_lo, q, _NT_DIMS, preferred_element_type=jnp.float32))
    n_idx = lax.broadcasted_iota(jnp.int32, (nbp, tq), 0)
    own = first_block + lax.broadcasted_iota(jnp.int32, (nbp, tq), 1) // blk
    past = n_idx < own
    gate = jnp.where(past, gate, NEG_BIG)
    beaten = jnp.zeros((nbp, tq), jnp.int32)
    for n in range(nb):
        g_n = gate[n:n + 1, :]
        wins = (g_n > gate) | ((g_n == gate) & (n < n_idx))
        beaten = beaten + jnp.where(wins & (n < own), 1, 0)
    selected = past & (beaten < MOBA_TOPK)
    far_bias = jnp.broadcast_to(far_row, (nbp, tq))
    term = jnp.where(selected, jnp.where(n_idx == own - 1, 0.0, far_bias), NEG_BIG)
    term = jnp.where(n_idx == own, 0.0, term)
    term_hi = term.astype(jnp.bfloat16).astype(jnp.float32)
    term_lo = jnp.where(selected, term - term_hi, 0.0)
    pieces = [term_hi[:nb], term_lo[:nb]]
    if 2 * nb < LANES:
        pieces.append(jnp.zeros((LANES - 2 * nb, tq), jnp.float32))
    return jnp.concatenate(pieces, axis=0).T.astype(q.dtype)


def _moba_kernel(q_ref, k_ref, v_ref, rbias_ref, far_ref, z_ref, o_ref,
                 kaug_ref, vext_ref, kmean_ref, t5_ref, sa_ref, sb_ref, m_ref, acc_ref, *, nb, blk):
    qi = pl.program_id(2)
    tq = q_ref.shape[1]
    dh = HEAD_DIM
    G = kaug_ref.shape[0]
    first_block = qi * (tq // blk)

    def rows(j):
        return pl.ds(pl.multiple_of(j * tq, tq), tq)

    @pl.when(qi == 0)
    def _():
        for g in range(G):
            kmean_ref[g] = jnp.zeros(kmean_ref.shape[1:], kmean_ref.dtype)
            rb = jnp.broadcast_to(rbias_ref[g], (tq, tq))
            t5_ref[g] = pltpu.roll(rb, 1, 1, stride=1, stride_axis=0)

    key_block = first_block + lax.broadcasted_iota(jnp.int32, (tq, LANES), 0) // blk
    lane = lax.broadcasted_iota(jnp.int32, (tq, LANES), 1)
    hot = jnp.where((lane == key_block) | (lane == key_block + nb), 1.0, 0.0).astype(kaug_ref.dtype)
    for g in range(G):
        k = k_ref[0, :, g * dh:(g + 1) * dh]
        kaug_ref[g, rows(qi), :dh] = k
        kaug_ref[g, rows(qi), dh:] = hot
        vext_ref[g, rows(qi), :dh] = v_ref[0, :, g * dh:(g + 1) * dh]
        vext_ref[g, rows(qi), dh:] = jnp.ones((tq, dh), vext_ref.dtype)
        for n in range(tq // blk):
            kb = k[n * blk:(n + 1) * blk, :].astype(jnp.float32)
            kmean_ref[g, pl.ds(first_block + n, 1), :] = jnp.sum(kb, axis=0, keepdims=True) * (1.0 / blk)

    def causal_with_bias(s, g):
        row = lax.broadcasted_iota(jnp.int32, (tq, tq), 0)
        col = lax.broadcasted_iota(jnp.int32, (tq, tq), 1)
        return jnp.where(col <= row, s + t5_ref[g], NEG_BIG)

    def previous_with_bias(s, g):
        top = jnp.concatenate([s[:blk, :tq - blk], s[:blk, tq - blk:] + t5_ref[g, tq - blk:, :blk]], axis=1)
        return jnp.concatenate([top, s[blk:, :]], axis=0)

    tiles = []
    for g in range(G):
        q = q_ref[0, :, g * dh:(g + 1) * dh]
        q_extra = _moba_query_columns(q, kmean_ref[g], far_ref[g], first_block, blk, nb)
        q_aug = jnp.concatenate([q, q_extra], axis=1)
        _init_softmax_state(m_ref.at[g], acc_ref.at[g])

        def logits(j, g=g, q_aug=q_aug):
            return lax.dot_general(q_aug, kaug_ref[g, rows(j), :], _NT_DIMS, preferred_element_type=jnp.float32)

        def values(j, g=g):
            return vext_ref[g, rows(j), :]

        diagonal = _Tile(functools.partial(logits, qi), functools.partial(causal_with_bias, g=g),
                         functools.partial(values, qi))
        previous = _Tile(functools.partial(logits, qi - 1), functools.partial(previous_with_bias, g=g),
                         functools.partial(values, qi - 1))
        tiles.append((diagonal, previous, logits, values, (sa_ref.at[g], sb_ref.at[g]), m_ref.at[g], acc_ref.at[g]))

    @pl.when(qi == 0)
    def _():
        _pipelined_tiles([_HeadTiles([t[0]], *t[2:]) for t in tiles], 0)

    @pl.when(qi >= 1)
    def _():
        _pipelined_tiles([_HeadTiles([t[0], t[1]], *t[2:]) for t in tiles], qi - 1)

    for g in range(G):
        o_ref[0, :, g * dh:(g + 1) * dh] = _finish_attention(acc_ref.at[g], z_ref[0, :, g * dh:(g + 1) * dh],
                                                             o_ref.dtype)


def _moba_attention(qkv, rel_bias_table, sz, H, col0, zcol0):
    B, S, _ = qkv.shape
    blk = MOBA_BLOCK
    tq = MOBA_Q_BLOCKS * blk
    assert MOBA_Q_BLOCKS == 2 and S % tq == 0
    nb = S // blk
    nbp = -(-nb // 8) * 8
    assert 2 * nb <= LANES
    Dh = HEAD_DIM
    G = ATTN_HEADS_PER_STEP
    assert H % G == 0 and col0 % G == 0 and zcol0 % G == 0
    W = G * Dh
    table_t = rel_bias_table.T.astype(jnp.float32) * LOG2E
    rbias = table_t[:, _t5_bucket_np(np.arange(tq)[::-1])][:, None, :]
    far_bucket = int(_t5_bucket_np(np.array([blk + 1]))[0])
    assert (_t5_bucket_np(np.arange(blk + 1, S + blk)) == far_bucket).all()
    far = jnp.broadcast_to(table_t[:, far_bucket][:, None, None], (H, 1, tq))
    return pl.pallas_call(
        functools.partial(_moba_kernel, nb=nb, blk=blk),
        grid=(B, H // G, S // tq),
        in_specs=[pl.BlockSpec((1, tq, W), lambda b, h, i: (b, i, col0 // G + h)),
                  pl.BlockSpec((1, tq, W), lambda b, h, i: (b, i, (col0 + H) // G + h)),
                  pl.BlockSpec((1, tq, W), lambda b, h, i: (b, i, (col0 + 2 * H) // G + h)),
                  pl.BlockSpec((G, 1, tq), lambda b, h, i: (h, 0, 0)),
                  pl.BlockSpec((G, 1, tq), lambda b, h, i: (h, 0, 0)),
                  pl.BlockSpec((1, tq, W), lambda b, h, i: (b, i, zcol0 // G + h))],
        out_specs=pl.BlockSpec((1, tq, W), lambda b, h, i: (b, i, h)),
        out_shape=jax.ShapeDtypeStruct((B, S, H * Dh), jnp.bfloat16),
        scratch_shapes=[pltpu.VMEM((G, S, Dh + LANES), jnp.bfloat16), pltpu.VMEM((G, S, 2 * Dh), jnp.bfloat16),
                        pltpu.VMEM((G, nbp, Dh), jnp.float32), pltpu.VMEM((G, tq, tq), jnp.float32),
                        pltpu.VMEM((G, tq, tq), jnp.float32), pltpu.VMEM((G, tq, tq), jnp.float32),
                        pltpu.VMEM((G, tq, LANES), jnp.float32), pltpu.VMEM((G, tq, 2 * Dh), jnp.float32)],
        compiler_params=_params("arbitrary", "arbitrary", "arbitrary"),
        name="moba_attention",
    )(qkv, qkv, qkv, rbias, far, sz)


def _branch_merge_kernel(ya_ref, yf_ref, wa_ref, wf_ref, ga_ref, gf_ref, o_ref):
    ua = jnp.dot(ya_ref[...], wa_ref[0], preferred_element_type=jnp.float32)
    uf = jnp.dot(yf_ref[...], wf_ref[0], preferred_element_type=jnp.float32)
    o_ref[...] = (ga_ref[...] * ua + gf_ref[...] * uf).astype(o_ref.dtype)


def _branch_merge(ya, yf, w, gates, tm=1024, tn=512):
    M, W = ya.shape
    D = w.shape[2]
    tm, tn = min(tm, M), min(tn, D)
    nj = D // tn
    return pl.pallas_call(
        _branch_merge_kernel,
        grid=(M // tm, nj),
        in_specs=[pl.BlockSpec((tm, W), lambda i, j: (i, 0)),
                  pl.BlockSpec((tm, W), lambda i, j: (i, 0)),
                  pl.BlockSpec((1, W, tn), lambda i, j: (0, 0, j)),
                  pl.BlockSpec((1, W, tn), lambda i, j: (1, 0, j)),
                  pl.BlockSpec((tm, tn), lambda i, j: (i, j)),
                  pl.BlockSpec((tm, tn), lambda i, j: (i, nj + j))],
        out_specs=pl.BlockSpec((tm, tn), lambda i, j: (i, j)),
        out_shape=jax.ShapeDtypeStruct((M, D), jnp.bfloat16),
        compiler_params=_params("parallel", "parallel"),
        name="branch_merge",
    )(ya, yf, w, w, gates, gates)


LN_ROW_CHUNK = 64


def _out_layernorm_kernel(h_ref, w_ref, x_ref, g_ref, b_ref, o_ref):
    j = pl.program_id(1)
    tn = w_ref.shape[1]
    out = jnp.dot(h_ref[...], w_ref[...], preferred_element_type=jnp.float32)
    o_ref[:, pl.ds(pl.multiple_of(j * tn, tn), tn)] = DEEPNORM_ALPHA * x_ref[...] + out

    @pl.when(j == pl.num_programs(1) - 1)
    def _():
        rows = min(LN_ROW_CHUNK, o_ref.shape[0])

        def normalize_rows(i, carry):
            sl = pl.ds(pl.multiple_of(i * rows, rows), rows)
            r = o_ref[sl, :]
            mu = jnp.mean(r, axis=-1, keepdims=True)
            d = r - mu
            var = jnp.mean(d * d, axis=-1, keepdims=True)
            o_ref[sl, :] = d * lax.rsqrt(var + LN_EPS) * g_ref[...] + b_ref[...]
            return carry

        lax.fori_loop(0, o_ref.shape[0] // rows, normalize_rows, 0)


def _out_layernorm(h, w, x, gain, bias, tm=512, tn=1024):
    M, D = x.shape
    tm, tn = min(tm, M), min(tn, D)
    return pl.pallas_call(
        _out_layernorm_kernel,
        grid=(M // tm, D // tn),
        in_specs=[pl.BlockSpec((tm, D), lambda i, j: (i, 0)),
                  pl.BlockSpec((D, tn), lambda i, j: (0, j)),
                  pl.BlockSpec((tm, tn), lambda i, j: (i, j)),
                  pl.BlockSpec((1, D), lambda i, j: (0, 0)),
                  pl.BlockSpec((1, D), lambda i, j: (0, 0))],
        out_specs=pl.BlockSpec((tm, D), lambda i, j: (i, 0)),
        out_shape=jax.ShapeDtypeStruct((M, D), jnp.float32),
        compiler_params=_params("parallel", "arbitrary"),
        name="out_layernorm",
    )(h, w, x, gain, bias)


def kernel(x, w_in, b_forget, b_gate, rel_bias_table, w_branch, w_out, ln_gain, ln_bias):
    B, S, D = x.shape
    Ha, Hf = MOBA_HEADS, FOX_HEADS
    Wa, Wf = Ha * HEAD_DIM, Hf * HEAD_DIM
    bf16, f32 = jnp.bfloat16, jnp.float32
    depth = w_in.shape[0]
    for layer in range(depth):
        w_t = jnp.swapaxes(w_in[layer], 0, 1).astype(bf16)
        n_attn = 4 * Wa + 4 * Wf
        w_f = jnp.pad(w_t[n_attn:n_attn + Hf], ((0, LANES - Hf), (0, 0)))
        b_f = jnp.pad(b_forget[layer].astype(f32), (0, LANES - Hf))[None, :]
        x2 = x.reshape(B * S, D).astype(f32)
        xb, cx = _forget_bias_columns(x2, w_f, b_f, Hf, B)

        scale = HEAD_DIM ** -0.5 * LOG2E
        qkv_scale = jnp.concatenate([jnp.full((Wa,), scale, f32), jnp.ones((2 * Wa,), f32),
                                     jnp.full((Wf,), scale, f32), jnp.ones((2 * Wf,), f32)])[None, :]
        tn = PROJ_TN
        assert Wa % tn == 0 and Wf % tn == 0
        a_blocks, f_blocks = Wa // tn, Wf // tn
        qkv = _matmul(xb, w_t, qkv_scale, jnp.zeros_like(qkv_scale), bf16, "none",
                      lambda j: jnp.where(j < 3 * a_blocks, j, j + a_blocks), tn=tn).reshape(B, S, -1)
        ones_z = jnp.ones((1, Wa + Wf), f32)
        sz = _matmul(xb, w_t, ones_z, jnp.zeros_like(ones_z), f32, "silu",
                     lambda j: jnp.where(j < a_blocks, j + 3 * a_blocks, j + 3 * a_blocks + 3 * f_blocks),
                     tn=tn).reshape(B, S, -1)
        gates = _matmul(xb, w_t, jnp.ones((1, 2 * D), f32), b_gate[layer].reshape(1, 2 * D).astype(f32),
                        f32, "sigmoid", w_row_start=n_attn + Hf, tn=tn)

        ya = _moba_attention(qkv, rel_bias_table, sz, Ha, 0, 0)
        yf = _fox_attention(qkv, cx, sz, Hf, 3 * Ha, Ha)

        assert Wa == Wf
        merged = _branch_merge(ya.reshape(B * S, Wa), yf.reshape(B * S, Wf), w_branch[layer].astype(bf16), gates)
        y = _out_layernorm(merged, w_out[layer].astype(bf16), x2,
                           ln_gain[layer].reshape(1, D).astype(f32), ln_bias[layer].reshape(1, D).astype(f32))
        x = y.reshape(B, S, D).astype(x.dtype)
    return x
```

```python
import functools
import math
from typing import Any, Callable, NamedTuple, Sequence

import jax
import jax.numpy as jnp
import numpy as np
from jax import lax
from jax.experimental import pallas as pl
from jax.experimental.pallas import tpu as pltpu

HEAD_DIM = 128
LANES = 128
MOBA_HEADS = 16
FOX_HEADS = 16
MOBA_BLOCK = 256
MOBA_TOPK = 3
T5_NUM_BUCKETS = 32
T5_MAX_DISTANCE = 128
LN_EPS = 1e-5
DEPTH = 1
DEEPNORM_ALPHA = (2.0 * DEPTH) ** 0.25
LOG2E = math.log2(math.e)

V7X_VMEM_LIMIT_BYTES = 56 * 1024 * 1024
PROJ_TN = 1024
NEG_BIG = -1e30

_NT_DIMS = (((1,), (1,)), ((), ()))


def _params(*semantics):
    return pltpu.CompilerParams(dimension_semantics=semantics,
                                vmem_limit_bytes=V7X_VMEM_LIMIT_BYTES)


def _split3_bf16(a):
    hi = a.astype(jnp.bfloat16).astype(jnp.float32)
    r = a - hi
    mid = r.astype(jnp.bfloat16).astype(jnp.float32)
    lo = (r - mid).astype(jnp.bfloat16).astype(jnp.float32)
    return hi, mid, lo


def _matmul_kernel(x_ref, wt_ref, s_ref, b_ref, o_ref, *, act):
    acc = lax.dot_general(x_ref[...], wt_ref[...], _NT_DIMS, preferred_element_type=jnp.float32)
    acc = acc * s_ref[...] + b_ref[...]
    if act == "silu":
        acc = acc * jax.nn.sigmoid(acc)
    elif act == "sigmoid":
        acc = jax.nn.sigmoid(acc)
    o_ref[...] = acc.astype(o_ref.dtype)


def _matmul(x, w_t, col_scale, col_bias, out_dtype, act, w_row_block=lambda j: j, w_row_start=None,
            tm=1024, tn=1024):
    M, K = x.shape
    N = col_scale.shape[1]
    tm, tn = min(tm, M), min(tn, N)
    if w_row_start is None:
        w_spec = pl.BlockSpec((tn, K), lambda i, j: (w_row_block(j), 0))
    else:
        align = math.gcd(w_row_start, tn)
        w_spec = pl.BlockSpec((pl.Element(tn), pl.Element(K)),
                              lambda i, j: (pl.multiple_of(w_row_start + j * tn, align), 0))
    return pl.pallas_call(
        functools.partial(_matmul_kernel, act=act),
        grid=(M // tm, N // tn),
        in_specs=[pl.BlockSpec((tm, K), lambda i, j: (i, 0)),
                  w_spec,
                  pl.BlockSpec((1, tn), lambda i, j: (0, j)),
                  pl.BlockSpec((1, tn), lambda i, j: (0, j))],
        out_specs=pl.BlockSpec((tm, tn), lambda i, j: (i, j)),
        out_shape=jax.ShapeDtypeStruct((M, N), out_dtype),
        compiler_params=_params("parallel", "parallel"),
        name="proj_" + act,
    )(x, w_t, col_scale, col_bias)


def _forget_bias_kernel(x_ref, w_ref, b_ref, xb_ref, o_ref, carry_ref, *, heads):
    @pl.when(pl.program_id(1) == 0)
    def _():
        carry_ref[...] = jnp.zeros_like(carry_ref)

    ts = x_ref.shape[0]
    xb = x_ref[...].astype(xb_ref.dtype)
    xb_ref[...] = xb
    z = lax.dot_general(xb, w_ref[...], _NT_DIMS, preferred_element_type=jnp.float32) + b_ref[...]
    log_f = jnp.minimum(z, 0.0) - jnp.log1p(jnp.exp(-jnp.abs(z)))
    row = lax.broadcasted_iota(jnp.int32, (ts, ts), 0)
    col = lax.broadcasted_iota(jnp.int32, (ts, ts), 1)
    lower = jnp.where(col <= row, 1.0, 0.0).astype(jnp.bfloat16)
    c = carry_ref[...]
    for part in _split3_bf16(log_f):
        c = c + jnp.dot(lower, part.astype(jnp.bfloat16), preferred_element_type=jnp.float32)
    carry_ref[...] = c[ts - 1:ts, :]
    hi, mid, lo = _split3_bf16(c * (-LOG2E))
    lane = lax.broadcasted_iota(jnp.int32, (ts, LANES), 1)
    cols = jnp.where(lane < heads, hi,
                     jnp.where(lane < 2 * heads, pltpu.roll(mid, heads, 1),
                               jnp.where(lane < 3 * heads, pltpu.roll(lo, 2 * heads, 1), 0.0)))
    o_ref[0] = cols.astype(o_ref.dtype)


def _forget_bias_columns(x, w_f, b_forget, heads, batch, ts=512):
    M, D = x.shape
    S = M // batch
    assert 3 * heads <= LANES
    ts = min(ts, S)
    steps = S // ts
    return pl.pallas_call(
        functools.partial(_forget_bias_kernel, heads=heads),
        grid=(batch, steps),
        in_specs=[pl.BlockSpec((ts, D), lambda b, i: (b * steps + i, 0)),
                  pl.BlockSpec((LANES, D), lambda b, i: (0, 0)),
                  pl.BlockSpec((1, LANES), lambda b, i: (0, 0))],
        out_specs=[pl.BlockSpec((ts, D), lambda b, i: (b * steps + i, 0)),
                   pl.BlockSpec((1, ts, LANES), lambda b, i: (b, i, 0))],
        out_shape=[jax.ShapeDtypeStruct((M, D), jnp.bfloat16),
                   jax.ShapeDtypeStruct((batch, S, LANES), jnp.bfloat16)],
        scratch_shapes=[pltpu.VMEM((1, LANES), jnp.float32)],
        compiler_params=_params("parallel", "arbitrary"),
        name="forget_bias_columns",
    )(x, w_f, b_forget)


ATTN_HEADS_PER_STEP = 4


def _softmax_tile(s, v_ext, m_ref, acc_ref):
    tk = s.shape[1]
    chunks = [s[:, c:c + LANES] for c in range(0, tk, LANES)]
    mx = functools.reduce(jnp.maximum, chunks)
    m_prev = m_ref[...]
    m_new = jnp.maximum(m_prev, jnp.max(mx, axis=-1, keepdims=True))
    alpha = jnp.exp2(m_prev - m_new)
    p = jnp.concatenate([jnp.exp2(ch - m_new) for ch in chunks], axis=1).astype(v_ext.dtype)
    pv = jnp.dot(p, v_ext, preferred_element_type=jnp.float32)
    acc_ref[...] = acc_ref[...] * jnp.concatenate([alpha, alpha], axis=1) + pv
    m_ref[...] = m_new


class _Tile(NamedTuple):
    qk: Callable[[], Any]
    post: Callable[[Any], Any]
    values: Callable[[], Any]


class _HeadTiles(NamedTuple):
    specials: Sequence[_Tile]
    qk: Callable[[Any], Any]
    values: Callable[[Any], Any]
    s_refs: Sequence[Any]
    m_ref: Any
    acc_ref: Any


def _pipelined_tiles(heads, n_rest):
    def fill(buf, logits_of):
        for hd in heads:
            s = logits_of(hd)
            hd.s_refs[buf][:, :s.shape[1]] = s

    def consume(buf, values_of, post=lambda hd, s: s):
        for hd in heads:
            v_ext = values_of(hd)
            s = hd.s_refs[buf][:, :v_ext.shape[0]]
            _softmax_tile(post(hd, s), v_ext, hd.m_ref, hd.acc_ref)

    n_special = len(heads[0].specials)
    no_uniform_tiles = isinstance(n_rest, int) and n_rest == 0
    fill(0, lambda hd: hd.specials[0].qk())
    for t in range(n_special):
        if t + 1 < n_special:
            fill((t + 1) % 2, lambda hd: hd.specials[t + 1].qk())
        elif not no_uniform_tiles:
            fill((t + 1) % 2, lambda hd: hd.qk(0))
        consume(t % 2, lambda hd: hd.specials[t].values(), lambda hd, s: hd.specials[t].post(s))
    if no_uniform_tiles:
        return
    cur, nxt = n_special % 2, (n_special + 1) % 2

    def two_tiles(i, carry):
        fill(nxt, lambda hd: hd.qk(2 * i + 1))
        consume(cur, lambda hd: hd.values(2 * i))
        fill(cur, lambda hd: hd.qk(jnp.minimum(2 * i + 2, n_rest - 1)))
        consume(nxt, lambda hd: hd.values(2 * i + 1))
        return carry

    lax.fori_loop(0, n_rest // 2, two_tiles, 0)

    @pl.when(n_rest % 2 == 1)
    def _():
        consume(cur, lambda hd: hd.values(n_rest - 1))


def _init_softmax_state(m_ref, acc_ref):
    m_ref[...] = jnp.full_like(m_ref, NEG_BIG)
    acc_ref[...] = jnp.zeros_like(acc_ref)


def _finish_attention(acc_ref, z, dtype):
    dh = z.shape[1]
    return (acc_ref[:, :dh] / acc_ref[:, dh:] * z).astype(dtype)


def _fox_kernel(q_ref, k_ref, cx_ref, v_ref, z_ref, o_ref, kaug_ref, vext_ref, sa_ref, sb_ref, m_ref, acc_ref,
                *, heads):
    qi = pl.program_id(2)
    tq = q_ref.shape[1]
    dh = HEAD_DIM
    G = kaug_ref.shape[0]

    def rows(j):
        return pl.ds(pl.multiple_of(j * tq, tq), tq)

    for g in range(G):
        kaug_ref[g, rows(qi), :dh] = k_ref[0, :, g * dh:(g + 1) * dh]
        kaug_ref[g, rows(qi), dh:] = cx_ref[0]
        vext_ref[g, rows(qi), :dh] = v_ref[0, :, g * dh:(g + 1) * dh]
        vext_ref[g, rows(qi), dh:] = jnp.ones((tq, dh), vext_ref.dtype)

    def causal(s):
        row = lax.broadcasted_iota(jnp.int32, (tq, tq), 0)
        col = lax.broadcasted_iota(jnp.int32, (tq, tq), 1)
        return jnp.where(col <= row, s, NEG_BIG)

    lane = lax.broadcasted_iota(jnp.int32, (1, LANES), 1)
    tiles = []
    for g in range(G):
        h = pl.program_id(1) * G + g
        pick = (lane == h) | (lane == heads + h) | (lane == 2 * heads + h)
        ones_at_head_terms = jnp.broadcast_to(jnp.where(pick, 1.0, 0.0), (tq, LANES)).astype(q_ref.dtype)
        q_aug = jnp.concatenate([q_ref[0, :, g * dh:(g + 1) * dh], ones_at_head_terms], axis=1)
        _init_softmax_state(m_ref.at[g], acc_ref.at[g])

        def logits(j, g=g, q_aug=q_aug):
            return lax.dot_general(q_aug, kaug_ref[g, rows(j), :], _NT_DIMS, preferred_element_type=jnp.float32)

        def values(j, g=g):
            return vext_ref[g, rows(j), :]

        diagonal = _Tile(functools.partial(logits, qi), causal, functools.partial(values, qi))
        tiles.append(_HeadTiles([diagonal], logits, values, (sa_ref.at[g], sb_ref.at[g]), m_ref.at[g], acc_ref.at[g]))

    @pl.when(qi == 0)
    def _():
        _pipelined_tiles(tiles, 0)

    @pl.when(qi >= 1)
    def _():
        _pipelined_tiles(tiles, qi)

    for g in range(G):
        o_ref[0, :, g * dh:(g + 1) * dh] = _finish_attention(acc_ref.at[g], z_ref[0, :, g * dh:(g + 1) * dh],
                                                             o_ref.dtype)


def _fox_attention(qkv, cx, sz, H, col0, zcol0, tq=512):
    B, S, _ = qkv.shape
    tq = min(tq, S)
    Dh = HEAD_DIM
    G = ATTN_HEADS_PER_STEP
    assert H % G == 0 and col0 % G == 0 and zcol0 % G == 0
    W = G * Dh
    return pl.pallas_call(
        functools.partial(_fox_kernel, heads=H),
        grid=(B, H // G, S // tq),
        in_specs=[pl.BlockSpec((1, tq, W), lambda b, h, i: (b, i, col0 // G + h)),
                  pl.BlockSpec((1, tq, W), lambda b, h, i: (b, i, (col0 + H) // G + h)),
                  pl.BlockSpec((1, tq, LANES), lambda b, h, i: (b, i, 0)),
                  pl.BlockSpec((1, tq, W), lambda b, h, i: (b, i, (col0 + 2 * H) // G + h)),
                  pl.BlockSpec((1, tq, W), lambda b, h, i: (b, i, zcol0 // G + h))],
        out_specs=pl.BlockSpec((1, tq, W), lambda b, h, i: (b, i, h)),
        out_shape=jax.ShapeDtypeStruct((B, S, H * Dh), jnp.bfloat16),
        scratch_shapes=[pltpu.VMEM((G, S, Dh + LANES), jnp.bfloat16), pltpu.VMEM((G, S, 2 * Dh), jnp.bfloat16),
                        pltpu.VMEM((G, tq, tq), jnp.float32), pltpu.VMEM((G, tq, tq), jnp.float32),
                        pltpu.VMEM((G, tq, LANES), jnp.float32), pltpu.VMEM((G, tq, 2 * Dh), jnp.float32)],
        compiler_params=_params("arbitrary", "arbitrary", "arbitrary"),
        name="fox_attention",
    )(qkv, qkv, cx, qkv, sz)


def _t5_bucket_np(dist):
    max_exact = T5_NUM_BUCKETS // 2
    d = np.maximum(dist, 1).astype(np.float32)
    ratio = (np.log(d / np.float32(max_exact)) / np.float32(math.log(T5_MAX_DISTANCE / max_exact))
             * np.float32(T5_NUM_BUCKETS - max_exact))
    large = np.minimum(max_exact + ratio.astype(np.int32), T5_NUM_BUCKETS - 1)
    return np.where(dist < max_exact, dist, large).astype(np.int32)


MOBA_Q_BLOCKS = 2


def _moba_query_columns(q, kmean, far_row, first_block, blk, nb):
    nbp, tq = kmean.shape[0], q.shape[0]
    km_hi = kmean.astype(jnp.bfloat16)
    km_lo = (kmean - km_hi.astype(jnp.float32)).astype(jnp.bfloat16)
    gate = (lax.dot_general(km_hi, q, _NT_DIMS, preferred_element_type=jnp.float32)
            + lax.dot_general(km_lo, q, _NT_DIMS, preferred_element_type=jnp.float32))
    n_idx = lax.broadcasted_iota(jnp.int32, (nbp, tq), 0)
    own = first_block + lax.broadcasted_iota(jnp.int32, (nbp, tq), 1) // blk
    past = n_idx < own
    gate = jnp.where(past, gate, NEG_BIG)
    beaten = jnp.zeros((nbp, tq), jnp.int32)
    for n in range(nb):
        g_n = gate[n:n + 1, :]
        lower_index_wins_ties = jnp.where(n_idx > n, 1, 0)
        beaten = beaten + jnp.where(g_n > gate, 1, 0) + jnp.where(g_n == gate, lower_index_wins_ties, 0)
    selected = past & (beaten < MOBA_TOPK)
    far_bias = jnp.broadcast_to(far_row, (nbp, tq))
    term = jnp.where(selected, jnp.where(n_idx == own - 1, 0.0, far_bias), NEG_BIG)
    term = jnp.where(n_idx == own, 0.0, term)
    term_hi = term.astype(jnp.bfloat16).astype(jnp.float32)
    term_lo = jnp.where(selected, term - term_hi, 0.0)
    pieces = [term_hi[:nb], term_lo[:nb]]
    if 2 * nb < LANES:
        pieces.append(jnp.zeros((LANES - 2 * nb, tq), jnp.float32))
    return jnp.concatenate(pieces, axis=0).T.astype(q.dtype)


def _moba_kernel(q_ref, k_ref, v_ref, rbias_ref, far_ref, z_ref, o_ref,
                 kaug_ref, vext_ref, kmean_ref, t5_ref, sa_ref, sb_ref, m_ref, acc_ref, *, nb, blk):
    qi = pl.program_id(2)
    tq = q_ref.shape[1]
    dh = HEAD_DIM
    G = kaug_ref.shape[0]
    first_block = qi * (tq // blk)

    def rows(j):
        return pl.ds(pl.multiple_of(j * tq, tq), tq)

    @pl.when(qi == 0)
    def _():
        for g in range(G):
            kmean_ref[g] = jnp.zeros(kmean_ref.shape[1:], kmean_ref.dtype)
            rb = jnp.broadcast_to(rbias_ref[g], (tq, tq))
            t5_ref[g] = pltpu.roll(rb, 1, 1, stride=1, stride_axis=0)

    key_block = first_block + lax.broadcasted_iota(jnp.int32, (tq, LANES), 0) // blk
    lane = lax.broadcasted_iota(jnp.int32, (1, LANES), 1)
    lane_block = jnp.where(lane < nb, lane, jnp.where(lane < 2 * nb, lane - nb, -1))
    hot = jnp.where(jnp.broadcast_to(lane_block, (tq, LANES)) == key_block, 1.0, 0.0).astype(kaug_ref.dtype)
    for g in range(G):
        k = k_ref[0, :, g * dh:(g + 1) * dh]
        kaug_ref[g, rows(qi), :dh] = k
        kaug_ref[g, rows(qi), dh:] = hot
        vext_ref[g, rows(qi), :dh] = v_ref[0, :, g * dh:(g + 1) * dh]
        vext_ref[g, rows(qi), dh:] = jnp.ones((tq, dh), vext_ref.dtype)
        for n in range(tq // blk):
            kb = k[n * blk:(n + 1) * blk, :].astype(jnp.float32)
            kmean_ref[g, pl.ds(first_block + n, 1), :] = jnp.sum(kb, axis=0, keepdims=True) * (1.0 / blk)

    def causal_with_bias(s, g):
        row = lax.broadcasted_iota(jnp.int32, (tq, tq), 0)
        col = lax.broadcasted_iota(jnp.int32, (tq, tq), 1)
        return jnp.where(col <= row, s + t5_ref[g], NEG_BIG)

    def previous_with_bias(s, g):
        top = jnp.concatenate([s[:blk, :tq - blk], s[:blk, tq - blk:] + t5_ref[g, tq - blk:, :blk]], axis=1)
        return jnp.concatenate([top, s[blk:, :]], axis=0)

    tiles = []
    for g in range(G):
        q = q_ref[0, :, g * dh:(g + 1) * dh]
        q_extra = _moba_query_columns(q, kmean_ref[g], far_ref[g], first_block, blk, nb)
        q_aug = jnp.concatenate([q, q_extra], axis=1)
        _init_softmax_state(m_ref.at[g], acc_ref.at[g])

        def logits(j, g=g, q_aug=q_aug):
            return lax.dot_general(q_aug, kaug_ref[g, rows(j), :], _NT_DIMS, preferred_element_type=jnp.float32)

        def values(j, g=g):
            return vext_ref[g, rows(j), :]

        diagonal = _Tile(functools.partial(logits, qi), functools.partial(causal_with_bias, g=g),
                         functools.partial(values, qi))
        previous = _Tile(functools.partial(logits, qi - 1), functools.partial(previous_with_bias, g=g),
                         functools.partial(values, qi - 1))
        tiles.append((diagonal, previous, logits, values, (sa_ref.at[g], sb_ref.at[g]), m_ref.at[g], acc_ref.at[g]))

    @pl.when(qi == 0)
    def _():
        _pipelined_tiles([_HeadTiles([t[0]], *t[2:]) for t in tiles], 0)

    @pl.when(qi >= 1)
    def _():
        _pipelined_tiles([_HeadTiles([t[0], t[1]], *t[2:]) for t in tiles], qi - 1)

    for g in range(G):
        o_ref[0, :, g * dh:(g + 1) * dh] = _finish_attention(acc_ref.at[g], z_ref[0, :, g * dh:(g + 1) * dh],
                                                             o_ref.dtype)


def _moba_attention(qkv, rel_bias_table, sz, H, col0, zcol0):
    B, S, _ = qkv.shape
    blk = MOBA_BLOCK
    tq = MOBA_Q_BLOCKS * blk
    assert MOBA_Q_BLOCKS == 2 and S % tq == 0
    nb = S // blk
    nbp = -(-nb // 8) * 8
    assert 2 * nb <= LANES
    Dh = HEAD_DIM
    G = ATTN_HEADS_PER_STEP
    assert H % G == 0 and col0 % G == 0 and zcol0 % G == 0
    W = G * Dh
    table_t = rel_bias_table.T.astype(jnp.float32) * LOG2E
    rbias = table_t[:, _t5_bucket_np(np.arange(tq)[::-1])][:, None, :]
    far_bucket = int(_t5_bucket_np(np.array([blk + 1]))[0])
    assert (_t5_bucket_np(np.arange(blk + 1, S + blk)) == far_bucket).all()
    far = jnp.broadcast_to(table_t[:, far_bucket][:, None, None], (H, 1, tq))
    return pl.pallas_call(
        functools.partial(_moba_kernel, nb=nb, blk=blk),
        grid=(B, H // G, S // tq),
        in_specs=[pl.BlockSpec((1, tq, W), lambda b, h, i: (b, i, col0 // G + h)),
                  pl.BlockSpec((1, tq, W), lambda b, h, i: (b, i, (col0 + H) // G + h)),
                  pl.BlockSpec((1, tq, W), lambda b, h, i: (b, i, (col0 + 2 * H) // G + h)),
                  pl.BlockSpec((G, 1, tq), lambda b, h, i: (h, 0, 0)),
                  pl.BlockSpec((G, 1, tq), lambda b, h, i: (h, 0, 0)),
                  pl.BlockSpec((1, tq, W), lambda b, h, i: (b, i, zcol0 // G + h))],
        out_specs=pl.BlockSpec((1, tq, W), lambda b, h, i: (b, i, h)),
        out_shape=jax.ShapeDtypeStruct((B, S, H * Dh), jnp.bfloat16),
        scratch_shapes=[pltpu.VMEM((G, S, Dh + LANES), jnp.bfloat16), pltpu.VMEM((G, S, 2 * Dh), jnp.bfloat16),
                        pltpu.VMEM((G, nbp, Dh), jnp.float32), pltpu.VMEM((G, tq, tq), jnp.float32),
                        pltpu.VMEM((G, tq, tq), jnp.float32), pltpu.VMEM((G, tq, tq), jnp.float32),
                        pltpu.VMEM((G, tq, LANES), jnp.float32), pltpu.VMEM((G, tq, 2 * Dh), jnp.float32)],
        compiler_params=_params("arbitrary", "arbitrary", "arbitrary"),
        name="moba_attention",
    )(qkv, qkv, qkv, rbias, far, sz)


def _branch_merge_kernel(ya_ref, yf_ref, wa_ref, wf_ref, ga_ref, gf_ref, o_ref):
    ua = jnp.dot(ya_ref[...], wa_ref[0], preferred_element_type=jnp.float32)
    uf = jnp.dot(yf_ref[...], wf_ref[0], preferred_element_type=jnp.float32)
    o_ref[...] = (ga_ref[...] * ua + gf_ref[...] * uf).astype(o_ref.dtype)


def _branch_merge(ya, yf, w, gates, tm=1024, tn=512):
    M, W = ya.shape
    D = w.shape[2]
    tm, tn = min(tm, M), min(tn, D)
    nj = D // tn
    return pl.pallas_call(
        _branch_merge_kernel,
        grid=(M // tm, nj),
        in_specs=[pl.BlockSpec((tm, W), lambda i, j: (i, 0)),
                  pl.BlockSpec((tm, W), lambda i, j: (i, 0)),
                  pl.BlockSpec((1, W, tn), lambda i, j: (0, 0, j)),
                  pl.BlockSpec((1, W, tn), lambda i, j: (1, 0, j)),
                  pl.BlockSpec((tm, tn), lambda i, j: (i, j)),
                  pl.BlockSpec((tm, tn), lambda i, j: (i, nj + j))],
        out_specs=pl.BlockSpec((tm, tn), lambda i, j: (i, j)),
        out_shape=jax.ShapeDtypeStruct((M, D), jnp.bfloat16),
        compiler_params=_params("parallel", "parallel"),
        name="branch_merge",
    )(ya, yf, w, w, gates, gates)


LN_ROW_CHUNK = 64


def _out_layernorm_kernel(h_ref, w_ref, x_ref, g_ref, b_ref, o_ref):
    j = pl.program_id(1)
    tn = w_ref.shape[1]
    out = jnp.dot(h_ref[...], w_ref[...], preferred_element_type=jnp.float32)
    o_ref[:, pl.ds(pl.multiple_of(j * tn, tn), tn)] = DEEPNORM_ALPHA * x_ref[...] + out

    @pl.when(j == pl.num_programs(1) - 1)
    def _():
        rows = min(LN_ROW_CHUNK, o_ref.shape[0])

        def normalize_rows(i, carry):
            sl = pl.ds(pl.multiple_of(i * rows, rows), rows)
            r = o_ref[sl, :]
            mu = jnp.mean(r, axis=-1, keepdims=True)
            d = r - mu
            var = jnp.mean(d * d, axis=-1, keepdims=True)
            o_ref[sl, :] = d * lax.rsqrt(var + LN_EPS) * g_ref[...] + b_ref[...]
            return carry

        lax.fori_loop(0, o_ref.shape[0] // rows, normalize_rows, 0)


def _out_layernorm(h, w, x, gain, bias, tm=512, tn=1024):
    M, D = x.shape
    tm, tn = min(tm, M), min(tn, D)
    return pl.pallas_call(
        _out_layernorm_kernel,
        grid=(M // tm, D // tn),
        in_specs=[pl.BlockSpec((tm, D), lambda i, j: (i, 0)),
                  pl.BlockSpec((D, tn), lambda i, j: (0, j)),
                  pl.BlockSpec((tm, tn), lambda i, j: (i, j)),
                  pl.BlockSpec((1, D), lambda i, j: (0, 0)),
                  pl.BlockSpec((1, D), lambda i, j: (0, 0))],
        out_specs=pl.BlockSpec((tm, D), lambda i, j: (i, 0)),
        out_shape=jax.ShapeDtypeStruct((M, D), jnp.float32),
        compiler_params=_params("parallel", "arbitrary"),
        name="out_layernorm",
    )(h, w, x, gain, bias)


def kernel(x, w_in, b_forget, b_gate, rel_bias_table, w_branch, w_out, ln_gain, ln_bias):
    B, S, D = x.shape
    Ha, Hf = MOBA_HEADS, FOX_HEADS
    Wa, Wf = Ha * HEAD_DIM, Hf * HEAD_DIM
    bf16, f32 = jnp.bfloat16, jnp.float32
    depth = w_in.shape[0]
    for layer in range(depth):
        w_t = jnp.swapaxes(w_in[layer], 0, 1).astype(bf16)
        n_attn = 4 * Wa + 4 * Wf
        w_f = jnp.pad(w_t[n_attn:n_attn + Hf], ((0, LANES - Hf), (0, 0)))
        b_f = jnp.pad(b_forget[layer].astype(f32), (0, LANES - Hf))[None, :]
        x2 = x.reshape(B * S, D).astype(f32)
        xb, cx = _forget_bias_columns(x2, w_f, b_f, Hf, B)

        scale = HEAD_DIM ** -0.5 * LOG2E
        qkv_scale = jnp.concatenate([jnp.full((Wa,), scale, f32), jnp.ones((2 * Wa,), f32),
                                     jnp.full((Wf,), scale, f32), jnp.ones((2 * Wf,), f32)])[None, :]
        tn = PROJ_TN
        assert Wa % tn == 0 and Wf % tn == 0
        a_blocks, f_blocks = Wa // tn, Wf // tn
        qkv = _matmul(xb, w_t, qkv_scale, jnp.zeros_like(qkv_scale), bf16, "none",
                      lambda j: jnp.where(j < 3 * a_blocks, j, j + a_blocks), tn=tn).reshape(B, S, -1)
        ones_z = jnp.ones((1, Wa + Wf), f32)
        sz = _matmul(xb, w_t, ones_z, jnp.zeros_like(ones_z), f32, "silu",
                     lambda j: jnp.where(j < a_blocks, j + 3 * a_blocks, j + 3 * a_blocks + 3 * f_blocks),
                     tn=tn).reshape(B, S, -1)
        gates = _matmul(xb, w_t, jnp.ones((1, 2 * D), f32), b_gate[layer].reshape(1, 2 * D).astype(f32),
                        f32, "sigmoid", w_row_start=n_attn + Hf, tn=tn)

        ya = _moba_attention(qkv, rel_bias_table, sz, Ha, 0, 0)
        yf = _fox_attention(qkv, cx, sz, Hf, 3 * Ha, Ha)

        assert Wa == Wf
        merged = _branch_merge(ya.reshape(B * S, Wa), yf.reshape(B * S, Wf), w_branch[layer].astype(bf16), gates)
        y = _out_layernorm(merged, w_out[layer].astype(bf16), x2,
                           ln_gain[layer].reshape(1, D).astype(f32), ln_bias[layer].reshape(1, D).astype(f32))
        x = y.reshape(B, S, D).astype(x.dtype)
    return x
```

```python
import functools
import math
from typing import Any, Callable, NamedTuple, Sequence

import jax
import jax.numpy as jnp
import numpy as np
from jax import lax
from jax.experimental import pallas as pl
from jax.experimental.pallas import tpu as pltpu

HEAD_DIM = 128
LANES = 128
MOBA_HEADS = 16
FOX_HEADS = 16
MOBA_BLOCK = 256
MOBA_TOPK = 3
T5_NUM_BUCKETS = 32
T5_MAX_DISTANCE = 128
LN_EPS = 1e-5
DEPTH = 1
DEEPNORM_ALPHA = (2.0 * DEPTH) ** 0.25
LOG2E = math.log2(math.e)

V7X_VMEM_LIMIT_BYTES = 56 * 1024 * 1024
PROJ_TN = 1024
NEG_BIG = -1e30

_NT_DIMS = (((1,), (1,)), ((), ()))


def _params(*semantics):
    return pltpu.CompilerParams(dimension_semantics=semantics,
                                vmem_limit_bytes=V7X_VMEM_LIMIT_BYTES)


def _split3_bf16(a):
    hi = a.astype(jnp.bfloat16).astype(jnp.float32)
    r = a - hi
    mid = r.astype(jnp.bfloat16).astype(jnp.float32)
    lo = (r - mid).astype(jnp.bfloat16).astype(jnp.float32)
    return hi, mid, lo


def _matmul_kernel(x_ref, wt_ref, s_ref, b_ref, o_ref, *, act):
    acc = lax.dot_general(x_ref[...], wt_ref[...], _NT_DIMS, preferred_element_type=jnp.float32)
    acc = acc * s_ref[...] + b_ref[...]
    if act == "silu":
        acc = acc * jax.nn.sigmoid(acc)
    elif act == "sigmoid":
        acc = jax.nn.sigmoid(acc)
    o_ref[...] = acc.astype(o_ref.dtype)


def _matmul(x, w_t, col_scale, col_bias, out_dtype, act, w_row_block=lambda j: j, w_row_start=None,
            tm=1024, tn=1024):
    M, K = x.shape
    N = col_scale.shape[1]
    tm, tn = min(tm, M), min(tn, N)
    if w_row_start is None:
        w_spec = pl.BlockSpec((tn, K), lambda i, j: (w_row_block(j), 0))
    else:
        align = math.gcd(w_row_start, tn)
        w_spec = pl.BlockSpec((pl.Element(tn), pl.Element(K)),
                              lambda i, j: (pl.multiple_of(w_row_start + j * tn, align), 0))
    return pl.pallas_call(
        functools.partial(_matmul_kernel, act=act),
        grid=(M // tm, N // tn),
        in_specs=[pl.BlockSpec((tm, K), lambda i, j: (i, 0)),
                  w_spec,
                  pl.BlockSpec((1, tn), lambda i, j: (0, j)),
                  pl.BlockSpec((1, tn), lambda i, j: (0, j))],
        out_specs=pl.BlockSpec((tm, tn), lambda i, j: (i, j)),
        out_shape=jax.ShapeDtypeStruct((M, N), out_dtype),
        compiler_params=_params("parallel", "parallel"),
        name="proj_" + act,
    )(x, w_t, col_scale, col_bias)


def _forget_bias_kernel(x_ref, w_ref, b_ref, xb_ref, o_ref, carry_ref, *, heads):
    @pl.when(pl.program_id(1) == 0)
    def _():
        carry_ref[...] = jnp.zeros_like(carry_ref)

    ts = x_ref.shape[0]
    xb = x_ref[...].astype(xb_ref.dtype)
    xb_ref[...] = xb
    z = lax.dot_general(xb, w_ref[...], _NT_DIMS, preferred_element_type=jnp.float32) + b_ref[...]
    log_f = jnp.minimum(z, 0.0) - jnp.log1p(jnp.exp(-jnp.abs(z)))
    row = lax.broadcasted_iota(jnp.int32, (ts, ts), 0)
    col = lax.broadcasted_iota(jnp.int32, (ts, ts), 1)
    lower = jnp.where(col <= row, 1.0, 0.0).astype(jnp.bfloat16)
    c = carry_ref[...]
    for part in _split3_bf16(log_f):
        c = c + jnp.dot(lower, part.astype(jnp.bfloat16), preferred_element_type=jnp.float32)
    carry_ref[...] = c[ts - 1:ts, :]
    hi, mid, lo = _split3_bf16(c * (-LOG2E))
    lane = lax.broadcasted_iota(jnp.int32, (ts, LANES), 1)
    cols = jnp.where(lane < heads, hi,
                     jnp.where(lane < 2 * heads, pltpu.roll(mid, heads, 1),
                               jnp.where(lane < 3 * heads, pltpu.roll(lo, 2 * heads, 1), 0.0)))
    o_ref[0] = cols.astype(o_ref.dtype)


def _forget_bias_columns(x, w_f, b_forget, heads, batch, ts=512):
    M, D = x.shape
    S = M // batch
    assert 3 * heads <= LANES
    ts = min(ts, S)
    steps = S // ts
    return pl.pallas_call(
        functools.partial(_forget_bias_kernel, heads=heads),
        grid=(batch, steps),
        in_specs=[pl.BlockSpec((ts, D), lambda b, i: (b * steps + i, 0)),
                  pl.BlockSpec((LANES, D), lambda b, i: (0, 0)),
                  pl.BlockSpec((1, LANES), lambda b, i: (0, 0))],
        out_specs=[pl.BlockSpec((ts, D), lambda b, i: (b * steps + i, 0)),
                   pl.BlockSpec((1, ts, LANES), lambda b, i: (b, i, 0))],
        out_shape=[jax.ShapeDtypeStruct((M, D), jnp.bfloat16),
                   jax.ShapeDtypeStruct((batch, S, LANES), jnp.bfloat16)],
        scratch_shapes=[pltpu.VMEM((1, LANES), jnp.float32)],
        compiler_params=_params("parallel", "arbitrary"),
        name="forget_bias_columns",
    )(x, w_f, b_forget)


ATTN_HEADS_PER_STEP = 4


def _softmax_tile(s, v_ext, m_ref, acc_ref):
    tk = s.shape[1]
    chunks = [s[:, c:c + LANES] for c in range(0, tk, LANES)]
    mx = functools.reduce(jnp.maximum, chunks)
    m_prev = m_ref[...]
    m_new = jnp.maximum(m_prev, jnp.max(mx, axis=-1, keepdims=True))
    alpha = jnp.exp2(m_prev - m_new)
    p = jnp.concatenate([jnp.exp2(ch - m_new) for ch in chunks], axis=1).astype(v_ext.dtype)
    pv = jnp.dot(p, v_ext, preferred_element_type=jnp.float32)
    acc_ref[...] = acc_ref[...] * jnp.concatenate([alpha, alpha], axis=1) + pv
    m_ref[...] = m_new


class _Tile(NamedTuple):
    qk: Callable[[], Any]
    post: Callable[[Any], Any]
    values: Callable[[], Any]


class _HeadTiles(NamedTuple):
    specials: Sequence[_Tile]
    qk: Callable[[Any], Any]
    values: Callable[[Any], Any]
    s_refs: Sequence[Any]
    m_ref: Any
    acc_ref: Any


def _pipelined_tiles(heads, n_rest):
    def fill(buf, logits_of):
        for hd in heads:
            s = logits_of(hd)
            hd.s_refs[buf][:, :s.shape[1]] = s

    def consume(buf, values_of, post=lambda hd, s: s):
        for hd in heads:
            v_ext = values_of(hd)
            s = hd.s_refs[buf][:, :v_ext.shape[0]]
            _softmax_tile(post(hd, s), v_ext, hd.m_ref, hd.acc_ref)

    n_special = len(heads[0].specials)
    no_uniform_tiles = isinstance(n_rest, int) and n_rest == 0
    fill(0, lambda hd: hd.specials[0].qk())
    for t in range(n_special):
        if t + 1 < n_special:
            fill((t + 1) % 2, lambda hd: hd.specials[t + 1].qk())
        elif not no_uniform_tiles:
            fill((t + 1) % 2, lambda hd: hd.qk(0))
        consume(t % 2, lambda hd: hd.specials[t].values(), lambda hd, s: hd.specials[t].post(s))
    if no_uniform_tiles:
        return
    cur, nxt = n_special % 2, (n_special + 1) % 2

    def two_tiles(i, carry):
        fill(nxt, lambda hd: hd.qk(2 * i + 1))
        consume(cur, lambda hd: hd.values(2 * i))
        fill(cur, lambda hd: hd.qk(jnp.minimum(2 * i + 2, n_rest - 1)))
        consume(nxt, lambda hd: hd.values(2 * i + 1))
        return carry

    lax.fori_loop(0, n_rest // 2, two_tiles, 0)

    @pl.when(n_rest % 2 == 1)
    def _():
        consume(cur, lambda hd: hd.values(n_rest - 1))


def _init_softmax_state(m_ref, acc_ref):
    m_ref[...] = jnp.full_like(m_ref, NEG_BIG)
    acc_ref[...] = jnp.zeros_like(acc_ref)


def _finish_attention(acc_ref, z, dtype):
    dh = z.shape[1]
    return (acc_ref[:, :dh] / acc_ref[:, dh:] * z).astype(dtype)


def _fox_kernel(q_ref, k_ref, cx_ref, v_ref, z_ref, o_ref, kaug_ref, vext_ref, sa_ref, sb_ref, m_ref, acc_ref,
                *, heads):
    qi = pl.program_id(2)
    tq = q_ref.shape[1]
    dh = HEAD_DIM
    G = kaug_ref.shape[0]

    def rows(j):
        return pl.ds(pl.multiple_of(j * tq, tq), tq)

    for g in range(G):
        kaug_ref[g, rows(qi), :dh] = k_ref[0, :, g * dh:(g + 1) * dh]
        kaug_ref[g, rows(qi), dh:] = cx_ref[0]
        vext_ref[g, rows(qi), :dh] = v_ref[0, :, g * dh:(g + 1) * dh]
        vext_ref[g, rows(qi), dh:] = jnp.ones((tq, dh), vext_ref.dtype)

    def causal(s):
        row = lax.broadcasted_iota(jnp.int32, (tq, tq), 0)
        col = lax.broadcasted_iota(jnp.int32, (tq, tq), 1)
        return jnp.where(col <= row, s, NEG_BIG)

    lane = lax.broadcasted_iota(jnp.int32, (1, LANES), 1)
    tiles = []
    for g in range(G):
        h = pl.program_id(1) * G + g
        pick = (lane == h) | (lane == heads + h) | (lane == 2 * heads + h)
        ones_at_head_terms = jnp.broadcast_to(jnp.where(pick, 1.0, 0.0), (tq, LANES)).astype(q_ref.dtype)
        q_aug = jnp.concatenate([q_ref[0, :, g * dh:(g + 1) * dh], ones_at_head_terms], axis=1)
        _init_softmax_state(m_ref.at[g], acc_ref.at[g])

        def logits(j, g=g, q_aug=q_aug):
            return lax.dot_general(q_aug, kaug_ref[g, rows(j), :], _NT_DIMS, preferred_element_type=jnp.float32)

        def values(j, g=g):
            return vext_ref[g, rows(j), :]

        diagonal = _Tile(functools.partial(logits, qi), causal, functools.partial(values, qi))
        tiles.append(_HeadTiles([diagonal], logits, values, (sa_ref.at[g], sb_ref.at[g]), m_ref.at[g], acc_ref.at[g]))

    @pl.when(qi == 0)
    def _():
        _pipelined_tiles(tiles, 0)

    @pl.when(qi >= 1)
    def _():
        _pipelined_tiles(tiles, qi)

    for g in range(G):
        o_ref[0, :, g * dh:(g + 1) * dh] = _finish_attention(acc_ref.at[g], z_ref[0, :, g * dh:(g + 1) * dh],
                                                             o_ref.dtype)


def _fox_attention(qkv, cx, sz, H, col0, zcol0, tq=512):
    B, S, _ = qkv.shape
    tq = min(tq, S)
    Dh = HEAD_DIM
    G = ATTN_HEADS_PER_STEP
    assert H % G == 0 and col0 % G == 0 and zcol0 % G == 0
    W = G * Dh
    return pl.pallas_call(
        functools.partial(_fox_kernel, heads=H),
        grid=(B, H // G, S // tq),
        in_specs=[pl.BlockSpec((1, tq, W), lambda b, h, i: (b, i, col0 // G + h)),
                  pl.BlockSpec((1, tq, W), lambda b, h, i: (b, i, (col0 + H) // G + h)),
                  pl.BlockSpec((1, tq, LANES), lambda b, h, i: (b, i, 0)),
                  pl.BlockSpec((1, tq, W), lambda b, h, i: (b, i, (col0 + 2 * H) // G + h)),
                  pl.BlockSpec((1, tq, W), lambda b, h, i: (b, i, zcol0 // G + h))],
        out_specs=pl.BlockSpec((1, tq, W), lambda b, h, i: (b, i, h)),
        out_shape=jax.ShapeDtypeStruct((B, S, H * Dh), jnp.bfloat16),
        scratch_shapes=[pltpu.VMEM((G, S, Dh + LANES), jnp.bfloat16), pltpu.VMEM((G, S, 2 * Dh), jnp.bfloat16),
                        pltpu.VMEM((G, tq, tq), jnp.float32), pltpu.VMEM((G, tq, tq), jnp.float32),
                        pltpu.VMEM((G, tq, LANES), jnp.float32), pltpu.VMEM((G, tq, 2 * Dh), jnp.float32)],
        compiler_params=_params("arbitrary", "arbitrary", "arbitrary"),
        name="fox_attention",
    )(qkv, qkv, cx, qkv, sz)


def _t5_bucket_np(dist):
    max_exact = T5_NUM_BUCKETS // 2
    d = np.maximum(dist, 1).astype(np.float32)
    ratio = (np.log(d / np.float32(max_exact)) / np.float32(math.log(T5_MAX_DISTANCE / max_exact))
             * np.float32(T5_NUM_BUCKETS - max_exact))
    large = np.minimum(max_exact + ratio.astype(np.int32), T5_NUM_BUCKETS - 1)
    return np.where(dist < max_exact, dist, large).astype(np.int32)


MOBA_Q_BLOCKS = 2


def _moba_query_columns(q, kmean, far_row, first_block, blk, nb):
    nbp, tq = kmean.shape[0], q.shape[0]
    km_hi = kmean.astype(jnp.bfloat16)
    km_lo = (kmean - km_hi.astype(jnp.float32)).astype(jnp.bfloat16)
    gate = (lax.dot_general(km_hi, q, _NT_DIMS, preferred_element_type=jnp.float32)
            + lax.dot_general(km_lo, q, _NT_DIMS, preferred_element_type=jnp.float32))
    n_idx = lax.broadcasted_iota(jnp.int32, (nbp, tq), 0)
    own = first_block + lax.broadcasted_iota(jnp.int32, (nbp, tq), 1) // blk
    past = n_idx < own
    gate = jnp.where(past, gate, NEG_BIG)
    beaten = jnp.zeros((nbp, tq), jnp.int32)
    for n in range(nb):
        g_n = gate[n:n + 1, :]
        lower_index_wins_ties = jnp.where(n_idx > n, 1, 0)
        beaten = beaten + jnp.where(g_n > gate, 1, 0) + jnp.where(g_n == gate, lower_index_wins_ties, 0)
    selected = past & (beaten < MOBA_TOPK)
    far_bias = jnp.broadcast_to(far_row, (nbp, tq))
    term = jnp.where(selected, jnp.where(n_idx == own - 1, 0.0, far_bias), NEG_BIG)
    term = jnp.where(n_idx == own, 0.0, term)
    term_hi = term.astype(jnp.bfloat16).astype(jnp.float32)
    term_lo = jnp.where(selected, term - term_hi, 0.0)
    pieces = [term_hi[:nb], term_lo[:nb]]
    if 2 * nb < LANES:
        pieces.append(jnp.zeros((LANES - 2 * nb, tq), jnp.float32))
    return jnp.concatenate(pieces, axis=0).T.astype(q.dtype)


def _moba_kernel(q_ref, k_ref, v_ref, rbias_ref, far_ref, z_ref, o_ref,
                 kaug_ref, vext_ref, kmean_ref, t5_ref, sa_ref, sb_ref, m_ref, acc_ref, *, nb, blk):
    qi = pl.program_id(2)
    tq = q_ref.shape[1]
    dh = HEAD_DIM
    G = kaug_ref.shape[0]
    first_block = qi * (tq // blk)

    def rows(j):
        return pl.ds(pl.multiple_of(j * tq, tq), tq)

    @pl.when(qi == 0)
    def _():
        for g in range(G):
            kmean_ref[g] = jnp.zeros(kmean_ref.shape[1:], kmean_ref.dtype)
            rb = jnp.broadcast_to(rbias_ref[g], (tq, tq))
            t5_ref[g] = pltpu.roll(rb, 1, 1, stride=1, stride_axis=0)

    key_block = first_block + lax.broadcasted_iota(jnp.int32, (tq, LANES), 0) // blk
    lane = lax.broadcasted_iota(jnp.int32, (1, LANES), 1)
    lane_block = jnp.where(lane < nb, lane, jnp.where(lane < 2 * nb, lane - nb, -1))
    hot = jnp.where(jnp.broadcast_to(lane_block, (tq, LANES)) == key_block, 1.0, 0.0).astype(kaug_ref.dtype)
    for g in range(G):
        k = k_ref[0, :, g * dh:(g + 1) * dh]
        kaug_ref[g, rows(qi), :dh] = k
        kaug_ref[g, rows(qi), dh:] = hot
        vext_ref[g, rows(qi), :dh] = v_ref[0, :, g * dh:(g + 1) * dh]
        vext_ref[g, rows(qi), dh:] = jnp.ones((tq, dh), vext_ref.dtype)
        for n in range(tq // blk):
            kb = k[n * blk:(n + 1) * blk, :].astype(jnp.float32)
            kmean_ref[g, pl.ds(first_block + n, 1), :] = jnp.sum(kb, axis=0, keepdims=True) * (1.0 / blk)

    def causal_with_bias(s, g):
        row = lax.broadcasted_iota(jnp.int32, (tq, tq), 0)
        col = lax.broadcasted_iota(jnp.int32, (tq, tq), 1)
        return jnp.where(col <= row, s + t5_ref[g], NEG_BIG)

    def previous_with_bias(s, g):
        top = jnp.concatenate([s[:blk, :tq - blk], s[:blk, tq - blk:] + t5_ref[g, tq - blk:, :blk]], axis=1)
        return jnp.concatenate([top, s[blk:, :]], axis=0)

    tiles = []
    for g in range(G):
        q = q_ref[0, :, g * dh:(g + 1) * dh]
        q_extra = _moba_query_columns(q, kmean_ref[g], far_ref[g], first_block, blk, nb)
        q_aug = jnp.concatenate([q, q_extra], axis=1)
        _init_softmax_state(m_ref.at[g], acc_ref.at[g])

        def logits(j, g=g, q_aug=q_aug):
            return lax.dot_general(q_aug, kaug_ref[g, rows(j), :], _NT_DIMS, preferred_element_type=jnp.float32)

        def values(j, g=g):
            return vext_ref[g, rows(j), :]

        diagonal = _Tile(functools.partial(logits, qi), functools.partial(causal_with_bias, g=g),
                         functools.partial(values, qi))
        previous = _Tile(functools.partial(logits, qi - 1), functools.partial(previous_with_bias, g=g),
                         functools.partial(values, qi - 1))
        tiles.append((diagonal, previous, logits, values, (sa_ref.at[g], sb_ref.at[g]), m_ref.at[g], acc_ref.at[g]))

    @pl.when(qi == 0)
    def _():
        _pipelined_tiles([_HeadTiles([t[0]], *t[2:]) for t in tiles], 0)

    @pl.when(qi >= 1)
    def _():
        _pipelined_tiles([_HeadTiles([t[0], t[1]], *t[2:]) for t in tiles], qi - 1)

    for g in range(G):
        o_ref[0, :, g * dh:(g + 1) * dh] = _finish_attention(acc_ref.at[g], z_ref[0, :, g * dh:(g + 1) * dh],
                                                             o_ref.dtype)


def _moba_attention(qkv, rel_bias_table, sz, H, col0, zcol0):
    B, S, _ = qkv.shape
    blk = MOBA_BLOCK
    tq = MOBA_Q_BLOCKS * blk
    assert MOBA_Q_BLOCKS == 2 and S % tq == 0
    nb = S // blk
    nbp = -(-nb // 8) * 8
    assert 2 * nb <= LANES
    Dh = HEAD_DIM
    G = ATTN_HEADS_PER_STEP
    assert H % G == 0 and col0 % G == 0 and zcol0 % G == 0
    W = G * Dh
    table_t = rel_bias_table.T.astype(jnp.float32) * LOG2E
    rbias = table_t[:, _t5_bucket_np(np.arange(tq)[::-1])][:, None, :]
    far_bucket = int(_t5_bucket_np(np.array([blk + 1]))[0])
    assert (_t5_bucket_np(np.arange(blk + 1, S + blk)) == far_bucket).all()
    far = jnp.broadcast_to(table_t[:, far_bucket][:, None, None], (H, 1, tq))
    return pl.pallas_call(
        functools.partial(_moba_kernel, nb=nb, blk=blk),
        grid=(B, H // G, S // tq),
        in_specs=[pl.BlockSpec((1, tq, W), lambda b, h, i: (b, i, col0 // G + h)),
                  pl.BlockSpec((1, tq, W), lambda b, h, i: (b, i, (col0 + H) // G + h)),
                  pl.BlockSpec((1, tq, W), lambda b, h, i: (b, i, (col0 + 2 * H) // G + h)),
                  pl.BlockSpec((G, 1, tq), lambda b, h, i: (h, 0, 0)),
                  pl.BlockSpec((G, 1, tq), lambda b, h, i: (h, 0, 0)),
                  pl.BlockSpec((1, tq, W), lambda b, h, i: (b, i, zcol0 // G + h))],
        out_specs=pl.BlockSpec((1, tq, W), lambda b, h, i: (b, i, h)),
        out_shape=jax.ShapeDtypeStruct((B, S, H * Dh), jnp.bfloat16),
        scratch_shapes=[pltpu.VMEM((G, S, Dh + LANES), jnp.bfloat16), pltpu.VMEM((G, S, 2 * Dh), jnp.bfloat16),
                        pltpu.VMEM((G, nbp, Dh), jnp.float32), pltpu.VMEM((G, tq, tq), jnp.float32),
                        pltpu.VMEM((G, tq, tq), jnp.float32), pltpu.VMEM((G, tq, tq), jnp.float32),
                        pltpu.VMEM((G, tq, LANES), jnp.float32), pltpu.VMEM((G, tq, 2 * Dh), jnp.float32)],
        compiler_params=_params("arbitrary", "arbitrary", "arbitrary"),
        name="moba_attention",
    )(qkv, qkv, qkv, rbias, far, sz)


def _branch_merge_kernel(ya_ref, yf_ref, wa_ref, wf_ref, ga_ref, gf_ref, o_ref):
    ua = jnp.dot(ya_ref[...], wa_ref[0], preferred_element_type=jnp.float32)
    uf = jnp.dot(yf_ref[...], wf_ref[0], preferred_element_type=jnp.float32)
    o_ref[...] = (ga_ref[...] * ua + gf_ref[...] * uf).astype(o_ref.dtype)


def _branch_merge(ya, yf, w, gates, tm=1024, tn=512):
    M, W = ya.shape
    D = w.shape[2]
    tm, tn = min(tm, M), min(tn, D)
    nj = D // tn
    return pl.pallas_call(
        _branch_merge_kernel,
        grid=(M // tm, nj),
        in_specs=[pl.BlockSpec((tm, W), lambda i, j: (i, 0)),
                  pl.BlockSpec((tm, W), lambda i, j: (i, 0)),
                  pl.BlockSpec((1, W, tn), lambda i, j: (0, 0, j)),
                  pl.BlockSpec((1, W, tn), lambda i, j: (1, 0, j)),
                  pl.BlockSpec((tm, tn), lambda i, j: (i, j)),
                  pl.BlockSpec((tm, tn), lambda i, j: (i, nj + j))],
        out_specs=pl.BlockSpec((tm, tn), lambda i, j: (i, j)),
        out_shape=jax.ShapeDtypeStruct((M, D), jnp.bfloat16),
        compiler_params=_params("parallel", "parallel"),
        name="branch_merge",
    )(ya, yf, w, w, gates, gates)


LN_ROW_CHUNK = 32


def _out_layernorm_kernel(h_ref, w_ref, x_ref, g_ref, b_ref, o_ref, pre_even_ref, pre_odd_ref):
    i, j = pl.program_id(0), pl.program_id(1)
    tn = w_ref.shape[1]
    chunk = o_ref.shape[0]

    @pl.when((i == 0) & (j == 0))
    def _():
        pre_odd_ref[...] = jnp.zeros_like(pre_odd_ref)

    def step(cur_ref, prev_ref):
        rows = min(LN_ROW_CHUNK, chunk)
        for c in range(chunk // rows):
            r = prev_ref[pl.ds(pl.multiple_of(j * chunk + c * rows, rows), rows), :]
            mu = jnp.mean(r, axis=-1, keepdims=True)
            d = r - mu
            var = jnp.mean(d * d, axis=-1, keepdims=True)
            o_ref[c * rows:(c + 1) * rows, :] = d * lax.rsqrt(var + LN_EPS) * g_ref[...] + b_ref[...]
        out = jnp.dot(h_ref[...], w_ref[...], preferred_element_type=jnp.float32)
        cur_ref[:, pl.ds(pl.multiple_of(j * tn, tn), tn)] = DEEPNORM_ALPHA * x_ref[...] + out

    @pl.when(i % 2 == 0)
    def _():
        step(pre_even_ref, pre_odd_ref)

    @pl.when(i % 2 == 1)
    def _():
        step(pre_odd_ref, pre_even_ref)


def _out_layernorm(h, w, x, gain, bias, tm=512, tn=1024):
    M, D = x.shape
    tm, tn = min(tm, M), min(tn, D)
    n_i, n_j = M // tm, D // tn
    chunk = tm // n_j
    assert chunk % 8 == 0
    return pl.pallas_call(
        _out_layernorm_kernel,
        grid=(n_i + 1, n_j),
        in_specs=[pl.BlockSpec((tm, D), lambda i, j: (jnp.minimum(i, n_i - 1), 0)),
                  pl.BlockSpec((D, tn), lambda i, j: (0, j)),
                  pl.BlockSpec((tm, tn), lambda i, j: (jnp.minimum(i, n_i - 1), j)),
                  pl.BlockSpec((1, D), lambda i, j: (0, 0)),
                  pl.BlockSpec((1, D), lambda i, j: (0, 0))],
        out_specs=pl.BlockSpec((chunk, D), lambda i, j: (jnp.where(i == 0, 0, (i - 1) * n_j + j), 0)),
        out_shape=jax.ShapeDtypeStruct((M, D), jnp.float32),
        scratch_shapes=[pltpu.VMEM((tm, D), jnp.float32), pltpu.VMEM((tm, D), jnp.float32)],
        compiler_params=_params("arbitrary", "arbitrary"),
        name="out_layernorm",
    )(h, w, x, gain, bias)


def kernel(x, w_in, b_forget, b_gate, rel_bias_table, w_branch, w_out, ln_gain, ln_bias):
    B, S, D = x.shape
    Ha, Hf = MOBA_HEADS, FOX_HEADS
    Wa, Wf = Ha * HEAD_DIM, Hf * HEAD_DIM
    bf16, f32 = jnp.bfloat16, jnp.float32
    depth = w_in.shape[0]
    for layer in range(depth):
        w_t = jnp.swapaxes(w_in[layer], 0, 1).astype(bf16)
        n_attn = 4 * Wa + 4 * Wf
        w_f = jnp.pad(w_t[n_attn:n_attn + Hf], ((0, LANES - Hf), (0, 0)))
        b_f = jnp.pad(b_forget[layer].astype(f32), (0, LANES - Hf))[None, :]
        x2 = x.reshape(B * S, D).astype(f32)
        xb, cx = _forget_bias_columns(x2, w_f, b_f, Hf, B)

        scale = HEAD_DIM ** -0.5 * LOG2E
        qkv_scale = jnp.concatenate([jnp.full((Wa,), scale, f32), jnp.ones((2 * Wa,), f32),
                                     jnp.full((Wf,), scale, f32), jnp.ones((2 * Wf,), f32)])[None, :]
        tn = PROJ_TN
        assert Wa % tn == 0 and Wf % tn == 0
        a_blocks, f_blocks = Wa // tn, Wf // tn
        qkv = _matmul(xb, w_t, qkv_scale, jnp.zeros_like(qkv_scale), bf16, "none",
                      lambda j: jnp.where(j < 3 * a_blocks, j, j + a_blocks), tn=tn).reshape(B, S, -1)
        ones_z = jnp.ones((1, Wa + Wf), f32)
        sz = _matmul(xb, w_t, ones_z, jnp.zeros_like(ones_z), f32, "silu",
                     lambda j: jnp.where(j < a_blocks, j + 3 * a_blocks, j + 3 * a_blocks + 3 * f_blocks),
                     tn=tn).reshape(B, S, -1)
        gates = _matmul(xb, w_t, jnp.ones((1, 2 * D), f32), b_gate[layer].reshape(1, 2 * D).astype(f32),
                        f32, "sigmoid", w_row_start=n_attn + Hf, tn=tn)

        ya = _moba_attention(qkv, rel_bias_table, sz, Ha, 0, 0)
        yf = _fox_attention(qkv, cx, sz, Hf, 3 * Ha, Ha)

        assert Wa == Wf
        merged = _branch_merge(ya.reshape(B * S, Wa), yf.reshape(B * S, Wf), w_branch[layer].astype(bf16), gates)
        y = _out_layernorm(merged, w_out[layer].astype(bf16), x2,
                           ln_gain[layer].reshape(1, D).astype(f32), ln_bias[layer].reshape(1, D).astype(f32))
        x = y.reshape(B, S, D).astype(x.dtype)
    return x
```

```python
import functools
import math
from typing import Any, Callable, NamedTuple, Sequence

import jax
import jax.numpy as jnp
import numpy as np
from jax import lax
from jax.experimental import pallas as pl
from jax.experimental.pallas import tpu as pltpu

HEAD_DIM = 128
LANES = 128
MOBA_HEADS = 16
FOX_HEADS = 16
MOBA_BLOCK = 256
MOBA_TOPK = 3
T5_NUM_BUCKETS = 32
T5_MAX_DISTANCE = 128
LN_EPS = 1e-5
DEPTH = 1
DEEPNORM_ALPHA = (2.0 * DEPTH) ** 0.25
LOG2E = math.log2(math.e)

V7X_VMEM_LIMIT_BYTES = 58 * 1024 * 1024
PROJ_TN = 1024
NEG_BIG = -1e30

_NT_DIMS = (((1,), (1,)), ((), ()))


def _params(*semantics):
    return pltpu.CompilerParams(dimension_semantics=semantics,
                                vmem_limit_bytes=V7X_VMEM_LIMIT_BYTES)


def _split3_bf16(a):
    hi = a.astype(jnp.bfloat16).astype(jnp.float32)
    r = a - hi
    mid = r.astype(jnp.bfloat16).astype(jnp.float32)
    lo = (r - mid).astype(jnp.bfloat16).astype(jnp.float32)
    return hi, mid, lo


def _matmul_kernel(x_ref, wt_ref, s_ref, b_ref, o_ref, wb_ref, *, act):
    @pl.when(pl.program_id(1) == 0)
    def _():
        wb_ref[...] = wt_ref[...].astype(wb_ref.dtype)

    acc = lax.dot_general(x_ref[...], wb_ref[...], _NT_DIMS, preferred_element_type=jnp.float32)
    acc = acc * s_ref[...] + b_ref[...]
    if act == "silu":
        acc = acc * jax.nn.sigmoid(acc)
    elif act == "sigmoid":
        acc = jax.nn.sigmoid(acc)
    o_ref[...] = acc.astype(o_ref.dtype)


def _matmul(x, w_t, col_scale, col_bias, out_dtype, act, w_row_block=lambda j: j, w_row_start=None,
            tm=512, tn=1024):
    M, K = x.shape
    N = col_scale.shape[1]
    tm, tn = min(tm, M), min(tn, N)
    if w_row_start is None:
        w_spec = pl.BlockSpec((tn, K), lambda j, i: (w_row_block(j), 0))
    else:
        align = math.gcd(w_row_start, tn)
        w_spec = pl.BlockSpec((pl.Element(tn), pl.Element(K)),
                              lambda j, i: (pl.multiple_of(w_row_start + j * tn, align), 0))
    return pl.pallas_call(
        functools.partial(_matmul_kernel, act=act),
        grid=(N // tn, M // tm),
        in_specs=[pl.BlockSpec((tm, K), lambda j, i: (i, 0)),
                  w_spec,
                  pl.BlockSpec((1, tn), lambda j, i: (0, j)),
                  pl.BlockSpec((1, tn), lambda j, i: (0, j))],
        out_specs=pl.BlockSpec((tm, tn), lambda j, i: (i, j)),
        out_shape=jax.ShapeDtypeStruct((M, N), out_dtype),
        scratch_shapes=[pltpu.VMEM((tn, K), jnp.bfloat16)],
        compiler_params=_params("arbitrary", "arbitrary"),
        name="proj_" + act,
    )(x, w_t, col_scale, col_bias)


def _forget_bias_kernel(x_ref, w_ref, b_ref, xb_ref, o_ref, carry_ref, *, heads):
    @pl.when(pl.program_id(1) == 0)
    def _():
        carry_ref[...] = jnp.zeros_like(carry_ref)

    ts = x_ref.shape[0]
    xb = x_ref[...].astype(xb_ref.dtype)
    xb_ref[...] = xb
    z = lax.dot_general(xb, w_ref[...], _NT_DIMS, preferred_element_type=jnp.float32) + b_ref[...]
    log_f = jnp.minimum(z, 0.0) - jnp.log1p(jnp.exp(-jnp.abs(z)))
    row = lax.broadcasted_iota(jnp.int32, (ts, ts), 0)
    col = lax.broadcasted_iota(jnp.int32, (ts, ts), 1)
    lower = jnp.where(col <= row, 1.0, 0.0).astype(jnp.bfloat16)
    c = carry_ref[...]
    for part in _split3_bf16(log_f):
        c = c + jnp.dot(lower, part.astype(jnp.bfloat16), preferred_element_type=jnp.float32)
    carry_ref[...] = c[ts - 1:ts, :]
    hi, mid, lo = _split3_bf16(c * (-LOG2E))
    lane = lax.broadcasted_iota(jnp.int32, (ts, LANES), 1)
    cols = jnp.where(lane < heads, hi,
                     jnp.where(lane < 2 * heads, pltpu.roll(mid, heads, 1),
                               jnp.where(lane < 3 * heads, pltpu.roll(lo, 2 * heads, 1), 0.0)))
    o_ref[0] = cols.astype(o_ref.dtype)


def _forget_bias_columns(x, w_f, b_forget, heads, batch, ts=512):
    M, D = x.shape
    S = M // batch
    assert 3 * heads <= LANES
    ts = min(ts, S)
    steps = S // ts
    return pl.pallas_call(
        functools.partial(_forget_bias_kernel, heads=heads),
        grid=(batch, steps),
        in_specs=[pl.BlockSpec((ts, D), lambda b, i: (b * steps + i, 0)),
                  pl.BlockSpec((LANES, D), lambda b, i: (0, 0)),
                  pl.BlockSpec((1, LANES), lambda b, i: (0, 0))],
        out_specs=[pl.BlockSpec((ts, D), lambda b, i: (b * steps + i, 0)),
                   pl.BlockSpec((1, ts, LANES), lambda b, i: (b, i, 0))],
        out_shape=[jax.ShapeDtypeStruct((M, D), jnp.bfloat16),
                   jax.ShapeDtypeStruct((batch, S, LANES), jnp.bfloat16)],
        scratch_shapes=[pltpu.VMEM((1, LANES), jnp.float32)],
        compiler_params=_params("parallel", "arbitrary"),
        name="forget_bias_columns",
    )(x, w_f, b_forget)


ATTN_HEADS_PER_STEP = 4


def _softmax_tile(s, v_ext, m_ref, acc_ref):
    tk = s.shape[1]
    chunks = [s[:, c:c + LANES] for c in range(0, tk, LANES)]
    mx = functools.reduce(jnp.maximum, chunks)
    m_prev = m_ref[...]
    m_new = jnp.maximum(m_prev, jnp.max(mx, axis=-1, keepdims=True))
    alpha = jnp.exp2(m_prev - m_new)
    p = jnp.concatenate([jnp.exp2(ch - m_new) for ch in chunks], axis=1).astype(v_ext.dtype)
    pv = jnp.dot(p, v_ext, preferred_element_type=jnp.float32)
    acc_ref[...] = acc_ref[...] * jnp.concatenate([alpha, alpha], axis=1) + pv
    m_ref[...] = m_new


class _Tile(NamedTuple):
    qk: Callable[[], Any]
    post: Callable[[Any], Any]
    values: Callable[[], Any]


class _HeadTiles(NamedTuple):
    specials: Sequence[_Tile]
    qk: Callable[[Any], Any]
    values: Callable[[Any], Any]
    s_refs: Sequence[Any]
    m_ref: Any
    acc_ref: Any


def _pipelined_tiles(heads, n_rest):
    def fill(buf, logits_of):
        for hd in heads:
            s = logits_of(hd)
            hd.s_refs[buf][:, :s.shape[1]] = s

    def consume(buf, values_of, post=lambda hd, s: s):
        for hd in heads:
            v_ext = values_of(hd)
            s = hd.s_refs[buf][:, :v_ext.shape[0]]
            _softmax_tile(post(hd, s), v_ext, hd.m_ref, hd.acc_ref)

    n_special = len(heads[0].specials)
    no_uniform_tiles = isinstance(n_rest, int) and n_rest == 0
    fill(0, lambda hd: hd.specials[0].qk())
    for t in range(n_special):
        if t + 1 < n_special:
            fill((t + 1) % 2, lambda hd: hd.specials[t + 1].qk())
        elif not no_uniform_tiles:
            fill((t + 1) % 2, lambda hd: hd.qk(0))
        consume(t % 2, lambda hd: hd.specials[t].values(), lambda hd, s: hd.specials[t].post(s))
    if no_uniform_tiles:
        return
    cur, nxt = n_special % 2, (n_special + 1) % 2

    def two_tiles(i, carry):
        fill(nxt, lambda hd: hd.qk(2 * i + 1))
        consume(cur, lambda hd: hd.values(2 * i))
        fill(cur, lambda hd: hd.qk(jnp.minimum(2 * i + 2, n_rest - 1)))
        consume(nxt, lambda hd: hd.values(2 * i + 1))
        return carry

    lax.fori_loop(0, n_rest // 2, two_tiles, 0)

    @pl.when(n_rest % 2 == 1)
    def _():
        consume(cur, lambda hd: hd.values(n_rest - 1))


def _init_softmax_state(m_ref, acc_ref):
    m_ref[...] = jnp.full_like(m_ref, NEG_BIG)
    acc_ref[...] = jnp.zeros_like(acc_ref)


def _finish_attention(acc_ref, z, dtype):
    dh = z.shape[1]
    return (acc_ref[:, :dh] / acc_ref[:, dh:] * z).astype(dtype)


def _fox_kernel(q_ref, k_ref, cx_ref, v_ref, z_ref, o_ref, kaug_ref, vext_ref, sa_ref, sb_ref, m_ref, acc_ref,
                *, heads):
    qi = pl.program_id(2)
    tq = q_ref.shape[1]
    dh = HEAD_DIM
    G = kaug_ref.shape[0]

    def rows(j):
        return pl.ds(pl.multiple_of(j * tq, tq), tq)

    for g in range(G):
        kaug_ref[g, rows(qi), :dh] = k_ref[0, :, g * dh:(g + 1) * dh]
        kaug_ref[g, rows(qi), dh:] = cx_ref[0]
        vext_ref[g, rows(qi), :dh] = v_ref[0, :, g * dh:(g + 1) * dh]
        vext_ref[g, rows(qi), dh:] = jnp.ones((tq, dh), vext_ref.dtype)

    def causal(s):
        row = lax.broadcasted_iota(jnp.int32, (tq, tq), 0)
        col = lax.broadcasted_iota(jnp.int32, (tq, tq), 1)
        return jnp.where(col <= row, s, NEG_BIG)

    lane = lax.broadcasted_iota(jnp.int32, (1, LANES), 1)
    tiles = []
    for g in range(G):
        h = pl.program_id(1) * G + g
        pick = (lane == h) | (lane == heads + h) | (lane == 2 * heads + h)
        ones_at_head_terms = jnp.broadcast_to(jnp.where(pick, 1.0, 0.0), (tq, LANES)).astype(q_ref.dtype)
        q_aug = jnp.concatenate([q_ref[0, :, g * dh:(g + 1) * dh], ones_at_head_terms], axis=1)
        _init_softmax_state(m_ref.at[g], acc_ref.at[g])

        def logits(j, g=g, q_aug=q_aug):
            return lax.dot_general(q_aug, kaug_ref[g, rows(j), :], _NT_DIMS, preferred_element_type=jnp.float32)

        def values(j, g=g):
            return vext_ref[g, rows(j), :]

        diagonal = _Tile(functools.partial(logits, qi), causal, functools.partial(values, qi))
        tiles.append(_HeadTiles([diagonal], logits, values, (sa_ref.at[g], sb_ref.at[g]), m_ref.at[g], acc_ref.at[g]))

    @pl.when(qi == 0)
    def _():
        _pipelined_tiles(tiles, 0)

    @pl.when(qi >= 1)
    def _():
        _pipelined_tiles(tiles, qi)

    for g in range(G):
        o_ref[0, :, g * dh:(g + 1) * dh] = _finish_attention(acc_ref.at[g], z_ref[0, :, g * dh:(g + 1) * dh],
                                                             o_ref.dtype)


def _fox_attention(qkv, cx, sz, H, col0, zcol0, tq=512):
    B, S, _ = qkv.shape
    tq = min(tq, S)
    Dh = HEAD_DIM
    G = ATTN_HEADS_PER_STEP
    assert H % G == 0 and col0 % G == 0 and zcol0 % G == 0
    W = G * Dh
    return pl.pallas_call(
        functools.partial(_fox_kernel, heads=H),
        grid=(B, H // G, S // tq),
        in_specs=[pl.BlockSpec((1, tq, W), lambda b, h, i: (b, i, col0 // G + h)),
                  pl.BlockSpec((1, tq, W), lambda b, h, i: (b, i, (col0 + H) // G + h)),
                  pl.BlockSpec((1, tq, LANES), lambda b, h, i: (b, i, 0)),
                  pl.BlockSpec((1, tq, W), lambda b, h, i: (b, i, (col0 + 2 * H) // G + h)),
                  pl.BlockSpec((1, tq, W), lambda b, h, i: (b, i, zcol0 // G + h))],
        out_specs=pl.BlockSpec((1, tq, W), lambda b, h, i: (b, i, h)),
        out_shape=jax.ShapeDtypeStruct((B, S, H * Dh), jnp.bfloat16),
        scratch_shapes=[pltpu.VMEM((G, S, Dh + LANES), jnp.bfloat16), pltpu.VMEM((G, S, 2 * Dh), jnp.bfloat16),
                        pltpu.VMEM((G, tq, tq), jnp.float32), pltpu.VMEM((G, tq, tq), jnp.float32),
                        pltpu.VMEM((G, tq, LANES), jnp.float32), pltpu.VMEM((G, tq, 2 * Dh), jnp.float32)],
        compiler_params=_params("arbitrary", "arbitrary", "arbitrary"),
        name="fox_attention",
    )(qkv, qkv, cx, qkv, sz)


def _t5_bucket_np(dist):
    max_exact = T5_NUM_BUCKETS // 2
    d = np.maximum(dist, 1).astype(np.float32)
    ratio = (np.log(d / np.float32(max_exact)) / np.float32(math.log(T5_MAX_DISTANCE / max_exact))
             * np.float32(T5_NUM_BUCKETS - max_exact))
    large = np.minimum(max_exact + ratio.astype(np.int32), T5_NUM_BUCKETS - 1)
    return np.where(dist < max_exact, dist, large).astype(np.int32)


MOBA_Q_BLOCKS = 2


def _moba_query_columns(q, kmean, far_row, first_block, blk, nb):
    nbp, tq = kmean.shape[0], q.shape[0]
    km_hi = kmean.astype(jnp.bfloat16)
    km_lo = (kmean - km_hi.astype(jnp.float32)).astype(jnp.bfloat16)
    gate = (lax.dot_general(km_hi, q, _NT_DIMS, preferred_element_type=jnp.float32)
            + lax.dot_general(km_lo, q, _NT_DIMS, preferred_element_type=jnp.float32))
    n_idx = lax.broadcasted_iota(jnp.int32, (nbp, tq), 0)
    own = first_block + lax.broadcasted_iota(jnp.int32, (nbp, tq), 1) // blk
    past = n_idx < own
    gate = jnp.where(past, gate, NEG_BIG)
    beaten = jnp.zeros((nbp, tq), jnp.int32)
    for n in range(nb):
        g_n = gate[n:n + 1, :]
        lower_index_wins_ties = jnp.where(n_idx > n, 1, 0)
        beaten = beaten + jnp.where(g_n > gate, 1, 0) + jnp.where(g_n == gate, lower_index_wins_ties, 0)
    selected = past & (beaten < MOBA_TOPK)
    far_bias = jnp.broadcast_to(far_row, (nbp, tq))
    term = jnp.where(selected, jnp.where(n_idx == own - 1, 0.0, far_bias), NEG_BIG)
    term = jnp.where(n_idx == own, 0.0, term)
    term_hi = term.astype(jnp.bfloat16).astype(jnp.float32)
    term_lo = jnp.where(selected, term - term_hi, 0.0)
    pieces = [term_hi[:nb], term_lo[:nb]]
    if 2 * nb < LANES:
        pieces.append(jnp.zeros((LANES - 2 * nb, tq), jnp.float32))
    return jnp.concatenate(pieces, axis=0).T.astype(q.dtype)


def _moba_kernel(q_ref, k_ref, v_ref, rbias_ref, far_ref, z_ref, o_ref,
                 kaug_ref, vext_ref, kmean_ref, t5_ref, sa_ref, sb_ref, m_ref, acc_ref, *, nb, blk):
    qi = pl.program_id(2)
    tq = q_ref.shape[1]
    dh = HEAD_DIM
    G = kaug_ref.shape[0]
    first_block = qi * (tq // blk)

    def rows(j):
        return pl.ds(pl.multiple_of(j * tq, tq), tq)

    @pl.when(qi == 0)
    def _():
        for g in range(G):
            kmean_ref[g] = jnp.zeros(kmean_ref.shape[1:], kmean_ref.dtype)
            rb = jnp.broadcast_to(rbias_ref[g], (tq, tq))
            t5_ref[g] = pltpu.roll(rb, 1, 1, stride=1, stride_axis=0)

    key_block = first_block + lax.broadcasted_iota(jnp.int32, (tq, LANES), 0) // blk
    lane = lax.broadcasted_iota(jnp.int32, (1, LANES), 1)
    lane_block = jnp.where(lane < nb, lane, jnp.where(lane < 2 * nb, lane - nb, -1))
    hot = jnp.where(jnp.broadcast_to(lane_block, (tq, LANES)) == key_block, 1.0, 0.0).astype(kaug_ref.dtype)
    for g in range(G):
        k = k_ref[0, :, g * dh:(g + 1) * dh]
        kaug_ref[g, rows(qi), :dh] = k
        kaug_ref[g, rows(qi), dh:] = hot
        vext_ref[g, rows(qi), :dh] = v_ref[0, :, g * dh:(g + 1) * dh]
        vext_ref[g, rows(qi), dh:] = jnp.ones((tq, dh), vext_ref.dtype)
        for n in range(tq // blk):
            kb = k[n * blk:(n + 1) * blk, :].astype(jnp.float32)
            kmean_ref[g, pl.ds(first_block + n, 1), :] = jnp.sum(kb, axis=0, keepdims=True) * (1.0 / blk)

    def causal_with_bias(s, g):
        row = lax.broadcasted_iota(jnp.int32, (tq, tq), 0)
        col = lax.broadcasted_iota(jnp.int32, (tq, tq), 1)
        return jnp.where(col <= row, s + t5_ref[g], NEG_BIG)

    def previous_with_bias(s, g):
        top = jnp.concatenate([s[:blk, :tq - blk], s[:blk, tq - blk:] + t5_ref[g, tq - blk:, :blk]], axis=1)
        return jnp.concatenate([top, s[blk:, :]], axis=0)

    tiles = []
    for g in range(G):
        q = q_ref[0, :, g * dh:(g + 1) * dh]
        q_extra = _moba_query_columns(q, kmean_ref[g], far_ref[g], first_block, blk, nb)
        q_aug = jnp.concatenate([q, q_extra], axis=1)
        _init_softmax_state(m_ref.at[g], acc_ref.at[g])

        def logits(j, g=g, q_aug=q_aug):
            return lax.dot_general(q_aug, kaug_ref[g, rows(j), :], _NT_DIMS, preferred_element_type=jnp.float32)

        def values(j, g=g):
            return vext_ref[g, rows(j), :]

        diagonal = _Tile(functools.partial(logits, qi), functools.partial(causal_with_bias, g=g),
                         functools.partial(values, qi))
        previous = _Tile(functools.partial(logits, qi - 1), functools.partial(previous_with_bias, g=g),
                         functools.partial(values, qi - 1))
        tiles.append((diagonal, previous, logits, values, (sa_ref.at[g], sb_ref.at[g]), m_ref.at[g], acc_ref.at[g]))

    @pl.when(qi == 0)
    def _():
        _pipelined_tiles([_HeadTiles([t[0]], *t[2:]) for t in tiles], 0)

    @pl.when(qi >= 1)
    def _():
        _pipelined_tiles([_HeadTiles([t[0], t[1]], *t[2:]) for t in tiles], qi - 1)

    for g in range(G):
        o_ref[0, :, g * dh:(g + 1) * dh] = _finish_attention(acc_ref.at[g], z_ref[0, :, g * dh:(g + 1) * dh],
                                                             o_ref.dtype)


def _moba_attention(qkv, rel_bias_table, sz, H, col0, zcol0):
    B, S, _ = qkv.shape
    blk = MOBA_BLOCK
    tq = MOBA_Q_BLOCKS * blk
    assert MOBA_Q_BLOCKS == 2 and S % tq == 0
    nb = S // blk
    nbp = -(-nb // 8) * 8
    assert 2 * nb <= LANES
    Dh = HEAD_DIM
    G = ATTN_HEADS_PER_STEP
    assert H % G == 0 and col0 % G == 0 and zcol0 % G == 0
    W = G * Dh
    table_t = rel_bias_table.T.astype(jnp.float32) * LOG2E
    rbias = table_t[:, _t5_bucket_np(np.arange(tq)[::-1])][:, None, :]
    far_bucket = int(_t5_bucket_np(np.array([blk + 1]))[0])
    assert (_t5_bucket_np(np.arange(blk + 1, S + blk)) == far_bucket).all()
    far = jnp.broadcast_to(table_t[:, far_bucket][:, None, None], (H, 1, tq))
    return pl.pallas_call(
        functools.partial(_moba_kernel, nb=nb, blk=blk),
        grid=(B, H // G, S // tq),
        in_specs=[pl.BlockSpec((1, tq, W), lambda b, h, i: (b, i, col0 // G + h)),
                  pl.BlockSpec((1, tq, W), lambda b, h, i: (b, i, (col0 + H) // G + h)),
                  pl.BlockSpec((1, tq, W), lambda b, h, i: (b, i, (col0 + 2 * H) // G + h)),
                  pl.BlockSpec((G, 1, tq), lambda b, h, i: (h, 0, 0)),
                  pl.BlockSpec((G, 1, tq), lambda b, h, i: (h, 0, 0)),
                  pl.BlockSpec((1, tq, W), lambda b, h, i: (b, i, zcol0 // G + h))],
        out_specs=pl.BlockSpec((1, tq, W), lambda b, h, i: (b, i, h)),
        out_shape=jax.ShapeDtypeStruct((B, S, H * Dh), jnp.bfloat16),
        scratch_shapes=[pltpu.VMEM((G, S, Dh + LANES), jnp.bfloat16), pltpu.VMEM((G, S, 2 * Dh), jnp.bfloat16),
                        pltpu.VMEM((G, nbp, Dh), jnp.float32), pltpu.VMEM((G, tq, tq), jnp.float32),
                        pltpu.VMEM((G, tq, tq), jnp.float32), pltpu.VMEM((G, tq, tq), jnp.float32),
                        pltpu.VMEM((G, tq, LANES), jnp.float32), pltpu.VMEM((G, tq, 2 * Dh), jnp.float32)],
        compiler_params=_params("arbitrary", "arbitrary", "arbitrary"),
        name="moba_attention",
    )(qkv, qkv, qkv, rbias, far, sz)


def _branch_merge_kernel(ya_ref, yf_ref, wa_ref, wf_ref, ga_ref, gf_ref, o_ref):
    ua = jnp.dot(ya_ref[...], wa_ref[0], preferred_element_type=jnp.float32)
    uf = jnp.dot(yf_ref[...], wf_ref[0], preferred_element_type=jnp.float32)
    o_ref[...] = (ga_ref[...] * ua + gf_ref[...] * uf).astype(o_ref.dtype)


def _branch_merge(ya, yf, w, gates, tm=1024, tn=512):
    M, W = ya.shape
    D = w.shape[2]
    tm, tn = min(tm, M), min(tn, D)
    nj = D // tn
    return pl.pallas_call(
        _branch_merge_kernel,
        grid=(M // tm, nj),
        in_specs=[pl.BlockSpec((tm, W), lambda i, j: (i, 0)),
                  pl.BlockSpec((tm, W), lambda i, j: (i, 0)),
                  pl.BlockSpec((1, W, tn), lambda i, j: (0, 0, j)),
                  pl.BlockSpec((1, W, tn), lambda i, j: (1, 0, j)),
                  pl.BlockSpec((tm, tn), lambda i, j: (i, j)),
                  pl.BlockSpec((tm, tn), lambda i, j: (i, nj + j))],
        out_specs=pl.BlockSpec((tm, tn), lambda i, j: (i, j)),
        out_shape=jax.ShapeDtypeStruct((M, D), jnp.bfloat16),
        compiler_params=_params("parallel", "parallel"),
        name="branch_merge",
    )(ya, yf, w, w, gates, gates)


LN_ROW_CHUNK = 32


def _out_layernorm_kernel(h_ref, w_ref, x_ref, g_ref, b_ref, o_ref, pre_even_ref, pre_odd_ref):
    i, j = pl.program_id(0), pl.program_id(1)
    tn = w_ref.shape[1]
    chunk = o_ref.shape[0]

    @pl.when((i == 0) & (j == 0))
    def _():
        pre_odd_ref[...] = jnp.zeros_like(pre_odd_ref)

    def step(cur_ref, prev_ref):
        rows = min(LN_ROW_CHUNK, chunk)
        for c in range(chunk // rows):
            r = prev_ref[pl.ds(pl.multiple_of(j * chunk + c * rows, rows), rows), :]
            mu = jnp.mean(r, axis=-1, keepdims=True)
            d = r - mu
            var = jnp.mean(d * d, axis=-1, keepdims=True)
            o_ref[c * rows:(c + 1) * rows, :] = d * lax.rsqrt(var + LN_EPS) * g_ref[...] + b_ref[...]
        out = jnp.dot(h_ref[...], w_ref[...], preferred_element_type=jnp.float32)
        cur_ref[:, pl.ds(pl.multiple_of(j * tn, tn), tn)] = DEEPNORM_ALPHA * x_ref[...] + out

    @pl.when(i % 2 == 0)
    def _():
        step(pre_even_ref, pre_odd_ref)

    @pl.when(i % 2 == 1)
    def _():
        step(pre_odd_ref, pre_even_ref)


def _out_layernorm(h, w, x, gain, bias, tm=512, tn=1024):
    M, D = x.shape
    tm, tn = min(tm, M), min(tn, D)
    n_i, n_j = M // tm, D // tn
    chunk = tm // n_j
    assert chunk % 8 == 0
    return pl.pallas_call(
        _out_layernorm_kernel,
        grid=(n_i + 1, n_j),
        in_specs=[pl.BlockSpec((tm, D), lambda i, j: (jnp.minimum(i, n_i - 1), 0)),
                  pl.BlockSpec((D, tn), lambda i, j: (0, j)),
                  pl.BlockSpec((tm, tn), lambda i, j: (jnp.minimum(i, n_i - 1), j)),
                  pl.BlockSpec((1, D), lambda i, j: (0, 0)),
                  pl.BlockSpec((1, D), lambda i, j: (0, 0))],
        out_specs=pl.BlockSpec((chunk, D), lambda i, j: (jnp.where(i == 0, 0, (i - 1) * n_j + j), 0)),
        out_shape=jax.ShapeDtypeStruct((M, D), jnp.float32),
        scratch_shapes=[pltpu.VMEM((tm, D), jnp.float32), pltpu.VMEM((tm, D), jnp.float32)],
        compiler_params=_params("arbitrary", "arbitrary"),
        name="out_layernorm",
    )(h, w, x, gain, bias)


def kernel(x, w_in, b_forget, b_gate, rel_bias_table, w_branch, w_out, ln_gain, ln_bias):
    B, S, D = x.shape
    Ha, Hf = MOBA_HEADS, FOX_HEADS
    Wa, Wf = Ha * HEAD_DIM, Hf * HEAD_DIM
    bf16, f32 = jnp.bfloat16, jnp.float32
    depth = w_in.shape[0]
    for layer in range(depth):
        w_t = jnp.swapaxes(w_in[layer], 0, 1).astype(f32)
        n_attn = 4 * Wa + 4 * Wf
        w_f = jnp.pad(w_t[n_attn:n_attn + Hf], ((0, LANES - Hf), (0, 0))).astype(bf16)
        b_f = jnp.pad(b_forget[layer].astype(f32), (0, LANES - Hf))[None, :]
        x2 = x.reshape(B * S, D).astype(f32)
        xb, cx = _forget_bias_columns(x2, w_f, b_f, Hf, B)

        scale = HEAD_DIM ** -0.5 * LOG2E
        qkv_scale = jnp.concatenate([jnp.full((Wa,), scale, f32), jnp.ones((2 * Wa,), f32),
                                     jnp.full((Wf,), scale, f32), jnp.ones((2 * Wf,), f32)])[None, :]
        tn = PROJ_TN
        assert Wa % tn == 0 and Wf % tn == 0
        a_blocks, f_blocks = Wa // tn, Wf // tn
        qkv = _matmul(xb, w_t, qkv_scale, jnp.zeros_like(qkv_scale), bf16, "none",
                      lambda j: jnp.where(j < 3 * a_blocks, j, j + a_blocks), tn=tn).reshape(B, S, -1)
        ones_z = jnp.ones((1, Wa + Wf), f32)
        sz = _matmul(xb, w_t, ones_z, jnp.zeros_like(ones_z), f32, "silu",
                     lambda j: jnp.where(j < a_blocks, j + 3 * a_blocks, j + 3 * a_blocks + 3 * f_blocks),
                     tn=tn).reshape(B, S, -1)
        gates = _matmul(xb, w_t, jnp.ones((1, 2 * D), f32), b_gate[layer].reshape(1, 2 * D).astype(f32),
                        f32, "sigmoid", w_row_start=n_attn + Hf, tn=tn)

        ya = _moba_attention(qkv, rel_bias_table, sz, Ha, 0, 0)
        yf = _fox_attention(qkv, cx, sz, Hf, 3 * Ha, Ha)

        assert Wa == Wf
        merged = _branch_merge(ya.reshape(B * S, Wa), yf.reshape(B * S, Wf), w_branch[layer].astype(bf16), gates)
        y = _out_layernorm(merged, w_out[layer].astype(bf16), x2,
                           ln_gain[layer].reshape(1, D).astype(f32), ln_bias[layer].reshape(1, D).astype(f32))
        x = y.reshape(B, S, D).astype(x.dtype)
    return x
```

```python
import functools
import math
from typing import Any, Callable, NamedTuple, Sequence

import jax
import jax.numpy as jnp
import numpy as np
from jax import lax
from jax.experimental import pallas as pl
from jax.experimental.pallas import tpu as pltpu

HEAD_DIM = 128
LANES = 128
MOBA_HEADS = 16
FOX_HEADS = 16
MOBA_BLOCK = 256
MOBA_TOPK = 3
T5_NUM_BUCKETS = 32
T5_MAX_DISTANCE = 128
LN_EPS = 1e-5
DEPTH = 1
DEEPNORM_ALPHA = (2.0 * DEPTH) ** 0.25
LOG2E = math.log2(math.e)

V7X_VMEM_LIMIT_BYTES = 56 * 1024 * 1024
PROJ_TN = 1024
NEG_BIG = -1e30

_NT_DIMS = (((1,), (1,)), ((), ()))


def _params(*semantics):
    return pltpu.CompilerParams(dimension_semantics=semantics,
                                vmem_limit_bytes=V7X_VMEM_LIMIT_BYTES)


def _split3_bf16(a):
    hi = a.astype(jnp.bfloat16).astype(jnp.float32)
    r = a - hi
    mid = r.astype(jnp.bfloat16).astype(jnp.float32)
    lo = (r - mid).astype(jnp.bfloat16).astype(jnp.float32)
    return hi, mid, lo


def _matmul_kernel(x_ref, wt_ref, s_ref, b_ref, o_ref, *, act):
    acc = lax.dot_general(x_ref[...], wt_ref[...], _NT_DIMS, preferred_element_type=jnp.float32)
    acc = acc * s_ref[...] + b_ref[...]
    if act == "silu":
        acc = acc * jax.nn.sigmoid(acc)
    elif act == "sigmoid":
        acc = jax.nn.sigmoid(acc)
    o_ref[...] = acc.astype(o_ref.dtype)


def _matmul(x, w_t, col_scale, col_bias, out_dtype, act, w_row_block=lambda j: j, w_row_start=None,
            tm=1024, tn=1024):
    M, K = x.shape
    N = col_scale.shape[1]
    tm, tn = min(tm, M), min(tn, N)
    if w_row_start is None:
        w_spec = pl.BlockSpec((tn, K), lambda i, j: (w_row_block(j), 0))
    else:
        align = math.gcd(w_row_start, tn)
        w_spec = pl.BlockSpec((pl.Element(tn), pl.Element(K)),
                              lambda i, j: (pl.multiple_of(w_row_start + j * tn, align), 0))
    return pl.pallas_call(
        functools.partial(_matmul_kernel, act=act),
        grid=(M // tm, N // tn),
        in_specs=[pl.BlockSpec((tm, K), lambda i, j: (i, 0)),
                  w_spec,
                  pl.BlockSpec((1, tn), lambda i, j: (0, j)),
                  pl.BlockSpec((1, tn), lambda i, j: (0, j))],
        out_specs=pl.BlockSpec((tm, tn), lambda i, j: (i, j)),
        out_shape=jax.ShapeDtypeStruct((M, N), out_dtype),
        compiler_params=_params("parallel", "parallel"),
        name="proj_" + act,
    )(x, w_t, col_scale, col_bias)


def _forget_bias_kernel(x_ref, w_ref, b_ref, xb_ref, o_ref, carry_ref, *, heads):
    @pl.when(pl.program_id(1) == 0)
    def _():
        carry_ref[...] = jnp.zeros_like(carry_ref)

    ts = x_ref.shape[0]
    xb = x_ref[...].astype(xb_ref.dtype)
    xb_ref[...] = xb
    z = lax.dot_general(xb, w_ref[...], _NT_DIMS, preferred_element_type=jnp.float32) + b_ref[...]
    log_f = jnp.minimum(z, 0.0) - jnp.log1p(jnp.exp(-jnp.abs(z)))
    row = lax.broadcasted_iota(jnp.int32, (ts, ts), 0)
    col = lax.broadcasted_iota(jnp.int32, (ts, ts), 1)
    lower = jnp.where(col <= row, 1.0, 0.0).astype(jnp.bfloat16)
    c = carry_ref[...]
    for part in _split3_bf16(log_f):
        c = c + jnp.dot(lower, part.astype(jnp.bfloat16), preferred_element_type=jnp.float32)
    carry_ref[...] = c[ts - 1:ts, :]
    hi, mid, lo = _split3_bf16(c * (-LOG2E))
    lane = lax.broadcasted_iota(jnp.int32, (ts, LANES), 1)
    cols = jnp.where(lane < heads, hi,
                     jnp.where(lane < 2 * heads, pltpu.roll(mid, heads, 1),
                               jnp.where(lane < 3 * heads, pltpu.roll(lo, 2 * heads, 1), 0.0)))
    o_ref[0] = cols.astype(o_ref.dtype)


def _forget_bias_columns(x, w_f, b_forget, heads, batch, ts=512):
    M, D = x.shape
    S = M // batch
    assert 3 * heads <= LANES
    ts = min(ts, S)
    steps = S // ts
    return pl.pallas_call(
        functools.partial(_forget_bias_kernel, heads=heads),
        grid=(batch, steps),
        in_specs=[pl.BlockSpec((ts, D), lambda b, i: (b * steps + i, 0)),
                  pl.BlockSpec((LANES, D), lambda b, i: (0, 0)),
                  pl.BlockSpec((1, LANES), lambda b, i: (0, 0))],
        out_specs=[pl.BlockSpec((ts, D), lambda b, i: (b * steps + i, 0)),
                   pl.BlockSpec((1, ts, LANES), lambda b, i: (b, i, 0))],
        out_shape=[jax.ShapeDtypeStruct((M, D), jnp.bfloat16),
                   jax.ShapeDtypeStruct((batch, S, LANES), jnp.bfloat16)],
        scratch_shapes=[pltpu.VMEM((1, LANES), jnp.float32)],
        compiler_params=_params("parallel", "arbitrary"),
        name="forget_bias_columns",
    )(x, w_f, b_forget)


ATTN_HEADS_PER_STEP = 4


def _softmax_tile(s, v_ext, m_ref, acc_ref, causal_halves=False):
    tk = s.shape[1]
    chunks = [s[:, c:c + LANES] for c in range(0, tk, LANES)]
    mx = functools.reduce(jnp.maximum, chunks)
    m_prev = m_ref[...]
    m_new = jnp.maximum(m_prev, jnp.max(mx, axis=-1, keepdims=True))
    alpha = jnp.exp2(m_prev - m_new)
    p = jnp.concatenate([jnp.exp2(ch - m_new) for ch in chunks], axis=1).astype(v_ext.dtype)
    if causal_halves:
        half = s.shape[0] // 2
        pv = jnp.concatenate([jnp.dot(p[:half, :half], v_ext[:half], preferred_element_type=jnp.float32),
                              jnp.dot(p[half:], v_ext, preferred_element_type=jnp.float32)], axis=0)
    else:
        pv = jnp.dot(p, v_ext, preferred_element_type=jnp.float32)
    acc_ref[...] = acc_ref[...] * jnp.concatenate([alpha, alpha], axis=1) + pv
    m_ref[...] = m_new


def _causal_tile_logits(q_aug, k):
    half = q_aug.shape[0] // 2
    top = lax.dot_general(q_aug[:half], k[:half], _NT_DIMS, preferred_element_type=jnp.float32)
    bottom = lax.dot_general(q_aug[half:], k, _NT_DIMS, preferred_element_type=jnp.float32)
    masked = jnp.full((half, k.shape[0] - half), NEG_BIG, jnp.float32)
    return jnp.concatenate([jnp.concatenate([top, masked], axis=1), bottom], axis=0)


class _Tile(NamedTuple):
    qk: Callable[[], Any]
    post: Callable[[Any], Any]
    values: Callable[[], Any]
    causal_halves: bool = False


class _HeadTiles(NamedTuple):
    specials: Sequence[_Tile]
    qk: Callable[[Any], Any]
    values: Callable[[Any], Any]
    s_refs: Sequence[Any]
    m_ref: Any
    acc_ref: Any


def _pipelined_tiles(heads, n_rest):
    def fill(buf, logits_of):
        for hd in heads:
            s = logits_of(hd)
            hd.s_refs[buf][:, :s.shape[1]] = s

    def consume(buf, values_of, post=lambda hd, s: s, causal_halves=False):
        for hd in heads:
            v_ext = values_of(hd)
            s = hd.s_refs[buf][:, :v_ext.shape[0]]
            _softmax_tile(post(hd, s), v_ext, hd.m_ref, hd.acc_ref, causal_halves)

    n_special = len(heads[0].specials)
    no_uniform_tiles = isinstance(n_rest, int) and n_rest == 0
    fill(0, lambda hd: hd.specials[0].qk())
    for t in range(n_special):
        if t + 1 < n_special:
            fill((t + 1) % 2, lambda hd: hd.specials[t + 1].qk())
        elif not no_uniform_tiles:
            fill((t + 1) % 2, lambda hd: hd.qk(0))
        consume(t % 2, lambda hd: hd.specials[t].values(), lambda hd, s: hd.specials[t].post(s),
                heads[0].specials[t].causal_halves)
    if no_uniform_tiles:
        return
    cur, nxt = n_special % 2, (n_special + 1) % 2

    def two_tiles(i, carry):
        fill(nxt, lambda hd: hd.qk(2 * i + 1))
        consume(cur, lambda hd: hd.values(2 * i))
        fill(cur, lambda hd: hd.qk(jnp.minimum(2 * i + 2, n_rest - 1)))
        consume(nxt, lambda hd: hd.values(2 * i + 1))
        return carry

    lax.fori_loop(0, n_rest // 2, two_tiles, 0)

    @pl.when(n_rest % 2 == 1)
    def _():
        consume(cur, lambda hd: hd.values(n_rest - 1))


def _init_softmax_state(m_ref, acc_ref):
    m_ref[...] = jnp.full_like(m_ref, NEG_BIG)
    acc_ref[...] = jnp.zeros_like(acc_ref)


def _finish_attention(acc_ref, z, dtype):
    dh = z.shape[1]
    return (acc_ref[:, :dh] / acc_ref[:, dh:] * z).astype(dtype)


def _fox_kernel(q_ref, k_ref, cx_ref, v_ref, z_ref, o_ref, kaug_ref, vext_ref, sa_ref, sb_ref, m_ref, acc_ref,
                *, heads):
    qi = pl.program_id(2)
    tq = q_ref.shape[1]
    dh = HEAD_DIM
    G = kaug_ref.shape[0]

    def rows(j):
        return pl.ds(pl.multiple_of(j * tq, tq), tq)

    for g in range(G):
        kaug_ref[g, rows(qi), :dh] = k_ref[0, :, g * dh:(g + 1) * dh]
        kaug_ref[g, rows(qi), dh:] = cx_ref[0]
        vext_ref[g, rows(qi), :dh] = v_ref[0, :, g * dh:(g + 1) * dh]
        vext_ref[g, rows(qi), dh:] = jnp.ones((tq, dh), vext_ref.dtype)

    def causal(s):
        row = lax.broadcasted_iota(jnp.int32, (tq, tq), 0)
        col = lax.broadcasted_iota(jnp.int32, (tq, tq), 1)
        return jnp.where(col <= row, s, NEG_BIG)

    lane = lax.broadcasted_iota(jnp.int32, (1, LANES), 1)
    tiles = []
    for g in range(G):
        h = pl.program_id(1) * G + g
        pick = (lane == h) | (lane == heads + h) | (lane == 2 * heads + h)
        ones_at_head_terms = jnp.broadcast_to(jnp.where(pick, 1.0, 0.0), (tq, LANES)).astype(q_ref.dtype)
        q_aug = jnp.concatenate([q_ref[0, :, g * dh:(g + 1) * dh], ones_at_head_terms], axis=1)
        _init_softmax_state(m_ref.at[g], acc_ref.at[g])

        def logits(j, g=g, q_aug=q_aug):
            return lax.dot_general(q_aug, kaug_ref[g, rows(j), :], _NT_DIMS, preferred_element_type=jnp.float32)

        def values(j, g=g):
            return vext_ref[g, rows(j), :]

        def diagonal_logits(g=g, q_aug=q_aug):
            return _causal_tile_logits(q_aug, kaug_ref[g, rows(qi), :])

        diagonal = _Tile(diagonal_logits, causal, functools.partial(values, qi), causal_halves=True)
        tiles.append(_HeadTiles([diagonal], logits, values, (sa_ref.at[g], sb_ref.at[g]), m_ref.at[g], acc_ref.at[g]))

    @pl.when(qi == 0)
    def _():
        _pipelined_tiles(tiles, 0)

    @pl.when(qi >= 1)
    def _():
        _pipelined_tiles(tiles, qi)

    for g in range(G):
        o_ref[0, :, g * dh:(g + 1) * dh] = _finish_attention(acc_ref.at[g], z_ref[0, :, g * dh:(g + 1) * dh],
                                                             o_ref.dtype)


def _fox_attention(qkv, cx, sz, H, col0, zcol0, tq=512):
    B, S, _ = qkv.shape
    tq = min(tq, S)
    Dh = HEAD_DIM
    G = ATTN_HEADS_PER_STEP
    assert H % G == 0 and col0 % G == 0 and zcol0 % G == 0
    W = G * Dh
    return pl.pallas_call(
        functools.partial(_fox_kernel, heads=H),
        grid=(B, H // G, S // tq),
        in_specs=[pl.BlockSpec((1, tq, W), lambda b, h, i: (b, i, col0 // G + h)),
                  pl.BlockSpec((1, tq, W), lambda b, h, i: (b, i, (col0 + H) // G + h)),
                  pl.BlockSpec((1, tq, LANES), lambda b, h, i: (b, i, 0)),
                  pl.BlockSpec((1, tq, W), lambda b, h, i: (b, i, (col0 + 2 * H) // G + h)),
                  pl.BlockSpec((1, tq, W), lambda b, h, i: (b, i, zcol0 // G + h))],
        out_specs=pl.BlockSpec((1, tq, W), lambda b, h, i: (b, i, h)),
        out_shape=jax.ShapeDtypeStruct((B, S, H * Dh), jnp.bfloat16),
        scratch_shapes=[pltpu.VMEM((G, S, Dh + LANES), jnp.bfloat16), pltpu.VMEM((G, S, 2 * Dh), jnp.bfloat16),
                        pltpu.VMEM((G, tq, tq), jnp.float32), pltpu.VMEM((G, tq, tq), jnp.float32),
                        pltpu.VMEM((G, tq, LANES), jnp.float32), pltpu.VMEM((G, tq, 2 * Dh), jnp.float32)],
        compiler_params=_params("arbitrary", "arbitrary", "arbitrary"),
        name="fox_attention",
    )(qkv, qkv, cx, qkv, sz)


def _t5_bucket_np(dist):
    max_exact = T5_NUM_BUCKETS // 2
    d = np.maximum(dist, 1).astype(np.float32)
    ratio = (np.log(d / np.float32(max_exact)) / np.float32(math.log(T5_MAX_DISTANCE / max_exact))
             * np.float32(T5_NUM_BUCKETS - max_exact))
    large = np.minimum(max_exact + ratio.astype(np.int32), T5_NUM_BUCKETS - 1)
    return np.where(dist < max_exact, dist, large).astype(np.int32)


MOBA_Q_BLOCKS = 2


def _moba_query_columns(q, kmean, far_row, first_block, blk, nb):
    nbp, tq = kmean.shape[0], q.shape[0]
    km_hi = kmean.astype(jnp.bfloat16)
    km_lo = (kmean - km_hi.astype(jnp.float32)).astype(jnp.bfloat16)
    gate = (lax.dot_general(km_hi, q, _NT_DIMS, preferred_element_type=jnp.float32)
            + lax.dot_general(km_lo, q, _NT_DIMS, preferred_element_type=jnp.float32))
    n_idx = lax.broadcasted_iota(jnp.int32, (nbp, tq), 0)
    own = first_block + lax.broadcasted_iota(jnp.int32, (nbp, tq), 1) // blk
    past = n_idx < own
    gate = jnp.where(past, gate, NEG_BIG)
    beaten = jnp.zeros((nbp, tq), jnp.int32)
    for n in range(nb):
        g_n = gate[n:n + 1, :]
        lower_index_wins_ties = jnp.where(n_idx > n, 1, 0)
        beaten = beaten + jnp.where(g_n > gate, 1, 0) + jnp.where(g_n == gate, lower_index_wins_ties, 0)
    selected = past & (beaten < MOBA_TOPK)
    far_bias = jnp.broadcast_to(far_row, (nbp, tq))
    term = jnp.where(selected, jnp.where(n_idx == own - 1, 0.0, far_bias), NEG_BIG)
    term = jnp.where(n_idx == own, 0.0, term)
    term_hi = term.astype(jnp.bfloat16).astype(jnp.float32)
    term_lo = jnp.where(selected, term - term_hi, 0.0)
    pieces = [term_hi[:nb], term_lo[:nb]]
    if 2 * nb < LANES:
        pieces.append(jnp.zeros((LANES - 2 * nb, tq), jnp.float32))
    return jnp.concatenate(pieces, axis=0).T.astype(q.dtype)


def _moba_kernel(q_ref, k_ref, v_ref, rbias_ref, far_ref, z_ref, o_ref,
                 kaug_ref, vext_ref, kmean_ref, t5_ref, sa_ref, sb_ref, m_ref, acc_ref, *, nb, blk):
    qi = pl.program_id(2)
    tq = q_ref.shape[1]
    dh = HEAD_DIM
    G = kaug_ref.shape[0]
    first_block = qi * (tq // blk)

    def rows(j):
        return pl.ds(pl.multiple_of(j * tq, tq), tq)

    @pl.when(qi == 0)
    def _():
        for g in range(G):
            kmean_ref[g] = jnp.zeros(kmean_ref.shape[1:], kmean_ref.dtype)
            rb = jnp.broadcast_to(rbias_ref[g], (tq, tq))
            t5_ref[g] = pltpu.roll(rb, 1, 1, stride=1, stride_axis=0)

    key_block = first_block + lax.broadcasted_iota(jnp.int32, (tq, LANES), 0) // blk
    lane = lax.broadcasted_iota(jnp.int32, (1, LANES), 1)
    lane_block = jnp.where(lane < nb, lane, jnp.where(lane < 2 * nb, lane - nb, -1))
    hot = jnp.where(jnp.broadcast_to(lane_block, (tq, LANES)) == key_block, 1.0, 0.0).astype(kaug_ref.dtype)
    for g in range(G):
        k = k_ref[0, :, g * dh:(g + 1) * dh]
        kaug_ref[g, rows(qi), :dh] = k
        kaug_ref[g, rows(qi), dh:] = hot
        vext_ref[g, rows(qi), :dh] = v_ref[0, :, g * dh:(g + 1) * dh]
        vext_ref[g, rows(qi), dh:] = jnp.ones((tq, dh), vext_ref.dtype)
        for n in range(tq // blk):
            kb = k[n * blk:(n + 1) * blk, :].astype(jnp.float32)
            kmean_ref[g, pl.ds(first_block + n, 1), :] = jnp.sum(kb, axis=0, keepdims=True) * (1.0 / blk)

    def causal_with_bias(s, g):
        row = lax.broadcasted_iota(jnp.int32, (tq, tq), 0)
        col = lax.broadcasted_iota(jnp.int32, (tq, tq), 1)
        return jnp.where(col <= row, s + t5_ref[g], NEG_BIG)

    def previous_with_bias(s, g):
        top = jnp.concatenate([s[:blk, :tq - blk], s[:blk, tq - blk:] + t5_ref[g, tq - blk:, :blk]], axis=1)
        return jnp.concatenate([top, s[blk:, :]], axis=0)

    tiles = []
    for g in range(G):
        q = q_ref[0, :, g * dh:(g + 1) * dh]
        q_extra = _moba_query_columns(q, kmean_ref[g], far_ref[g], first_block, blk, nb)
        q_aug = jnp.concatenate([q, q_extra], axis=1)
        _init_softmax_state(m_ref.at[g], acc_ref.at[g])

        def logits(j, g=g, q_aug=q_aug):
            return lax.dot_general(q_aug, kaug_ref[g, rows(j), :], _NT_DIMS, preferred_element_type=jnp.float32)

        def values(j, g=g):
            return vext_ref[g, rows(j), :]

        def diagonal_logits(g=g, q_aug=q_aug):
            return _causal_tile_logits(q_aug, kaug_ref[g, rows(qi), :])

        diagonal = _Tile(diagonal_logits, functools.partial(causal_with_bias, g=g), functools.partial(values, qi),
                         causal_halves=True)
        previous = _Tile(functools.partial(logits, qi - 1), functools.partial(previous_with_bias, g=g),
                         functools.partial(values, qi - 1))
        tiles.append((diagonal, previous, logits, values, (sa_ref.at[g], sb_ref.at[g]), m_ref.at[g], acc_ref.at[g]))

    @pl.when(qi == 0)
    def _():
        _pipelined_tiles([_HeadTiles([t[0]], *t[2:]) for t in tiles], 0)

    @pl.when(qi >= 1)
    def _():
        _pipelined_tiles([_HeadTiles([t[0], t[1]], *t[2:]) for t in tiles], qi - 1)

    for g in range(G):
        o_ref[0, :, g * dh:(g + 1) * dh] = _finish_attention(acc_ref.at[g], z_ref[0, :, g * dh:(g + 1) * dh],
                                                             o_ref.dtype)


def _moba_attention(qkv, rel_bias_table, sz, H, col0, zcol0):
    B, S, _ = qkv.shape
    blk = MOBA_BLOCK
    tq = MOBA_Q_BLOCKS * blk
    assert MOBA_Q_BLOCKS == 2 and S % tq == 0
    nb = S // blk
    nbp = -(-nb // 8) * 8
    assert 2 * nb <= LANES
    Dh = HEAD_DIM
    G = ATTN_HEADS_PER_STEP
    assert H % G == 0 and col0 % G == 0 and zcol0 % G == 0
    W = G * Dh
    table_t = rel_bias_table.T.astype(jnp.float32) * LOG2E
    rbias = table_t[:, _t5_bucket_np(np.arange(tq)[::-1])][:, None, :]
    far_bucket = int(_t5_bucket_np(np.array([blk + 1]))[0])
    assert (_t5_bucket_np(np.arange(blk + 1, S + blk)) == far_bucket).all()
    far = jnp.broadcast_to(table_t[:, far_bucket][:, None, None], (H, 1, tq))
    return pl.pallas_call(
        functools.partial(_moba_kernel, nb=nb, blk=blk),
        grid=(B, H // G, S // tq),
        in_specs=[pl.BlockSpec((1, tq, W), lambda b, h, i: (b, i, col0 // G + h)),
                  pl.BlockSpec((1, tq, W), lambda b, h, i: (b, i, (col0 + H) // G + h)),
                  pl.BlockSpec((1, tq, W), lambda b, h, i: (b, i, (col0 + 2 * H) // G + h)),
                  pl.BlockSpec((G, 1, tq), lambda b, h, i: (h, 0, 0)),
                  pl.BlockSpec((G, 1, tq), lambda b, h, i: (h, 0, 0)),
                  pl.BlockSpec((1, tq, W), lambda b, h, i: (b, i, zcol0 // G + h))],
        out_specs=pl.BlockSpec((1, tq, W), lambda b, h, i: (b, i, h)),
        out_shape=jax.ShapeDtypeStruct((B, S, H * Dh), jnp.bfloat16),
        scratch_shapes=[pltpu.VMEM((G, S, Dh + LANES), jnp.bfloat16), pltpu.VMEM((G, S, 2 * Dh), jnp.bfloat16),
                        pltpu.VMEM((G, nbp, Dh), jnp.float32), pltpu.VMEM((G, tq, tq), jnp.float32),
                        pltpu.VMEM((G, tq, tq), jnp.float32), pltpu.VMEM((G, tq, tq), jnp.float32),
                        pltpu.VMEM((G, tq, LANES), jnp.float32), pltpu.VMEM((G, tq, 2 * Dh), jnp.float32)],
        compiler_params=_params("arbitrary", "arbitrary", "arbitrary"),
        name="moba_attention",
    )(qkv, qkv, qkv, rbias, far, sz)


def _branch_merge_kernel(ya_ref, yf_ref, wa_ref, wf_ref, ga_ref, gf_ref, o_ref):
    ua = jnp.dot(ya_ref[...], wa_ref[0], preferred_element_type=jnp.float32)
    uf = jnp.dot(yf_ref[...], wf_ref[0], preferred_element_type=jnp.float32)
    o_ref[...] = (ga_ref[...] * ua + gf_ref[...] * uf).astype(o_ref.dtype)


def _branch_merge(ya, yf, w, gates, tm=1024, tn=512):
    M, W = ya.shape
    D = w.shape[2]
    tm, tn = min(tm, M), min(tn, D)
    nj = D // tn
    return pl.pallas_call(
        _branch_merge_kernel,
        grid=(M // tm, nj),
        in_specs=[pl.BlockSpec((tm, W), lambda i, j: (i, 0)),
                  pl.BlockSpec((tm, W), lambda i, j: (i, 0)),
                  pl.BlockSpec((1, W, tn), lambda i, j: (0, 0, j)),
                  pl.BlockSpec((1, W, tn), lambda i, j: (1, 0, j)),
                  pl.BlockSpec((tm, tn), lambda i, j: (i, j)),
                  pl.BlockSpec((tm, tn), lambda i, j: (i, nj + j))],
        out_specs=pl.BlockSpec((tm, tn), lambda i, j: (i, j)),
        out_shape=jax.ShapeDtypeStruct((M, D), jnp.bfloat16),
        compiler_params=_params("parallel", "parallel"),
        name="branch_merge",
    )(ya, yf, w, w, gates, gates)


LN_ROW_CHUNK = 32


def _out_layernorm_kernel(h_ref, w_ref, x_ref, g_ref, b_ref, o_ref, pre_even_ref, pre_odd_ref):
    i, j = pl.program_id(0), pl.program_id(1)
    tn = w_ref.shape[1]
    chunk = o_ref.shape[0]

    @pl.when((i == 0) & (j == 0))
    def _():
        pre_odd_ref[...] = jnp.zeros_like(pre_odd_ref)

    def step(cur_ref, prev_ref):
        rows = min(LN_ROW_CHUNK, chunk)
        for c in range(chunk // rows):
            r = prev_ref[pl.ds(pl.multiple_of(j * chunk + c * rows, rows), rows), :]
            mu = jnp.mean(r, axis=-1, keepdims=True)
            d = r - mu
            var = jnp.mean(d * d, axis=-1, keepdims=True)
            o_ref[c * rows:(c + 1) * rows, :] = d * lax.rsqrt(var + LN_EPS) * g_ref[...] + b_ref[...]
        out = jnp.dot(h_ref[...], w_ref[...], preferred_element_type=jnp.float32)
        cur_ref[:, pl.ds(pl.multiple_of(j * tn, tn), tn)] = DEEPNORM_ALPHA * x_ref[...] + out

    @pl.when(i % 2 == 0)
    def _():
        step(pre_even_ref, pre_odd_ref)

    @pl.when(i % 2 == 1)
    def _():
        step(pre_odd_ref, pre_even_ref)


def _out_layernorm(h, w, x, gain, bias, tm=512, tn=1024):
    M, D = x.shape
    tm, tn = min(tm, M), min(tn, D)
    n_i, n_j = M // tm, D // tn
    chunk = tm // n_j
    assert chunk % 8 == 0
    return pl.pallas_call(
        _out_layernorm_kernel,
        grid=(n_i + 1, n_j),
        in_specs=[pl.BlockSpec((tm, D), lambda i, j: (jnp.minimum(i, n_i - 1), 0)),
                  pl.BlockSpec((D, tn), lambda i, j: (0, j)),
                  pl.BlockSpec((tm, tn), lambda i, j: (jnp.minimum(i, n_i - 1), j)),
                  pl.BlockSpec((1, D), lambda i, j: (0, 0)),
                  pl.BlockSpec((1, D), lambda i, j: (0, 0))],
        out_specs=pl.BlockSpec((chunk, D), lambda i, j: (jnp.where(i == 0, 0, (i - 1) * n_j + j), 0)),
        out_shape=jax.ShapeDtypeStruct((M, D), jnp.float32),
        scratch_shapes=[pltpu.VMEM((tm, D), jnp.float32), pltpu.VMEM((tm, D), jnp.float32)],
        compiler_params=_params("arbitrary", "arbitrary"),
        name="out_layernorm",
    )(h, w, x, gain, bias)


def kernel(x, w_in, b_forget, b_gate, rel_bias_table, w_branch, w_out, ln_gain, ln_bias):
    B, S, D = x.shape
    Ha, Hf = MOBA_HEADS, FOX_HEADS
    Wa, Wf = Ha * HEAD_DIM, Hf * HEAD_DIM
    bf16, f32 = jnp.bfloat16, jnp.float32
    depth = w_in.shape[0]
    for layer in range(depth):
        w_t = jnp.swapaxes(w_in[layer], 0, 1).astype(bf16)
        n_attn = 4 * Wa + 4 * Wf
        w_f = jnp.pad(w_t[n_attn:n_attn + Hf], ((0, LANES - Hf), (0, 0)))
        b_f = jnp.pad(b_forget[layer].astype(f32), (0, LANES - Hf))[None, :]
        x2 = x.reshape(B * S, D).astype(f32)
        xb, cx = _forget_bias_columns(x2, w_f, b_f, Hf, B)

        scale = HEAD_DIM ** -0.5 * LOG2E
        qkv_scale = jnp.concatenate([jnp.full((Wa,), scale, f32), jnp.ones((2 * Wa,), f32),
                                     jnp.full((Wf,), scale, f32), jnp.ones((2 * Wf,), f32)])[None, :]
        tn = PROJ_TN
        assert Wa % tn == 0 and Wf % tn == 0
        a_blocks, f_blocks = Wa // tn, Wf // tn
        qkv = _matmul(xb, w_t, qkv_scale, jnp.zeros_like(qkv_scale), bf16, "none",
                      lambda j: jnp.where(j < 3 * a_blocks, j, j + a_blocks), tn=tn).reshape(B, S, -1)
        ones_z = jnp.ones((1, Wa + Wf), f32)
        sz = _matmul(xb, w_t, ones_z, jnp.zeros_like(ones_z), f32, "silu",
                     lambda j: jnp.where(j < a_blocks, j + 3 * a_blocks, j + 3 * a_blocks + 3 * f_blocks),
                     tn=tn).reshape(B, S, -1)
        gates = _matmul(xb, w_t, jnp.ones((1, 2 * D), f32), b_gate[layer].reshape(1, 2 * D).astype(f32),
                        f32, "sigmoid", w_row_start=n_attn + Hf, tn=tn)

        ya = _moba_attention(qkv, rel_bias_table, sz, Ha, 0, 0)
        yf = _fox_attention(qkv, cx, sz, Hf, 3 * Ha, Ha)

        assert Wa == Wf
        merged = _branch_merge(ya.reshape(B * S, Wa), yf.reshape(B * S, Wf), w_branch[layer].astype(bf16), gates)
        y = _out_layernorm(merged, w_out[layer].astype(bf16), x2,
                           ln_gain[layer].reshape(1, D).astype(f32), ln_bias[layer].reshape(1, D).astype(f32))
        x = y.reshape(B, S, D).astype(x.dtype)
    return x
```

```python
import functools
import math
from typing import Any, Callable, NamedTuple, Sequence

import jax
import jax.numpy as jnp
import numpy as np
from jax import lax
from jax.experimental import pallas as pl
from jax.experimental.pallas import tpu as pltpu

HEAD_DIM = 128
LANES = 128
MOBA_HEADS = 16
FOX_HEADS = 16
MOBA_BLOCK = 256
MOBA_TOPK = 3
T5_NUM_BUCKETS = 32
T5_MAX_DISTANCE = 128
LN_EPS = 1e-5
DEPTH = 1
DEEPNORM_ALPHA = (2.0 * DEPTH) ** 0.25
LOG2E = math.log2(math.e)

V7X_VMEM_LIMIT_BYTES = 56 * 1024 * 1024
PROJ_TN = 1024
NEG_BIG = -1e30

_NT_DIMS = (((1,), (1,)), ((), ()))


def _params(*semantics):
    return pltpu.CompilerParams(dimension_semantics=semantics,
                                vmem_limit_bytes=V7X_VMEM_LIMIT_BYTES)


def _split3_bf16(a):
    hi = a.astype(jnp.bfloat16).astype(jnp.float32)
    r = a - hi
    mid = r.astype(jnp.bfloat16).astype(jnp.float32)
    lo = (r - mid).astype(jnp.bfloat16).astype(jnp.float32)
    return hi, mid, lo


def _matmul_kernel(x_ref, wt_ref, s_ref, b_ref, o_ref, *, act):
    acc = lax.dot_general(x_ref[...], wt_ref[...], _NT_DIMS, preferred_element_type=jnp.float32)
    acc = acc * s_ref[...] + b_ref[...]
    if act == "silu":
        acc = acc * jax.nn.sigmoid(acc)
    elif act == "sigmoid":
        acc = jax.nn.sigmoid(acc)
    o_ref[...] = acc.astype(o_ref.dtype)


def _matmul(x, w_t, col_scale, col_bias, out_dtype, act, w_row_block=lambda j: j, w_row_start=None,
            tm=1024, tn=1024):
    M, K = x.shape
    N = col_scale.shape[1]
    tm, tn = min(tm, M), min(tn, N)
    if w_row_start is None:
        w_spec = pl.BlockSpec((tn, K), lambda i, j: (w_row_block(j), 0))
    else:
        align = math.gcd(w_row_start, tn)
        w_spec = pl.BlockSpec((pl.Element(tn), pl.Element(K)),
                              lambda i, j: (pl.multiple_of(w_row_start + j * tn, align), 0))
    return pl.pallas_call(
        functools.partial(_matmul_kernel, act=act),
        grid=(M // tm, N // tn),
        in_specs=[pl.BlockSpec((tm, K), lambda i, j: (i, 0)),
                  w_spec,
                  pl.BlockSpec((1, tn), lambda i, j: (0, j)),
                  pl.BlockSpec((1, tn), lambda i, j: (0, j))],
        out_specs=pl.BlockSpec((tm, tn), lambda i, j: (i, j)),
        out_shape=jax.ShapeDtypeStruct((M, N), out_dtype),
        compiler_params=_params("parallel", "parallel"),
        name="proj_" + act,
    )(x, w_t, col_scale, col_bias)


def _forget_bias_kernel(x_ref, w_ref, b_ref, xb_ref, o_ref, carry_ref, *, heads):
    @pl.when(pl.program_id(1) == 0)
    def _():
        carry_ref[...] = jnp.zeros_like(carry_ref)

    ts = x_ref.shape[0]
    xb = x_ref[...].astype(xb_ref.dtype)
    xb_ref[...] = xb
    z = lax.dot_general(xb, w_ref[...], _NT_DIMS, preferred_element_type=jnp.float32) + b_ref[...]
    log_f = jnp.minimum(z, 0.0) - jnp.log1p(jnp.exp(-jnp.abs(z)))
    row = lax.broadcasted_iota(jnp.int32, (ts, ts), 0)
    col = lax.broadcasted_iota(jnp.int32, (ts, ts), 1)
    lower = jnp.where(col <= row, 1.0, 0.0).astype(jnp.bfloat16)
    c = carry_ref[...]
    for part in _split3_bf16(log_f):
        c = c + jnp.dot(lower, part.astype(jnp.bfloat16), preferred_element_type=jnp.float32)
    carry_ref[...] = c[ts - 1:ts, :]
    hi, mid, lo = _split3_bf16(c * (-LOG2E))
    lane = lax.broadcasted_iota(jnp.int32, (ts, LANES), 1)
    cols = jnp.where(lane < heads, hi,
                     jnp.where(lane < 2 * heads, pltpu.roll(mid, heads, 1),
                               jnp.where(lane < 3 * heads, pltpu.roll(lo, 2 * heads, 1), 0.0)))
    o_ref[0] = cols.astype(o_ref.dtype)


def _forget_bias_columns(x, w_f, b_forget, heads, batch, ts=512):
    M, D = x.shape
    S = M // batch
    assert 3 * heads <= LANES
    ts = min(ts, S)
    steps = S // ts
    return pl.pallas_call(
        functools.partial(_forget_bias_kernel, heads=heads),
        grid=(batch, steps),
        in_specs=[pl.BlockSpec((ts, D), lambda b, i: (b * steps + i, 0)),
                  pl.BlockSpec((LANES, D), lambda b, i: (0, 0)),
                  pl.BlockSpec((1, LANES), lambda b, i: (0, 0))],
        out_specs=[pl.BlockSpec((ts, D), lambda b, i: (b * steps + i, 0)),
                   pl.BlockSpec((1, ts, LANES), lambda b, i: (b, i, 0))],
        out_shape=[jax.ShapeDtypeStruct((M, D), jnp.bfloat16),
                   jax.ShapeDtypeStruct((batch, S, LANES), jnp.bfloat16)],
        scratch_shapes=[pltpu.VMEM((1, LANES), jnp.float32)],
        compiler_params=_params("parallel", "arbitrary"),
        name="forget_bias_columns",
    )(x, w_f, b_forget)


ATTN_HEADS_PER_STEP = 4


def _softmax_tile(s, v_ext, m_ref, acc_ref, causal_halves=False):
    tk = s.shape[1]
    chunks = [s[:, c:c + LANES] for c in range(0, tk, LANES)]
    mx = functools.reduce(jnp.maximum, chunks)
    m_prev = m_ref[...]
    m_new = jnp.maximum(m_prev, jnp.max(mx, axis=-1, keepdims=True))
    alpha = jnp.exp2(m_prev - m_new)
    p = jnp.concatenate([jnp.exp2(ch - m_new) for ch in chunks], axis=1).astype(v_ext.dtype)
    if causal_halves:
        half = s.shape[0] // 2
        pv = jnp.concatenate([jnp.dot(p[:half, :half], v_ext[:half], preferred_element_type=jnp.float32),
                              jnp.dot(p[half:], v_ext, preferred_element_type=jnp.float32)], axis=0)
    else:
        pv = jnp.dot(p, v_ext, preferred_element_type=jnp.float32)
    acc_ref[...] = acc_ref[...] * jnp.concatenate([alpha, alpha], axis=1) + pv
    m_ref[...] = m_new


def _causal_tile_logits(q_aug, k_t):
    half = q_aug.shape[0] // 2
    top = jnp.dot(q_aug[:half], k_t[:, :half], preferred_element_type=jnp.float32)
    bottom = jnp.dot(q_aug[half:], k_t, preferred_element_type=jnp.float32)
    masked = jnp.full((half, k_t.shape[1] - half), NEG_BIG, jnp.float32)
    return jnp.concatenate([jnp.concatenate([top, masked], axis=1), bottom], axis=0)


class _Tile(NamedTuple):
    qk: Callable[[], Any]
    post: Callable[[Any], Any]
    values: Callable[[], Any]
    causal_halves: bool = False


class _HeadTiles(NamedTuple):
    specials: Sequence[_Tile]
    qk: Callable[[Any], Any]
    values: Callable[[Any], Any]
    s_refs: Sequence[Any]
    m_ref: Any
    acc_ref: Any


def _pipelined_tiles(heads, n_rest):
    def fill(buf, logits_of):
        for hd in heads:
            s = logits_of(hd)
            hd.s_refs[buf][:, :s.shape[1]] = s

    def consume(buf, values_of, post=lambda hd, s: s, causal_halves=False):
        for hd in heads:
            v_ext = values_of(hd)
            s = hd.s_refs[buf][:, :v_ext.shape[0]]
            _softmax_tile(post(hd, s), v_ext, hd.m_ref, hd.acc_ref, causal_halves)

    n_special = len(heads[0].specials)
    no_uniform_tiles = isinstance(n_rest, int) and n_rest == 0
    fill(0, lambda hd: hd.specials[0].qk())
    for t in range(n_special):
        if t + 1 < n_special:
            fill((t + 1) % 2, lambda hd: hd.specials[t + 1].qk())
        elif not no_uniform_tiles:
            fill((t + 1) % 2, lambda hd: hd.qk(0))
        consume(t % 2, lambda hd: hd.specials[t].values(), lambda hd, s: hd.specials[t].post(s),
                heads[0].specials[t].causal_halves)
    if no_uniform_tiles:
        return
    cur, nxt = n_special % 2, (n_special + 1) % 2

    def two_tiles(i, carry):
        fill(nxt, lambda hd: hd.qk(2 * i + 1))
        consume(cur, lambda hd: hd.values(2 * i))
        fill(cur, lambda hd: hd.qk(jnp.minimum(2 * i + 2, n_rest - 1)))
        consume(nxt, lambda hd: hd.values(2 * i + 1))
        return carry

    lax.fori_loop(0, n_rest // 2, two_tiles, 0)

    @pl.when(n_rest % 2 == 1)
    def _():
        consume(cur, lambda hd: hd.values(n_rest - 1))


def _init_softmax_state(m_ref, acc_ref):
    m_ref[...] = jnp.full_like(m_ref, NEG_BIG)
    acc_ref[...] = jnp.zeros_like(acc_ref)


def _finish_attention(acc_ref, z, dtype):
    dh = z.shape[1]
    return (acc_ref[:, :dh] / acc_ref[:, dh:] * z).astype(dtype)


def _fox_kernel(q_ref, k_ref, cx_ref, v_ref, z_ref, o_ref, kaug_ref, vext_ref, sa_ref, sb_ref, m_ref, acc_ref,
                *, heads):
    qi = pl.program_id(2)
    tq = q_ref.shape[1]
    dh = HEAD_DIM
    G = kaug_ref.shape[0]

    def rows(j):
        return pl.ds(pl.multiple_of(j * tq, tq), tq)

    for g in range(G):
        k_aug = jnp.concatenate([k_ref[0, :, g * dh:(g + 1) * dh], cx_ref[0]], axis=1)
        kaug_ref[g, :, rows(qi)] = k_aug.T
        vext_ref[g, rows(qi), :dh] = v_ref[0, :, g * dh:(g + 1) * dh]
        vext_ref[g, rows(qi), dh:] = jnp.ones((tq, dh), vext_ref.dtype)

    def causal(s):
        row = lax.broadcasted_iota(jnp.int32, (tq, tq), 0)
        col = lax.broadcasted_iota(jnp.int32, (tq, tq), 1)
        return jnp.where(col <= row, s, NEG_BIG)

    lane = lax.broadcasted_iota(jnp.int32, (1, LANES), 1)
    tiles = []
    for g in range(G):
        h = pl.program_id(1) * G + g
        pick = (lane == h) | (lane == heads + h) | (lane == 2 * heads + h)
        ones_at_head_terms = jnp.broadcast_to(jnp.where(pick, 1.0, 0.0), (tq, LANES)).astype(q_ref.dtype)
        q_aug = jnp.concatenate([q_ref[0, :, g * dh:(g + 1) * dh], ones_at_head_terms], axis=1)
        _init_softmax_state(m_ref.at[g], acc_ref.at[g])

        def logits(j, g=g, q_aug=q_aug):
            return jnp.dot(q_aug, kaug_ref[g, :, rows(j)], preferred_element_type=jnp.float32)

        def values(j, g=g):
            return vext_ref[g, rows(j), :]

        def diagonal_logits(g=g, q_aug=q_aug):
            return _causal_tile_logits(q_aug, kaug_ref[g, :, rows(qi)])

        diagonal = _Tile(diagonal_logits, causal, functools.partial(values, qi), causal_halves=True)
        tiles.append(_HeadTiles([diagonal], logits, values, (sa_ref.at[g], sb_ref.at[g]), m_ref.at[g], acc_ref.at[g]))

    @pl.when(qi == 0)
    def _():
        _pipelined_tiles(tiles, 0)

    @pl.when(qi >= 1)
    def _():
        _pipelined_tiles(tiles, qi)

    for g in range(G):
        o_ref[0, :, g * dh:(g + 1) * dh] = _finish_attention(acc_ref.at[g], z_ref[0, :, g * dh:(g + 1) * dh],
                                                             o_ref.dtype)


def _fox_attention(qkv, cx, sz, H, col0, zcol0, tq=512):
    B, S, _ = qkv.shape
    tq = min(tq, S)
    Dh = HEAD_DIM
    G = ATTN_HEADS_PER_STEP
    assert H % G == 0 and col0 % G == 0 and zcol0 % G == 0
    W = G * Dh
    return pl.pallas_call(
        functools.partial(_fox_kernel, heads=H),
        grid=(B, H // G, S // tq),
        in_specs=[pl.BlockSpec((1, tq, W), lambda b, h, i: (b, i, col0 // G + h)),
                  pl.BlockSpec((1, tq, W), lambda b, h, i: (b, i, (col0 + H) // G + h)),
                  pl.BlockSpec((1, tq, LANES), lambda b, h, i: (b, i, 0)),
                  pl.BlockSpec((1, tq, W), lambda b, h, i: (b, i, (col0 + 2 * H) // G + h)),
                  pl.BlockSpec((1, tq, W), lambda b, h, i: (b, i, zcol0 // G + h))],
        out_specs=pl.BlockSpec((1, tq, W), lambda b, h, i: (b, i, h)),
        out_shape=jax.ShapeDtypeStruct((B, S, H * Dh), jnp.bfloat16),
        scratch_shapes=[pltpu.VMEM((G, Dh + LANES, S), jnp.bfloat16), pltpu.VMEM((G, S, 2 * Dh), jnp.bfloat16),
                        pltpu.VMEM((G, tq, tq), jnp.float32), pltpu.VMEM((G, tq, tq), jnp.float32),
                        pltpu.VMEM((G, tq, LANES), jnp.float32), pltpu.VMEM((G, tq, 2 * Dh), jnp.float32)],
        compiler_params=_params("arbitrary", "arbitrary", "arbitrary"),
        name="fox_attention",
    )(qkv, qkv, cx, qkv, sz)


def _t5_bucket_np(dist):
    max_exact = T5_NUM_BUCKETS // 2
    d = np.maximum(dist, 1).astype(np.float32)
    ratio = (np.log(d / np.float32(max_exact)) / np.float32(math.log(T5_MAX_DISTANCE / max_exact))
             * np.float32(T5_NUM_BUCKETS - max_exact))
    large = np.minimum(max_exact + ratio.astype(np.int32), T5_NUM_BUCKETS - 1)
    return np.where(dist < max_exact, dist, large).astype(np.int32)


MOBA_Q_BLOCKS = 2


def _moba_query_columns(q, kmean, far_row, first_block, blk, nb):
    nbp, tq = kmean.shape[0], q.shape[0]
    km_hi = kmean.astype(jnp.bfloat16)
    km_lo = (kmean - km_hi.astype(jnp.float32)).astype(jnp.bfloat16)
    gate = (lax.dot_general(km_hi, q, _NT_DIMS, preferred_element_type=jnp.float32)
            + lax.dot_general(km_lo, q, _NT_DIMS, preferred_element_type=jnp.float32))
    n_idx = lax.broadcasted_iota(jnp.int32, (nbp, tq), 0)
    own = first_block + lax.broadcasted_iota(jnp.int32, (nbp, tq), 1) // blk
    past = n_idx < own
    gate = jnp.where(past, gate, NEG_BIG)
    beaten = jnp.zeros((nbp, tq), jnp.int32)
    for n in range(nb):
        g_n = gate[n:n + 1, :]
        lower_index_wins_ties = jnp.where(n_idx > n, 1, 0)
        beaten = beaten + jnp.where(g_n > gate, 1, 0) + jnp.where(g_n == gate, lower_index_wins_ties, 0)
    selected = past & (beaten < MOBA_TOPK)
    far_bias = jnp.broadcast_to(far_row, (nbp, tq))
    term = jnp.where(selected, jnp.where(n_idx == own - 1, 0.0, far_bias), NEG_BIG)
    term = jnp.where(n_idx == own, 0.0, term)
    term_hi = term.astype(jnp.bfloat16).astype(jnp.float32)
    term_lo = jnp.where(selected, term - term_hi, 0.0)
    pieces = [term_hi[:nb], term_lo[:nb]]
    if 2 * nb < LANES:
        pieces.append(jnp.zeros((LANES - 2 * nb, tq), jnp.float32))
    return jnp.concatenate(pieces, axis=0).T.astype(q.dtype)


def _moba_kernel(q_ref, k_ref, v_ref, rbias_ref, far_ref, z_ref, o_ref,
                 kaug_ref, vext_ref, kmean_ref, t5_ref, sa_ref, sb_ref, m_ref, acc_ref, *, nb, blk):
    qi = pl.program_id(2)
    tq = q_ref.shape[1]
    dh = HEAD_DIM
    G = kaug_ref.shape[0]
    first_block = qi * (tq // blk)

    def rows(j):
        return pl.ds(pl.multiple_of(j * tq, tq), tq)

    @pl.when(qi == 0)
    def _():
        for g in range(G):
            kmean_ref[g] = jnp.zeros(kmean_ref.shape[1:], kmean_ref.dtype)
            rb = jnp.broadcast_to(rbias_ref[g], (tq, tq))
            t5_ref[g] = pltpu.roll(rb, 1, 1, stride=1, stride_axis=0)

    key_block = first_block + lax.broadcasted_iota(jnp.int32, (tq, LANES), 0) // blk
    lane = lax.broadcasted_iota(jnp.int32, (1, LANES), 1)
    lane_block = jnp.where(lane < nb, lane, jnp.where(lane < 2 * nb, lane - nb, -1))
    hot = jnp.where(jnp.broadcast_to(lane_block, (tq, LANES)) == key_block, 1.0, 0.0).astype(kaug_ref.dtype)
    for g in range(G):
        k = k_ref[0, :, g * dh:(g + 1) * dh]
        kaug_ref[g, :, rows(qi)] = jnp.concatenate([k, hot], axis=1).T
        vext_ref[g, rows(qi), :dh] = v_ref[0, :, g * dh:(g + 1) * dh]
        vext_ref[g, rows(qi), dh:] = jnp.ones((tq, dh), vext_ref.dtype)
        for n in range(tq // blk):
            kb = k[n * blk:(n + 1) * blk, :].astype(jnp.float32)
            kmean_ref[g, pl.ds(first_block + n, 1), :] = jnp.sum(kb, axis=0, keepdims=True) * (1.0 / blk)

    def causal_with_bias(s, g):
        row = lax.broadcasted_iota(jnp.int32, (tq, tq), 0)
        col = lax.broadcasted_iota(jnp.int32, (tq, tq), 1)
        return jnp.where(col <= row, s + t5_ref[g], NEG_BIG)

    def previous_with_bias(s, g):
        top = jnp.concatenate([s[:blk, :tq - blk], s[:blk, tq - blk:] + t5_ref[g, tq - blk:, :blk]], axis=1)
        return jnp.concatenate([top, s[blk:, :]], axis=0)

    tiles = []
    for g in range(G):
        q = q_ref[0, :, g * dh:(g + 1) * dh]
        q_extra = _moba_query_columns(q, kmean_ref[g], far_ref[g], first_block, blk, nb)
        q_aug = jnp.concatenate([q, q_extra], axis=1)
        _init_softmax_state(m_ref.at[g], acc_ref.at[g])

        def logits(j, g=g, q_aug=q_aug):
            return jnp.dot(q_aug, kaug_ref[g, :, rows(j)], preferred_element_type=jnp.float32)

        def values(j, g=g):
            return vext_ref[g, rows(j), :]

        def diagonal_logits(g=g, q_aug=q_aug):
            return _causal_tile_logits(q_aug, kaug_ref[g, :, rows(qi)])

        diagonal = _Tile(diagonal_logits, functools.partial(causal_with_bias, g=g), functools.partial(values, qi),
                         causal_halves=True)
        previous = _Tile(functools.partial(logits, qi - 1), functools.partial(previous_with_bias, g=g),
                         functools.partial(values, qi - 1))
        tiles.append((diagonal, previous, logits, values, (sa_ref.at[g], sb_ref.at[g]), m_ref.at[g], acc_ref.at[g]))

    @pl.when(qi == 0)
    def _():
        _pipelined_tiles([_HeadTiles([t[0]], *t[2:]) for t in tiles], 0)

    @pl.when(qi >= 1)
    def _():
        _pipelined_tiles([_HeadTiles([t[0], t[1]], *t[2:]) for t in tiles], qi - 1)

    for g in range(G):
        o_ref[0, :, g * dh:(g + 1) * dh] = _finish_attention(acc_ref.at[g], z_ref[0, :, g * dh:(g + 1) * dh],
                                                             o_ref.dtype)


def _moba_attention(qkv, rel_bias_table, sz, H, col0, zcol0):
    B, S, _ = qkv.shape
    blk = MOBA_BLOCK
    tq = MOBA_Q_BLOCKS * blk
    assert MOBA_Q_BLOCKS == 2 and S % tq == 0
    nb = S // blk
    nbp = -(-nb // 8) * 8
    assert 2 * nb <= LANES
    Dh = HEAD_DIM
    G = ATTN_HEADS_PER_STEP
    assert H % G == 0 and col0 % G == 0 and zcol0 % G == 0
    W = G * Dh
    table_t = rel_bias_table.T.astype(jnp.float32) * LOG2E
    rbias = table_t[:, _t5_bucket_np(np.arange(tq)[::-1])][:, None, :]
    far_bucket = int(_t5_bucket_np(np.array([blk + 1]))[0])
    assert (_t5_bucket_np(np.arange(blk + 1, S + blk)) == far_bucket).all()
    far = jnp.broadcast_to(table_t[:, far_bucket][:, None, None], (H, 1, tq))
    return pl.pallas_call(
        functools.partial(_moba_kernel, nb=nb, blk=blk),
        grid=(B, H // G, S // tq),
        in_specs=[pl.BlockSpec((1, tq, W), lambda b, h, i: (b, i, col0 // G + h)),
                  pl.BlockSpec((1, tq, W), lambda b, h, i: (b, i, (col0 + H) // G + h)),
                  pl.BlockSpec((1, tq, W), lambda b, h, i: (b, i, (col0 + 2 * H) // G + h)),
                  pl.BlockSpec((G, 1, tq), lambda b, h, i: (h, 0, 0)),
                  pl.BlockSpec((G, 1, tq), lambda b, h, i: (h, 0, 0)),
                  pl.BlockSpec((1, tq, W), lambda b, h, i: (b, i, zcol0 // G + h))],
        out_specs=pl.BlockSpec((1, tq, W), lambda b, h, i: (b, i, h)),
        out_shape=jax.ShapeDtypeStruct((B, S, H * Dh), jnp.bfloat16),
        scratch_shapes=[pltpu.VMEM((G, Dh + LANES, S), jnp.bfloat16), pltpu.VMEM((G, S, 2 * Dh), jnp.bfloat16),
                        pltpu.VMEM((G, nbp, Dh), jnp.float32), pltpu.VMEM((G, tq, tq), jnp.float32),
                        pltpu.VMEM((G, tq, tq), jnp.float32), pltpu.VMEM((G, tq, tq), jnp.float32),
                        pltpu.VMEM((G, tq, LANES), jnp.float32), pltpu.VMEM((G, tq, 2 * Dh), jnp.float32)],
        compiler_params=_params("arbitrary", "arbitrary", "arbitrary"),
        name="moba_attention",
    )(qkv, qkv, qkv, rbias, far, sz)


def _branch_merge_kernel(ya_ref, yf_ref, wa_ref, wf_ref, ga_ref, gf_ref, o_ref):
    ua = jnp.dot(ya_ref[...], wa_ref[0], preferred_element_type=jnp.float32)
    uf = jnp.dot(yf_ref[...], wf_ref[0], preferred_element_type=jnp.float32)
    o_ref[...] = (ga_ref[...] * ua + gf_ref[...] * uf).astype(o_ref.dtype)


def _branch_merge(ya, yf, w, gates, tm=1024, tn=512):
    M, W = ya.shape
    D = w.shape[2]
    tm, tn = min(tm, M), min(tn, D)
    nj = D // tn
    return pl.pallas_call(
        _branch_merge_kernel,
        grid=(M // tm, nj),
        in_specs=[pl.BlockSpec((tm, W), lambda i, j: (i, 0)),
                  pl.BlockSpec((tm, W), lambda i, j: (i, 0)),
                  pl.BlockSpec((1, W, tn), lambda i, j: (0, 0, j)),
                  pl.BlockSpec((1, W, tn), lambda i, j: (1, 0, j)),
                  pl.BlockSpec((tm, tn), lambda i, j: (i, j)),
                  pl.BlockSpec((tm, tn), lambda i, j: (i, nj + j))],
        out_specs=pl.BlockSpec((tm, tn), lambda i, j: (i, j)),
        out_shape=jax.ShapeDtypeStruct((M, D), jnp.bfloat16),
        compiler_params=_params("parallel", "parallel"),
        name="branch_merge",
    )(ya, yf, w, w, gates, gates)


LN_ROW_CHUNK = 32


def _out_layernorm_kernel(h_ref, w_ref, x_ref, g_ref, b_ref, o_ref, pre_even_ref, pre_odd_ref):
    i, j = pl.program_id(0), pl.program_id(1)
    tn = w_ref.shape[1]
    chunk = o_ref.shape[0]

    @pl.when((i == 0) & (j == 0))
    def _():
        pre_odd_ref[...] = jnp.zeros_like(pre_odd_ref)

    def step(cur_ref, prev_ref):
        rows = min(LN_ROW_CHUNK, chunk)
        for c in range(chunk // rows):
            r = prev_ref[pl.ds(pl.multiple_of(j * chunk + c * rows, rows), rows), :]
            mu = jnp.mean(r, axis=-1, keepdims=True)
            d = r - mu
            var = jnp.mean(d * d, axis=-1, keepdims=True)
            o_ref[c * rows:(c + 1) * rows, :] = d * lax.rsqrt(var + LN_EPS) * g_ref[...] + b_ref[...]
        out = jnp.dot(h_ref[...], w_ref[...], preferred_element_type=jnp.float32)
        cur_ref[:, pl.ds(pl.multiple_of(j * tn, tn), tn)] = DEEPNORM_ALPHA * x_ref[...] + out

    @pl.when(i % 2 == 0)
    def _():
        step(pre_even_ref, pre_odd_ref)

    @pl.when(i % 2 == 1)
    def _():
        step(pre_odd_ref, pre_even_ref)


def _out_layernorm(h, w, x, gain, bias, tm=512, tn=1024):
    M, D = x.shape
    tm, tn = min(tm, M), min(tn, D)
    n_i, n_j = M // tm, D // tn
    chunk = tm // n_j
    assert chunk % 8 == 0
    return pl.pallas_call(
        _out_layernorm_kernel,
        grid=(n_i + 1, n_j),
        in_specs=[pl.BlockSpec((tm, D), lambda i, j: (jnp.minimum(i, n_i - 1), 0)),
                  pl.BlockSpec((D, tn), lambda i, j: (0, j)),
                  pl.BlockSpec((tm, tn), lambda i, j: (jnp.minimum(i, n_i - 1), j)),
                  pl.BlockSpec((1, D), lambda i, j: (0, 0)),
                  pl.BlockSpec((1, D), lambda i, j: (0, 0))],
        out_specs=pl.BlockSpec((chunk, D), lambda i, j: (jnp.where(i == 0, 0, (i - 1) * n_j + j), 0)),
        out_shape=jax.ShapeDtypeStruct((M, D), jnp.float32),
        scratch_shapes=[pltpu.VMEM((tm, D), jnp.float32), pltpu.VMEM((tm, D), jnp.float32)],
        compiler_params=_params("arbitrary", "arbitrary"),
        name="out_layernorm",
    )(h, w, x, gain, bias)


def kernel(x, w_in, b_forget, b_gate, rel_bias_table, w_branch, w_out, ln_gain, ln_bias):
    B, S, D = x.shape
    Ha, Hf = MOBA_HEADS, FOX_HEADS
    Wa, Wf = Ha * HEAD_DIM, Hf * HEAD_DIM
    bf16, f32 = jnp.bfloat16, jnp.float32
    depth = w_in.shape[0]
    for layer in range(depth):
        w_t = jnp.swapaxes(w_in[layer], 0, 1).astype(bf16)
        n_attn = 4 * Wa + 4 * Wf
        w_f = jnp.pad(w_t[n_attn:n_attn + Hf], ((0, LANES - Hf), (0, 0)))
        b_f = jnp.pad(b_forget[layer].astype(f32), (0, LANES - Hf))[None, :]
        x2 = x.reshape(B * S, D).astype(f32)
        xb, cx = _forget_bias_columns(x2, w_f, b_f, Hf, B)

        scale = HEAD_DIM ** -0.5 * LOG2E
        qkv_scale = jnp.concatenate([jnp.full((Wa,), scale, f32), jnp.ones((2 * Wa,), f32),
                                     jnp.full((Wf,), scale, f32), jnp.ones((2 * Wf,), f32)])[None, :]
        tn = PROJ_TN
        assert Wa % tn == 0 and Wf % tn == 0
        a_blocks, f_blocks = Wa // tn, Wf // tn
        qkv = _matmul(xb, w_t, qkv_scale, jnp.zeros_like(qkv_scale), bf16, "none",
                      lambda j: jnp.where(j < 3 * a_blocks, j, j + a_blocks), tn=tn).reshape(B, S, -1)
        ones_z = jnp.ones((1, Wa + Wf), f32)
        sz = _matmul(xb, w_t, ones_z, jnp.zeros_like(ones_z), f32, "silu",
                     lambda j: jnp.where(j < a_blocks, j + 3 * a_blocks, j + 3 * a_blocks + 3 * f_blocks),
                     tn=tn).reshape(B, S, -1)
        gates = _matmul(xb, w_t, jnp.ones((1, 2 * D), f32), b_gate[layer].reshape(1, 2 * D).astype(f32),
                        f32, "sigmoid", w_row_start=n_attn + Hf, tn=tn)

        ya = _moba_attention(qkv, rel_bias_table, sz, Ha, 0, 0)
        yf = _fox_attention(qkv, cx, sz, Hf, 3 * Ha, Ha)

        assert Wa == Wf
        merged = _branch_merge(ya.reshape(B * S, Wa), yf.reshape(B * S, Wf), w_branch[layer].astype(bf16), gates)
        y = _out_layernorm(merged, w_out[layer].astype(bf16), x2,
                           ln_gain[layer].reshape(1, D).astype(f32), ln_bias[layer].reshape(1, D).astype(f32))
        x = y.reshape(B, S, D).astype(x.dtype)
    return x
```

```python
import functools
import math
from typing import Any, Callable, NamedTuple, Sequence

import jax
import jax.numpy as jnp
import numpy as np
from jax import lax
from jax.experimental import pallas as pl
from jax.experimental.pallas import tpu as pltpu

HEAD_DIM = 128
LANES = 128
MOBA_HEADS = 16
FOX_HEADS = 16
MOBA_BLOCK = 256
MOBA_TOPK = 3
T5_NUM_BUCKETS = 32
T5_MAX_DISTANCE = 128
LN_EPS = 1e-5
DEPTH = 1
DEEPNORM_ALPHA = (2.0 * DEPTH) ** 0.25
LOG2E = math.log2(math.e)

V7X_VMEM_LIMIT_BYTES = 56 * 1024 * 1024
PROJ_TN = 1024
NEG_BIG = -1e30

_NT_DIMS = (((1,), (1,)), ((), ()))


def _params(*semantics):
    return pltpu.CompilerParams(dimension_semantics=semantics,
                                vmem_limit_bytes=V7X_VMEM_LIMIT_BYTES)


def _split3_bf16(a):
    hi = a.astype(jnp.bfloat16).astype(jnp.float32)
    r = a - hi
    mid = r.astype(jnp.bfloat16).astype(jnp.float32)
    lo = (r - mid).astype(jnp.bfloat16).astype(jnp.float32)
    return hi, mid, lo


def _matmul_kernel(x_ref, wt_ref, s_ref, b_ref, o_ref, *, act):
    acc = lax.dot_general(x_ref[...], wt_ref[...], _NT_DIMS, preferred_element_type=jnp.float32)
    acc = acc * s_ref[...] + b_ref[...]
    if act == "silu":
        acc = acc * jax.nn.sigmoid(acc)
    elif act == "sigmoid":
        acc = jax.nn.sigmoid(acc)
    o_ref[...] = acc.astype(o_ref.dtype)


def _matmul(x, w_t, col_scale, col_bias, out_dtype, act, w_row_block=lambda j: j, w_row_start=None,
            tm=1024, tn=1024):
    M, K = x.shape
    N = col_scale.shape[1]
    tm, tn = min(tm, M), min(tn, N)
    if w_row_start is None:
        w_spec = pl.BlockSpec((tn, K), lambda i, j: (w_row_block(j), 0))
    else:
        align = math.gcd(w_row_start, tn)
        w_spec = pl.BlockSpec((pl.Element(tn), pl.Element(K)),
                              lambda i, j: (pl.multiple_of(w_row_start + j * tn, align), 0))
    return pl.pallas_call(
        functools.partial(_matmul_kernel, act=act),
        grid=(M // tm, N // tn),
        in_specs=[pl.BlockSpec((tm, K), lambda i, j: (i, 0)),
                  w_spec,
                  pl.BlockSpec((1, tn), lambda i, j: (0, j)),
                  pl.BlockSpec((1, tn), lambda i, j: (0, j))],
        out_specs=pl.BlockSpec((tm, tn), lambda i, j: (i, j)),
        out_shape=jax.ShapeDtypeStruct((M, N), out_dtype),
        compiler_params=_params("parallel", "parallel"),
        name="proj_" + act,
    )(x, w_t, col_scale, col_bias)


def _forget_bias_kernel(x_ref, w_ref, b_ref, xb_ref, o_ref, carry_ref, *, heads):
    @pl.when(pl.program_id(1) == 0)
    def _():
        carry_ref[...] = jnp.zeros_like(carry_ref)

    ts = x_ref.shape[0]
    xb = x_ref[...].astype(xb_ref.dtype)
    xb_ref[...] = xb
    z = lax.dot_general(xb, w_ref[...], _NT_DIMS, preferred_element_type=jnp.float32) + b_ref[...]
    log_f = jnp.minimum(z, 0.0) - jnp.log1p(jnp.exp(-jnp.abs(z)))
    row = lax.broadcasted_iota(jnp.int32, (ts, ts), 0)
    col = lax.broadcasted_iota(jnp.int32, (ts, ts), 1)
    lower = jnp.where(col <= row, 1.0, 0.0).astype(jnp.bfloat16)
    c = carry_ref[...]
    for part in _split3_bf16(log_f):
        c = c + jnp.dot(lower, part.astype(jnp.bfloat16), preferred_element_type=jnp.float32)
    carry_ref[...] = c[ts - 1:ts, :]
    hi, mid, lo = _split3_bf16(c * (-LOG2E))
    lane = lax.broadcasted_iota(jnp.int32, (ts, LANES), 1)
    cols = jnp.where(lane < heads, hi,
                     jnp.where(lane < 2 * heads, pltpu.roll(mid, heads, 1),
                               jnp.where(lane < 3 * heads, pltpu.roll(lo, 2 * heads, 1), 0.0)))
    o_ref[0] = cols.astype(o_ref.dtype)


def _forget_bias_columns(x, w_f, b_forget, heads, batch, ts=512):
    M, D = x.shape
    S = M // batch
    assert 3 * heads <= LANES
    ts = min(ts, S)
    steps = S // ts
    return pl.pallas_call(
        functools.partial(_forget_bias_kernel, heads=heads),
        grid=(batch, steps),
        in_specs=[pl.BlockSpec((ts, D), lambda b, i: (b * steps + i, 0)),
                  pl.BlockSpec((LANES, D), lambda b, i: (0, 0)),
                  pl.BlockSpec((1, LANES), lambda b, i: (0, 0))],
        out_specs=[pl.BlockSpec((ts, D), lambda b, i: (b * steps + i, 0)),
                   pl.BlockSpec((1, ts, LANES), lambda b, i: (b, i, 0))],
        out_shape=[jax.ShapeDtypeStruct((M, D), jnp.bfloat16),
                   jax.ShapeDtypeStruct((batch, S, LANES), jnp.bfloat16)],
        scratch_shapes=[pltpu.VMEM((1, LANES), jnp.float32)],
        compiler_params=_params("parallel", "arbitrary"),
        name="forget_bias_columns",
    )(x, w_f, b_forget)


ATTN_HEADS_PER_STEP = 4


def _softmax_tile(s, v_ext, m_ref, acc_ref):
    tk = s.shape[1]
    chunks = [s[:, c:c + LANES] for c in range(0, tk, LANES)]
    mx = functools.reduce(jnp.maximum, chunks)
    m_prev = m_ref[...]
    m_new = jnp.maximum(m_prev, jnp.max(mx, axis=-1, keepdims=True))
    alpha = jnp.exp2(m_prev - m_new)
    p = jnp.concatenate([jnp.exp2(ch - m_new) for ch in chunks], axis=1).astype(v_ext.dtype)
    pv = jnp.dot(p, v_ext, preferred_element_type=jnp.float32)
    acc_ref[...] = acc_ref[...] * jnp.concatenate([alpha, alpha], axis=1) + pv
    m_ref[...] = m_new


class _Part(NamedTuple):
    row0: int
    rows: int
    width: int
    post: Callable[[Any], Any]


class _Tile(NamedTuple):
    qk: Callable[[], Sequence[Any]]
    parts: Sequence[_Part]
    values: Callable[[], Any]


class _HeadTiles(NamedTuple):
    specials: Sequence[_Tile]
    qk: Callable[[Any], Any]
    values: Callable[[Any], Any]
    s_refs: Sequence[Any]
    m_ref: Any
    acc_ref: Any


def _pipelined_tiles(heads, n_rest):
    def fill(buf, logits_of):
        for hd in heads:
            s = logits_of(hd)
            hd.s_refs[buf][:, :s.shape[1]] = s

    def consume(buf, values_of):
        for hd in heads:
            v_ext = values_of(hd)
            _softmax_tile(hd.s_refs[buf][:, :v_ext.shape[0]], v_ext, hd.m_ref, hd.acc_ref)

    def fill_special(buf, t):
        for hd in heads:
            tile = hd.specials[t]
            for part, s in zip(tile.parts, tile.qk()):
                hd.s_refs[buf][part.row0:part.row0 + part.rows, :part.width] = s

    def consume_special(buf, t):
        for hd in heads:
            tile = hd.specials[t]
            v_ext = tile.values()
            for part in tile.parts:
                rows = pl.ds(part.row0, part.rows)
                s = hd.s_refs[buf][part.row0:part.row0 + part.rows, :part.width]
                _softmax_tile(part.post(s), v_ext[:part.width], hd.m_ref.at[rows], hd.acc_ref.at[rows])

    n_special = len(heads[0].specials)
    no_uniform_tiles = isinstance(n_rest, int) and n_rest == 0
    fill_special(0, 0)
    for t in range(n_special):
        if t + 1 < n_special:
            fill_special((t + 1) % 2, t + 1)
        elif not no_uniform_tiles:
            fill((t + 1) % 2, lambda hd: hd.qk(0))
        consume_special(t % 2, t)
    if no_uniform_tiles:
        return
    cur, nxt = n_special % 2, (n_special + 1) % 2

    def two_tiles(i, carry):
        fill(nxt, lambda hd: hd.qk(2 * i + 1))
        consume(cur, lambda hd: hd.values(2 * i))
        fill(cur, lambda hd: hd.qk(jnp.minimum(2 * i + 2, n_rest - 1)))
        consume(nxt, lambda hd: hd.values(2 * i + 1))
        return carry

    lax.fori_loop(0, n_rest // 2, two_tiles, 0)

    @pl.when(n_rest % 2 == 1)
    def _():
        consume(cur, lambda hd: hd.values(n_rest - 1))


def _diagonal_tile(q_aug, key_tile, values, bias=None):
    tq = q_aug.shape[0]
    half = tq // 2

    def masked(row0, rows, width):
        def post(s):
            row = lax.broadcasted_iota(jnp.int32, (rows, width), 0) + row0
            col = lax.broadcasted_iota(jnp.int32, (rows, width), 1)
            if bias is not None:
                s = s + bias(row0, rows, width)
            return jnp.where(col <= row, s, NEG_BIG)
        return _Part(row0, rows, width, post)

    def qk():
        k = key_tile()
        return [lax.dot_general(q_aug[:half], k[:half], _NT_DIMS, preferred_element_type=jnp.float32),
                lax.dot_general(q_aug[half:], k, _NT_DIMS, preferred_element_type=jnp.float32)]

    return _Tile(qk, [masked(0, half, half), masked(half, tq - half, tq)], values)


def _init_softmax_state(m_ref, acc_ref):
    m_ref[...] = jnp.full_like(m_ref, NEG_BIG)
    acc_ref[...] = jnp.zeros_like(acc_ref)


def _finish_attention(acc_ref, z, dtype):
    dh = z.shape[1]
    return (acc_ref[:, :dh] / acc_ref[:, dh:] * z).astype(dtype)


def _fox_kernel(q_ref, k_ref, cx_ref, v_ref, z_ref, o_ref, kaug_ref, vext_ref, sa_ref, sb_ref, m_ref, acc_ref,
                *, heads):
    qi = pl.program_id(2)
    tq = q_ref.shape[1]
    dh = HEAD_DIM
    G = kaug_ref.shape[0]

    def rows(j):
        return pl.ds(pl.multiple_of(j * tq, tq), tq)

    for g in range(G):
        kaug_ref[g, rows(qi), :dh] = k_ref[0, :, g * dh:(g + 1) * dh]
        kaug_ref[g, rows(qi), dh:] = cx_ref[0]
        vext_ref[g, rows(qi), :dh] = v_ref[0, :, g * dh:(g + 1) * dh]
        vext_ref[g, rows(qi), dh:] = jnp.ones((tq, dh), vext_ref.dtype)

    lane = lax.broadcasted_iota(jnp.int32, (1, LANES), 1)
    tiles = []
    for g in range(G):
        h = pl.program_id(1) * G + g
        pick = (lane == h) | (lane == heads + h) | (lane == 2 * heads + h)
        ones_at_head_terms = jnp.broadcast_to(jnp.where(pick, 1.0, 0.0), (tq, LANES)).astype(q_ref.dtype)
        q_aug = jnp.concatenate([q_ref[0, :, g * dh:(g + 1) * dh], ones_at_head_terms], axis=1)
        _init_softmax_state(m_ref.at[g], acc_ref.at[g])

        def logits(j, g=g, q_aug=q_aug):
            return lax.dot_general(q_aug, kaug_ref[g, rows(j), :], _NT_DIMS, preferred_element_type=jnp.float32)

        def values(j, g=g):
            return vext_ref[g, rows(j), :]

        diagonal = _diagonal_tile(q_aug, lambda g=g: kaug_ref[g, rows(qi), :], functools.partial(values, qi))
        tiles.append(_HeadTiles([diagonal], logits, values, (sa_ref.at[g], sb_ref.at[g]), m_ref.at[g], acc_ref.at[g]))

    @pl.when(qi == 0)
    def _():
        _pipelined_tiles(tiles, 0)

    @pl.when(qi >= 1)
    def _():
        _pipelined_tiles(tiles, qi)

    for g in range(G):
        o_ref[0, :, g * dh:(g + 1) * dh] = _finish_attention(acc_ref.at[g], z_ref[0, :, g * dh:(g + 1) * dh],
                                                             o_ref.dtype)


def _fox_attention(qkv, cx, sz, H, col0, zcol0, tq=512):
    B, S, _ = qkv.shape
    tq = min(tq, S)
    Dh = HEAD_DIM
    G = ATTN_HEADS_PER_STEP
    assert H % G == 0 and col0 % G == 0 and zcol0 % G == 0
    W = G * Dh
    return pl.pallas_call(
        functools.partial(_fox_kernel, heads=H),
        grid=(B, H // G, S // tq),
        in_specs=[pl.BlockSpec((1, tq, W), lambda b, h, i: (b, i, col0 // G + h)),
                  pl.BlockSpec((1, tq, W), lambda b, h, i: (b, i, (col0 + H) // G + h)),
                  pl.BlockSpec((1, tq, LANES), lambda b, h, i: (b, i, 0)),
                  pl.BlockSpec((1, tq, W), lambda b, h, i: (b, i, (col0 + 2 * H) // G + h)),
                  pl.BlockSpec((1, tq, W), lambda b, h, i: (b, i, zcol0 // G + h))],
        out_specs=pl.BlockSpec((1, tq, W), lambda b, h, i: (b, i, h)),
        out_shape=jax.ShapeDtypeStruct((B, S, H * Dh), jnp.bfloat16),
        scratch_shapes=[pltpu.VMEM((G, S, Dh + LANES), jnp.bfloat16), pltpu.VMEM((G, S, 2 * Dh), jnp.bfloat16),
                        pltpu.VMEM((G, tq, tq), jnp.float32), pltpu.VMEM((G, tq, tq), jnp.float32),
                        pltpu.VMEM((G, tq, LANES), jnp.float32), pltpu.VMEM((G, tq, 2 * Dh), jnp.float32)],
        compiler_params=_params("arbitrary", "arbitrary", "arbitrary"),
        name="fox_attention",
    )(qkv, qkv, cx, qkv, sz)


def _t5_bucket_np(dist):
    max_exact = T5_NUM_BUCKETS // 2
    d = np.maximum(dist, 1).astype(np.float32)
    ratio = (np.log(d / np.float32(max_exact)) / np.float32(math.log(T5_MAX_DISTANCE / max_exact))
             * np.float32(T5_NUM_BUCKETS - max_exact))
    large = np.minimum(max_exact + ratio.astype(np.int32), T5_NUM_BUCKETS - 1)
    return np.where(dist < max_exact, dist, large).astype(np.int32)


MOBA_Q_BLOCKS = 2


def _moba_query_columns(q, kmean, far_row, first_block, blk, nb):
    nbp, tq = kmean.shape[0], q.shape[0]
    km_hi = kmean.astype(jnp.bfloat16)
    km_lo = (kmean - km_hi.astype(jnp.float32)).astype(jnp.bfloat16)
    gate = (lax.dot_general(km_hi, q, _NT_DIMS, preferred_element_type=jnp.float32)
            + lax.dot_general(km_lo, q, _NT_DIMS, preferred_element_type=jnp.float32))
    n_idx = lax.broadcasted_iota(jnp.int32, (nbp, tq), 0)
    own = first_block + lax.broadcasted_iota(jnp.int32, (nbp, tq), 1) // blk
    past = n_idx < own
    gate = jnp.where(past, gate, NEG_BIG)
    beaten = jnp.zeros((nbp, tq), jnp.int32)
    for n in range(nb):
        g_n = gate[n:n + 1, :]
        lower_index_wins_ties = jnp.where(n_idx > n, 1, 0)
        beaten = beaten + jnp.where(g_n > gate, 1, 0) + jnp.where(g_n == gate, lower_index_wins_ties, 0)
    selected = past & (beaten < MOBA_TOPK)
    far_bias = jnp.broadcast_to(far_row, (nbp, tq))
    term = jnp.where(selected, jnp.where(n_idx == own - 1, 0.0, far_bias), NEG_BIG)
    term = jnp.where(n_idx == own, 0.0, term)
    term_hi = term.astype(jnp.bfloat16).astype(jnp.float32)
    term_lo = jnp.where(selected, term - term_hi, 0.0)
    pieces = [term_hi[:nb], term_lo[:nb]]
    if 2 * nb < LANES:
        pieces.append(jnp.zeros((LANES - 2 * nb, tq), jnp.float32))
    return jnp.concatenate(pieces, axis=0).T.astype(q.dtype)


def _moba_kernel(q_ref, k_ref, v_ref, rbias_ref, far_ref, z_ref, o_ref,
                 kaug_ref, vext_ref, kmean_ref, t5_ref, sa_ref, sb_ref, m_ref, acc_ref, *, nb, blk):
    qi = pl.program_id(2)
    tq = q_ref.shape[1]
    dh = HEAD_DIM
    G = kaug_ref.shape[0]
    first_block = qi * (tq // blk)

    def rows(j):
        return pl.ds(pl.multiple_of(j * tq, tq), tq)

    @pl.when(qi == 0)
    def _():
        for g in range(G):
            kmean_ref[g] = jnp.zeros(kmean_ref.shape[1:], kmean_ref.dtype)
            rb = jnp.broadcast_to(rbias_ref[g], (tq, tq))
            t5_ref[g] = pltpu.roll(rb, 1, 1, stride=1, stride_axis=0)

    key_block = first_block + lax.broadcasted_iota(jnp.int32, (tq, LANES), 0) // blk
    lane = lax.broadcasted_iota(jnp.int32, (1, LANES), 1)
    lane_block = jnp.where(lane < nb, lane, jnp.where(lane < 2 * nb, lane - nb, -1))
    hot = jnp.where(jnp.broadcast_to(lane_block, (tq, LANES)) == key_block, 1.0, 0.0).astype(kaug_ref.dtype)
    for g in range(G):
        k = k_ref[0, :, g * dh:(g + 1) * dh]
        kaug_ref[g, rows(qi), :dh] = k
        kaug_ref[g, rows(qi), dh:] = hot
        vext_ref[g, rows(qi), :dh] = v_ref[0, :, g * dh:(g + 1) * dh]
        vext_ref[g, rows(qi), dh:] = jnp.ones((tq, dh), vext_ref.dtype)
        for n in range(tq // blk):
            kb = k[n * blk:(n + 1) * blk, :].astype(jnp.float32)
            kmean_ref[g, pl.ds(first_block + n, 1), :] = jnp.sum(kb, axis=0, keepdims=True) * (1.0 / blk)

    def add_previous_block_bias(s, g):
        return jnp.concatenate([s[:, :tq - blk], s[:, tq - blk:] + t5_ref[g, tq - blk:, :blk]], axis=1)

    tiles = []
    for g in range(G):
        q = q_ref[0, :, g * dh:(g + 1) * dh]
        q_extra = _moba_query_columns(q, kmean_ref[g], far_ref[g], first_block, blk, nb)
        q_aug = jnp.concatenate([q, q_extra], axis=1)
        _init_softmax_state(m_ref.at[g], acc_ref.at[g])

        def logits(j, g=g, q_aug=q_aug):
            return lax.dot_general(q_aug, kaug_ref[g, rows(j), :], _NT_DIMS, preferred_element_type=jnp.float32)

        def values(j, g=g):
            return vext_ref[g, rows(j), :]

        diagonal = _diagonal_tile(q_aug, lambda g=g: kaug_ref[g, rows(qi), :], functools.partial(values, qi),
                                  bias=lambda row0, n, width, g=g: t5_ref[g, row0:row0 + n, :width])

        def previous_logits(logits=logits):
            s_prev = logits(qi - 1)
            return [s_prev[:blk], s_prev[blk:]]

        previous = _Tile(previous_logits,
                         [_Part(0, blk, tq, functools.partial(add_previous_block_bias, g=g)),
                          _Part(blk, tq - blk, tq, lambda s_rows: s_rows)],
                         functools.partial(values, qi - 1))
        tiles.append((diagonal, previous, logits, values, (sa_ref.at[g], sb_ref.at[g]), m_ref.at[g], acc_ref.at[g]))

    @pl.when(qi == 0)
    def _():
        _pipelined_tiles([_HeadTiles([t[0]], *t[2:]) for t in tiles], 0)

    @pl.when(qi >= 1)
    def _():
        _pipelined_tiles([_HeadTiles([t[0], t[1]], *t[2:]) for t in tiles], qi - 1)

    for g in range(G):
        o_ref[0, :, g * dh:(g + 1) * dh] = _finish_attention(acc_ref.at[g], z_ref[0, :, g * dh:(g + 1) * dh],
                                                             o_ref.dtype)


def _moba_attention(qkv, rel_bias_table, sz, H, col0, zcol0):
    B, S, _ = qkv.shape
    blk = MOBA_BLOCK
    tq = MOBA_Q_BLOCKS * blk
    assert MOBA_Q_BLOCKS == 2 and S % tq == 0
    nb = S // blk
    nbp = -(-nb // 8) * 8
    assert 2 * nb <= LANES
    Dh = HEAD_DIM
    G = ATTN_HEADS_PER_STEP
    assert H % G == 0 and col0 % G == 0 and zcol0 % G == 0
    W = G * Dh
    table_t = rel_bias_table.T.astype(jnp.float32) * LOG2E
    rbias = table_t[:, _t5_bucket_np(np.arange(tq)[::-1])][:, None, :]
    far_bucket = int(_t5_bucket_np(np.array([blk + 1]))[0])
    assert (_t5_bucket_np(np.arange(blk + 1, S + blk)) == far_bucket).all()
    far = jnp.broadcast_to(table_t[:, far_bucket][:, None, None], (H, 1, tq))
    return pl.pallas_call(
        functools.partial(_moba_kernel, nb=nb, blk=blk),
        grid=(B, H // G, S // tq),
        in_specs=[pl.BlockSpec((1, tq, W), lambda b, h, i: (b, i, col0 // G + h)),
                  pl.BlockSpec((1, tq, W), lambda b, h, i: (b, i, (col0 + H) // G + h)),
                  pl.BlockSpec((1, tq, W), lambda b, h, i: (b, i, (col0 + 2 * H) // G + h)),
                  pl.BlockSpec((G, 1, tq), lambda b, h, i: (h, 0, 0)),
                  pl.BlockSpec((G, 1, tq), lambda b, h, i: (h, 0, 0)),
                  pl.BlockSpec((1, tq, W), lambda b, h, i: (b, i, zcol0 // G + h))],
        out_specs=pl.BlockSpec((1, tq, W), lambda b, h, i: (b, i, h)),
        out_shape=jax.ShapeDtypeStruct((B, S, H * Dh), jnp.bfloat16),
        scratch_shapes=[pltpu.VMEM((G, S, Dh + LANES), jnp.bfloat16), pltpu.VMEM((G, S, 2 * Dh), jnp.bfloat16),
                        pltpu.VMEM((G, nbp, Dh), jnp.float32), pltpu.VMEM((G, tq, tq), jnp.float32),
                        pltpu.VMEM((G, tq, tq), jnp.float32), pltpu.VMEM((G, tq, tq), jnp.float32),
                        pltpu.VMEM((G, tq, LANES), jnp.float32), pltpu.VMEM((G, tq, 2 * Dh), jnp.float32)],
        compiler_params=_params("arbitrary", "arbitrary", "arbitrary"),
        name="moba_attention",
    )(qkv, qkv, qkv, rbias, far, sz)


def _branch_merge_kernel(ya_ref, yf_ref, wa_ref, wf_ref, ga_ref, gf_ref, o_ref):
    ua = jnp.dot(ya_ref[...], wa_ref[0], preferred_element_type=jnp.float32)
    uf = jnp.dot(yf_ref[...], wf_ref[0], preferred_element_type=jnp.float32)
    o_ref[...] = (ga_ref[...] * ua + gf_ref[...] * uf).astype(o_ref.dtype)


def _branch_merge(ya, yf, w, gates, tm=1024, tn=512):
    M, W = ya.shape
    D = w.shape[2]
    tm, tn = min(tm, M), min(tn, D)
    nj = D // tn
    return pl.pallas_call(
        _branch_merge_kernel,
        grid=(M // tm, nj),
        in_specs=[pl.BlockSpec((tm, W), lambda i, j: (i, 0)),
                  pl.BlockSpec((tm, W), lambda i, j: (i, 0)),
                  pl.BlockSpec((1, W, tn), lambda i, j: (0, 0, j)),
                  pl.BlockSpec((1, W, tn), lambda i, j: (1, 0, j)),
                  pl.BlockSpec((tm, tn), lambda i, j: (i, j)),
                  pl.BlockSpec((tm, tn), lambda i, j: (i, nj + j))],
        out_specs=pl.BlockSpec((tm, tn), lambda i, j: (i, j)),
        out_shape=jax.ShapeDtypeStruct((M, D), jnp.bfloat16),
        compiler_params=_params("parallel", "parallel"),
        name="branch_merge",
    )(ya, yf, w, w, gates, gates)


LN_ROW_CHUNK = 32


def _out_layernorm_kernel(h_ref, w_ref, x_ref, g_ref, b_ref, o_ref, pre_even_ref, pre_odd_ref):
    i, j = pl.program_id(0), pl.program_id(1)
    tn = w_ref.shape[1]
    chunk = o_ref.shape[0]

    @pl.when((i == 0) & (j == 0))
    def _():
        pre_odd_ref[...] = jnp.zeros_like(pre_odd_ref)

    def step(cur_ref, prev_ref):
        rows = min(LN_ROW_CHUNK, chunk)
        for c in range(chunk // rows):
            r = prev_ref[pl.ds(pl.multiple_of(j * chunk + c * rows, rows), rows), :]
            mu = jnp.mean(r, axis=-1, keepdims=True)
            d = r - mu
            var = jnp.mean(d * d, axis=-1, keepdims=True)
            o_ref[c * rows:(c + 1) * rows, :] = d * lax.rsqrt(var + LN_EPS) * g_ref[...] + b_ref[...]
        out = jnp.dot(h_ref[...], w_ref[...], preferred_element_type=jnp.float32)
        cur_ref[:, pl.ds(pl.multiple_of(j * tn, tn), tn)] = DEEPNORM_ALPHA * x_ref[...] + out

    @pl.when(i % 2 == 0)
    def _():
        step(pre_even_ref, pre_odd_ref)

    @pl.when(i % 2 == 1)
    def _():
        step(pre_odd_ref, pre_even_ref)


def _out_layernorm(h, w, x, gain, bias, tm=512, tn=1024):
    M, D = x.shape
    tm, tn = min(tm, M), min(tn, D)
    n_i, n_j = M // tm, D // tn
    chunk = tm // n_j
    assert chunk % 8 == 0
    return pl.pallas_call(
        _out_layernorm_kernel,
        grid=(n_i + 1, n_j),
        in_specs=[pl.BlockSpec((tm, D), lambda i, j: (jnp.minimum(i, n_i - 1), 0)),
                  pl.BlockSpec((D, tn), lambda i, j: (0, j)),
                  pl.BlockSpec((tm, tn), lambda i, j: (jnp.minimum(i, n_i - 1), j)),
                  pl.BlockSpec((1, D), lambda i, j: (0, 0)),
                  pl.BlockSpec((1, D), lambda i, j: (0, 0))],
        out_specs=pl.BlockSpec((chunk, D), lambda i, j: (jnp.where(i == 0, 0, (i - 1) * n_j + j), 0)),
        out_shape=jax.ShapeDtypeStruct((M, D), jnp.float32),
        scratch_shapes=[pltpu.VMEM((tm, D), jnp.float32), pltpu.VMEM((tm, D), jnp.float32)],
        compiler_params=_params("arbitrary", "arbitrary"),
        name="out_layernorm",
    )(h, w, x, gain, bias)


def kernel(x, w_in, b_forget, b_gate, rel_bias_table, w_branch, w_out, ln_gain, ln_bias):
    B, S, D = x.shape
    Ha, Hf = MOBA_HEADS, FOX_HEADS
    Wa, Wf = Ha * HEAD_DIM, Hf * HEAD_DIM
    bf16, f32 = jnp.bfloat16, jnp.float32
    depth = w_in.shape[0]
    for layer in range(depth):
        w_t = jnp.swapaxes(w_in[layer], 0, 1).astype(bf16)
        n_attn = 4 * Wa + 4 * Wf
        w_f = jnp.pad(w_t[n_attn:n_attn + Hf], ((0, LANES - Hf), (0, 0)))
        b_f = jnp.pad(b_forget[layer].astype(f32), (0, LANES - Hf))[None, :]
        x2 = x.reshape(B * S, D).astype(f32)
        xb, cx = _forget_bias_columns(x2, w_f, b_f, Hf, B)

        scale = HEAD_DIM ** -0.5 * LOG2E
        qkv_scale = jnp.concatenate([jnp.full((Wa,), scale, f32), jnp.ones((2 * Wa,), f32),
                                     jnp.full((Wf,), scale, f32), jnp.ones((2 * Wf,), f32)])[None, :]
        tn = PROJ_TN
        assert Wa % tn == 0 and Wf % tn == 0
        a_blocks, f_blocks = Wa // tn, Wf // tn
        qkv = _matmul(xb, w_t, qkv_scale, jnp.zeros_like(qkv_scale), bf16, "none",
                      lambda j: jnp.where(j < 3 * a_blocks, j, j + a_blocks), tn=tn).reshape(B, S, -1)
        ones_z = jnp.ones((1, Wa + Wf), f32)
        sz = _matmul(xb, w_t, ones_z, jnp.zeros_like(ones_z), f32, "silu",
                     lambda j: jnp.where(j < a_blocks, j + 3 * a_blocks, j + 3 * a_blocks + 3 * f_blocks),
                     tn=tn).reshape(B, S, -1)
        gates = _matmul(xb, w_t, jnp.ones((1, 2 * D), f32), b_gate[layer].reshape(1, 2 * D).astype(f32),
                        f32, "sigmoid", w_row_start=n_attn + Hf, tn=tn)

        ya = _moba_attention(qkv, rel_bias_table, sz, Ha, 0, 0)
        yf = _fox_attention(qkv, cx, sz, Hf, 3 * Ha, Ha)

        assert Wa == Wf
        merged = _branch_merge(ya.reshape(B * S, Wa), yf.reshape(B * S, Wf), w_branch[layer].astype(bf16), gates)
        y = _out_layernorm(merged, w_out[layer].astype(bf16), x2,
                           ln_gain[layer].reshape(1, D).astype(f32), ln_bias[layer].reshape(1, D).astype(f32))
        x = y.reshape(B, S, D).astype(x.dtype)
    return x
```

```python
import functools
import math
from typing import Any, Callable, NamedTuple, Sequence

import jax
import jax.numpy as jnp
import numpy as np
from jax import lax
from jax.experimental import pallas as pl
from jax.experimental.pallas import tpu as pltpu

HEAD_DIM = 128
LANES = 128
MOBA_HEADS = 16
FOX_HEADS = 16
MOBA_BLOCK = 256
MOBA_TOPK = 3
T5_NUM_BUCKETS = 32
T5_MAX_DISTANCE = 128
LN_EPS = 1e-5
DEPTH = 1
DEEPNORM_ALPHA = (2.0 * DEPTH) ** 0.25
LOG2E = math.log2(math.e)

V7X_VMEM_LIMIT_BYTES = 56 * 1024 * 1024
PROJ_TN = 1024
NEG_BIG = -1e30

_NT_DIMS = (((1,), (1,)), ((), ()))


def _params(*semantics):
    return pltpu.CompilerParams(dimension_semantics=semantics,
                                vmem_limit_bytes=V7X_VMEM_LIMIT_BYTES)


def _split3_bf16(a):
    hi = a.astype(jnp.bfloat16).astype(jnp.float32)
    r = a - hi
    mid = r.astype(jnp.bfloat16).astype(jnp.float32)
    lo = (r - mid).astype(jnp.bfloat16).astype(jnp.float32)
    return hi, mid, lo


def _matmul_kernel(x_ref, wt_ref, s_ref, b_ref, o_ref, *, act):
    acc = lax.dot_general(x_ref[...], wt_ref[...], _NT_DIMS, preferred_element_type=jnp.float32)
    acc = acc * s_ref[...] + b_ref[...]
    if act == "silu":
        acc = acc * jax.nn.sigmoid(acc)
    elif act == "sigmoid":
        acc = jax.nn.sigmoid(acc)
    o_ref[...] = acc.astype(o_ref.dtype)


def _matmul(x, w_t, col_scale, col_bias, out_dtype, act, w_row_block=lambda j: j, w_row_start=None,
            tm=1024, tn=1024):
    M, K = x.shape
    N = col_scale.shape[1]
    tm, tn = min(tm, M), min(tn, N)
    if w_row_start is None:
        w_spec = pl.BlockSpec((tn, K), lambda i, j: (w_row_block(j), 0))
    else:
        align = math.gcd(w_row_start, tn)
        w_spec = pl.BlockSpec((pl.Element(tn), pl.Element(K)),
                              lambda i, j: (pl.multiple_of(w_row_start + j * tn, align), 0))
    return pl.pallas_call(
        functools.partial(_matmul_kernel, act=act),
        grid=(M // tm, N // tn),
        in_specs=[pl.BlockSpec((tm, K), lambda i, j: (i, 0)),
                  w_spec,
                  pl.BlockSpec((1, tn), lambda i, j: (0, j)),
                  pl.BlockSpec((1, tn), lambda i, j: (0, j))],
        out_specs=pl.BlockSpec((tm, tn), lambda i, j: (i, j)),
        out_shape=jax.ShapeDtypeStruct((M, N), out_dtype),
        compiler_params=_params("parallel", "parallel"),
        name="proj_" + act,
    )(x, w_t, col_scale, col_bias)


def _forget_bias_kernel(x_ref, w_ref, b_ref, xb_ref, o_ref, carry_ref, *, heads):
    @pl.when(pl.program_id(1) == 0)
    def _():
        carry_ref[...] = jnp.zeros_like(carry_ref)

    ts = x_ref.shape[0]
    xb = x_ref[...].astype(xb_ref.dtype)
    xb_ref[...] = xb
    z = lax.dot_general(xb, w_ref[...], _NT_DIMS, preferred_element_type=jnp.float32) + b_ref[...]
    log_f = jnp.minimum(z, 0.0) - jnp.log1p(jnp.exp(-jnp.abs(z)))
    row = lax.broadcasted_iota(jnp.int32, (ts, ts), 0)
    col = lax.broadcasted_iota(jnp.int32, (ts, ts), 1)
    lower = jnp.where(col <= row, 1.0, 0.0).astype(jnp.bfloat16)
    c = carry_ref[...]
    for part in _split3_bf16(log_f):
        c = c + jnp.dot(lower, part.astype(jnp.bfloat16), preferred_element_type=jnp.float32)
    carry_ref[...] = c[ts - 1:ts, :]
    hi, mid, lo = _split3_bf16(c * (-LOG2E))
    lane = lax.broadcasted_iota(jnp.int32, (ts, LANES), 1)
    cols = jnp.where(lane < heads, hi,
                     jnp.where(lane < 2 * heads, pltpu.roll(mid, heads, 1),
                               jnp.where(lane < 3 * heads, pltpu.roll(lo, 2 * heads, 1), 0.0)))
    o_ref[0] = cols.astype(o_ref.dtype)


def _forget_bias_columns(x, w_f, b_forget, heads, batch, ts=512):
    M, D = x.shape
    S = M // batch
    assert 3 * heads <= LANES
    ts = min(ts, S)
    steps = S // ts
    return pl.pallas_call(
        functools.partial(_forget_bias_kernel, heads=heads),
        grid=(batch, steps),
        in_specs=[pl.BlockSpec((ts, D), lambda b, i: (b * steps + i, 0)),
                  pl.BlockSpec((LANES, D), lambda b, i: (0, 0)),
                  pl.BlockSpec((1, LANES), lambda b, i: (0, 0))],
        out_specs=[pl.BlockSpec((ts, D), lambda b, i: (b * steps + i, 0)),
                   pl.BlockSpec((1, ts, LANES), lambda b, i: (b, i, 0))],
        out_shape=[jax.ShapeDtypeStruct((M, D), jnp.bfloat16),
                   jax.ShapeDtypeStruct((batch, S, LANES), jnp.bfloat16)],
        scratch_shapes=[pltpu.VMEM((1, LANES), jnp.float32)],
        compiler_params=_params("parallel", "arbitrary"),
        name="forget_bias_columns",
    )(x, w_f, b_forget)


ATTN_HEADS_PER_STEP = 4


def _softmax_tile(s, v_ext, m_ref, acc_ref):
    tk = s.shape[1]
    chunks = [s[:, c:c + LANES] for c in range(0, tk, LANES)]
    mx = functools.reduce(jnp.maximum, chunks)
    m_prev = m_ref[...]
    m_new = jnp.maximum(m_prev, jnp.max(mx, axis=-1, keepdims=True))
    alpha = jnp.exp2(m_prev - m_new)
    p = jnp.concatenate([jnp.exp2(ch - m_new) for ch in chunks], axis=1).astype(v_ext.dtype)
    pv = jnp.dot(p, v_ext, preferred_element_type=jnp.float32)
    acc_ref[...] = acc_ref[...] * jnp.concatenate([alpha, alpha], axis=1) + pv
    m_ref[...] = m_new


class _Part(NamedTuple):
    row0: int
    rows: int
    width: int
    post: Callable[[Any], Any]


class _Tile(NamedTuple):
    qk: Callable[[], Sequence[Any]]
    parts: Sequence[_Part]
    values: Callable[[], Any]


class _HeadTiles(NamedTuple):
    specials: Sequence[_Tile]
    qk: Callable[[Any], Any]
    values: Callable[[Any], Any]
    s_refs: Sequence[Any]
    m_ref: Any
    acc_ref: Any


def _pipelined_tiles(heads, n_rest):
    def fill(buf, logits_of):
        for hd in heads:
            s = logits_of(hd)
            hd.s_refs[buf][:, :s.shape[1]] = s

    def consume(buf, values_of):
        for hd in heads:
            v_ext = values_of(hd)
            _softmax_tile(hd.s_refs[buf][:, :v_ext.shape[0]], v_ext, hd.m_ref, hd.acc_ref)

    def fill_special(buf, t):
        for hd in heads:
            tile = hd.specials[t]
            for part, s in zip(tile.parts, tile.qk()):
                hd.s_refs[buf][part.row0:part.row0 + part.rows, :part.width] = s

    def consume_special(buf, t):
        for hd in heads:
            tile = hd.specials[t]
            v_ext = tile.values()
            for part in tile.parts:
                rows = pl.ds(part.row0, part.rows)
                s = hd.s_refs[buf][part.row0:part.row0 + part.rows, :part.width]
                _softmax_tile(part.post(s), v_ext[:part.width], hd.m_ref.at[rows], hd.acc_ref.at[rows])

    n_special = len(heads[0].specials)
    no_uniform_tiles = isinstance(n_rest, int) and n_rest == 0
    fill_special(0, 0)
    for t in range(n_special):
        if t + 1 < n_special:
            fill_special((t + 1) % 2, t + 1)
        elif not no_uniform_tiles:
            fill((t + 1) % 2, lambda hd: hd.qk(0))
        consume_special(t % 2, t)
    if no_uniform_tiles:
        return
    cur, nxt = n_special % 2, (n_special + 1) % 2

    def two_tiles(i, carry):
        fill(nxt, lambda hd: hd.qk(2 * i + 1))
        consume(cur, lambda hd: hd.values(2 * i))
        fill(cur, lambda hd: hd.qk(jnp.minimum(2 * i + 2, n_rest - 1)))
        consume(nxt, lambda hd: hd.values(2 * i + 1))
        return carry

    lax.fori_loop(0, n_rest // 2, two_tiles, 0)

    @pl.when(n_rest % 2 == 1)
    def _():
        consume(cur, lambda hd: hd.values(n_rest - 1))


def _diagonal_tile(q_aug, key_tile, values, bias=None):
    tq = q_aug.shape[0]
    half = tq // 2

    def masked(row0, rows, width):
        def post(s):
            row = lax.broadcasted_iota(jnp.int32, (rows, width), 0) + row0
            col = lax.broadcasted_iota(jnp.int32, (rows, width), 1)
            if bias is not None:
                s = s + bias(row0, rows, width)
            return jnp.where(col <= row, s, NEG_BIG)
        return _Part(row0, rows, width, post)

    def qk():
        k = key_tile()
        return [lax.dot_general(q_aug[:half], k[:half], _NT_DIMS, preferred_element_type=jnp.float32),
                lax.dot_general(q_aug[half:], k, _NT_DIMS, preferred_element_type=jnp.float32)]

    return _Tile(qk, [masked(0, half, half), masked(half, tq - half, tq)], values)


def _init_softmax_state(m_ref, acc_ref):
    m_ref[...] = jnp.full_like(m_ref, NEG_BIG)
    acc_ref[...] = jnp.zeros_like(acc_ref)


def _finish_attention(acc_ref, z, dtype):
    dh = z.shape[1]
    return (acc_ref[:, :dh] / acc_ref[:, dh:] * z).astype(dtype)


def _fox_kernel(q_ref, k_ref, cx_ref, v_ref, z_ref, o_ref, kaug_ref, vext_ref, sa_ref, sb_ref, m_ref, acc_ref,
                *, heads):
    qi = pl.program_id(2)
    tq = q_ref.shape[1]
    dh = HEAD_DIM
    G = kaug_ref.shape[0]

    def rows(j):
        return pl.ds(pl.multiple_of(j * tq, tq), tq)

    for g in range(G):
        kaug_ref[g, rows(qi), :dh] = k_ref[0, :, g * dh:(g + 1) * dh]
        kaug_ref[g, rows(qi), dh:] = cx_ref[0]
        vext_ref[g, rows(qi), :dh] = v_ref[0, :, g * dh:(g + 1) * dh]
        vext_ref[g, rows(qi), dh:] = jnp.ones((tq, dh), vext_ref.dtype)

    lane = lax.broadcasted_iota(jnp.int32, (1, LANES), 1)
    tiles = []
    for g in range(G):
        h = pl.program_id(1) * G + g
        pick = (lane == h) | (lane == heads + h) | (lane == 2 * heads + h)
        ones_at_head_terms = jnp.broadcast_to(jnp.where(pick, 1.0, 0.0), (tq, LANES)).astype(q_ref.dtype)
        q_aug = jnp.concatenate([q_ref[0, :, g * dh:(g + 1) * dh], ones_at_head_terms], axis=1)
        _init_softmax_state(m_ref.at[g], acc_ref.at[g])

        def logits(j, g=g, q_aug=q_aug):
            return lax.dot_general(q_aug, kaug_ref[g, rows(j), :], _NT_DIMS, preferred_element_type=jnp.float32)

        def values(j, g=g):
            return vext_ref[g, rows(j), :]

        diagonal = _diagonal_tile(q_aug, lambda g=g: kaug_ref[g, rows(qi), :], functools.partial(values, qi))
        tiles.append(_HeadTiles([diagonal], logits, values, (sa_ref.at[g], sb_ref.at[g]), m_ref.at[g], acc_ref.at[g]))

    @pl.when(qi == 0)
    def _():
        _pipelined_tiles(tiles, 0)

    @pl.when(qi >= 1)
    def _():
        _pipelined_tiles(tiles, qi)

    for g in range(G):
        o_ref[0, :, g * dh:(g + 1) * dh] = _finish_attention(acc_ref.at[g], z_ref[0, :, g * dh:(g + 1) * dh],
                                                             o_ref.dtype)


def _fox_attention(qkv, cx, sz, H, col0, zcol0, tq=512):
    B, S, _ = qkv.shape
    tq = min(tq, S)
    Dh = HEAD_DIM
    G = ATTN_HEADS_PER_STEP
    assert H % G == 0 and col0 % G == 0 and zcol0 % G == 0
    W = G * Dh
    return pl.pallas_call(
        functools.partial(_fox_kernel, heads=H),
        grid=(B, H // G, S // tq),
        in_specs=[pl.BlockSpec((1, tq, W), lambda b, h, i: (b, i, col0 // G + h)),
                  pl.BlockSpec((1, tq, W), lambda b, h, i: (b, i, (col0 + H) // G + h)),
                  pl.BlockSpec((1, tq, LANES), lambda b, h, i: (b, i, 0)),
                  pl.BlockSpec((1, tq, W), lambda b, h, i: (b, i, (col0 + 2 * H) // G + h)),
                  pl.BlockSpec((1, tq, W), lambda b, h, i: (b, i, zcol0 // G + h))],
        out_specs=pl.BlockSpec((1, tq, W), lambda b, h, i: (b, i, h)),
        out_shape=jax.ShapeDtypeStruct((B, S, H * Dh), jnp.bfloat16),
        scratch_shapes=[pltpu.VMEM((G, S, Dh + LANES), jnp.bfloat16), pltpu.VMEM((G, S, 2 * Dh), jnp.bfloat16),
                        pltpu.VMEM((G, tq, tq), jnp.float32), pltpu.VMEM((G, tq, tq), jnp.float32),
                        pltpu.VMEM((G, tq, LANES), jnp.float32), pltpu.VMEM((G, tq, 2 * Dh), jnp.float32)],
        compiler_params=_params("arbitrary", "arbitrary", "arbitrary"),
        name="fox_attention",
    )(qkv, qkv, cx, qkv, sz)


def _t5_bucket_np(dist):
    max_exact = T5_NUM_BUCKETS // 2
    d = np.maximum(dist, 1).astype(np.float32)
    ratio = (np.log(d / np.float32(max_exact)) / np.float32(math.log(T5_MAX_DISTANCE / max_exact))
             * np.float32(T5_NUM_BUCKETS - max_exact))
    large = np.minimum(max_exact + ratio.astype(np.int32), T5_NUM_BUCKETS - 1)
    return np.where(dist < max_exact, dist, large).astype(np.int32)


MOBA_Q_BLOCKS = 2
MOBA_TERMS = 3


def _moba_query_columns(q, kmean, far_row, first_block, blk, nb):
    nbp, tq = kmean.shape[0], q.shape[0]
    km_hi = kmean.astype(jnp.bfloat16)
    km_lo = (kmean - km_hi.astype(jnp.float32)).astype(jnp.bfloat16)
    gate = (lax.dot_general(km_hi, q, _NT_DIMS, preferred_element_type=jnp.float32)
            + lax.dot_general(km_lo, q, _NT_DIMS, preferred_element_type=jnp.float32))
    n_idx = lax.broadcasted_iota(jnp.int32, (nbp, tq), 0)
    own = first_block + lax.broadcasted_iota(jnp.int32, (nbp, tq), 1) // blk
    past = n_idx < own
    gate = jnp.where(past, gate, NEG_BIG)
    beaten = jnp.zeros((nbp, tq), jnp.int32)
    for n in range(nb):
        g_n = gate[n:n + 1, :]
        lower_index_wins_ties = jnp.where(n_idx > n, 1, 0)
        beaten = beaten + jnp.where(g_n > gate, 1, 0) + jnp.where(g_n == gate, lower_index_wins_ties, 0)
    selected = past & (beaten < MOBA_TOPK)
    far_bias = jnp.broadcast_to(far_row, (nbp, tq))
    term = jnp.where(selected, jnp.where(n_idx == own - 1, 0.0, far_bias), NEG_BIG)
    term = jnp.where(n_idx == own, 0.0, term)
    term_hi, term_mid, term_lo = _split3_bf16(term)
    pieces = [term_hi[:nb], jnp.where(selected, term_mid, 0.0)[:nb], jnp.where(selected, term_lo, 0.0)[:nb]]
    if MOBA_TERMS * nb < LANES:
        pieces.append(jnp.zeros((LANES - MOBA_TERMS * nb, tq), jnp.float32))
    return jnp.concatenate(pieces, axis=0).T.astype(q.dtype)


def _moba_kernel(q_ref, k_ref, v_ref, rbias_ref, far_ref, z_ref, o_ref,
                 kaug_ref, vext_ref, kmean_ref, t5_ref, sa_ref, sb_ref, m_ref, acc_ref, *, nb, blk):
    qi = pl.program_id(2)
    tq = q_ref.shape[1]
    dh = HEAD_DIM
    G = kaug_ref.shape[0]
    first_block = qi * (tq // blk)

    def rows(j):
        return pl.ds(pl.multiple_of(j * tq, tq), tq)

    @pl.when(qi == 0)
    def _():
        for g in range(G):
            kmean_ref[g] = jnp.zeros(kmean_ref.shape[1:], kmean_ref.dtype)
            rb = jnp.broadcast_to(rbias_ref[g], (tq, tq))
            t5_ref[g] = pltpu.roll(rb, 1, 1, stride=1, stride_axis=0)

    key_block = first_block + lax.broadcasted_iota(jnp.int32, (tq, LANES), 0) // blk
    lane = lax.broadcasted_iota(jnp.int32, (1, LANES), 1)
    lane_block = jnp.full((1, LANES), -1, jnp.int32)
    for term in range(MOBA_TERMS):
        lane_block = jnp.where((lane >= term * nb) & (lane < (term + 1) * nb), lane - term * nb, lane_block)
    hot = jnp.where(jnp.broadcast_to(lane_block, (tq, LANES)) == key_block, 1.0, 0.0).astype(kaug_ref.dtype)
    for g in range(G):
        k = k_ref[0, :, g * dh:(g + 1) * dh]
        kaug_ref[g, rows(qi), :dh] = k
        kaug_ref[g, rows(qi), dh:] = hot
        vext_ref[g, rows(qi), :dh] = v_ref[0, :, g * dh:(g + 1) * dh]
        vext_ref[g, rows(qi), dh:] = jnp.ones((tq, dh), vext_ref.dtype)
        for n in range(tq // blk):
            kb = k[n * blk:(n + 1) * blk, :].astype(jnp.float32)
            kmean_ref[g, pl.ds(first_block + n, 1), :] = jnp.sum(kb, axis=0, keepdims=True) * (1.0 / blk)

    def add_previous_block_bias(s, g):
        return jnp.concatenate([s[:, :tq - blk], s[:, tq - blk:] + t5_ref[g, tq - blk:, :blk]], axis=1)

    tiles = []
    for g in range(G):
        q = q_ref[0, :, g * dh:(g + 1) * dh]
        q_extra = _moba_query_columns(q, kmean_ref[g], far_ref[g], first_block, blk, nb)
        q_aug = jnp.concatenate([q, q_extra], axis=1)
        _init_softmax_state(m_ref.at[g], acc_ref.at[g])

        def logits(j, g=g, q_aug=q_aug):
            return lax.dot_general(q_aug, kaug_ref[g, rows(j), :], _NT_DIMS, preferred_element_type=jnp.float32)

        def values(j, g=g):
            return vext_ref[g, rows(j), :]

        diagonal = _diagonal_tile(q_aug, lambda g=g: kaug_ref[g, rows(qi), :], functools.partial(values, qi),
                                  bias=lambda row0, n, width, g=g: t5_ref[g, row0:row0 + n, :width])

        def previous_logits(logits=logits):
            s_prev = logits(qi - 1)
            return [s_prev[:blk], s_prev[blk:]]

        previous = _Tile(previous_logits,
                         [_Part(0, blk, tq, functools.partial(add_previous_block_bias, g=g)),
                          _Part(blk, tq - blk, tq, lambda s_rows: s_rows)],
                         functools.partial(values, qi - 1))
        tiles.append((diagonal, previous, logits, values, (sa_ref.at[g], sb_ref.at[g]), m_ref.at[g], acc_ref.at[g]))

    @pl.when(qi == 0)
    def _():
        _pipelined_tiles([_HeadTiles([t[0]], *t[2:]) for t in tiles], 0)

    @pl.when(qi >= 1)
    def _():
        _pipelined_tiles([_HeadTiles([t[0], t[1]], *t[2:]) for t in tiles], qi - 1)

    for g in range(G):
        o_ref[0, :, g * dh:(g + 1) * dh] = _finish_attention(acc_ref.at[g], z_ref[0, :, g * dh:(g + 1) * dh],
                                                             o_ref.dtype)


def _moba_attention(qkv, rel_bias_table, sz, H, col0, zcol0):
    B, S, _ = qkv.shape
    blk = MOBA_BLOCK
    tq = MOBA_Q_BLOCKS * blk
    assert MOBA_Q_BLOCKS == 2 and S % tq == 0
    nb = S // blk
    nbp = -(-nb // 8) * 8
    assert MOBA_TERMS * nb <= LANES
    Dh = HEAD_DIM
    G = ATTN_HEADS_PER_STEP
    assert H % G == 0 and col0 % G == 0 and zcol0 % G == 0
    W = G * Dh
    table_t = rel_bias_table.T.astype(jnp.float32) * LOG2E
    rbias = table_t[:, _t5_bucket_np(np.arange(tq)[::-1])][:, None, :]
    far_bucket = int(_t5_bucket_np(np.array([blk + 1]))[0])
    assert (_t5_bucket_np(np.arange(blk + 1, S + blk)) == far_bucket).all()
    far = jnp.broadcast_to(table_t[:, far_bucket][:, None, None], (H, 1, tq))
    return pl.pallas_call(
        functools.partial(_moba_kernel, nb=nb, blk=blk),
        grid=(B, H // G, S // tq),
        in_specs=[pl.BlockSpec((1, tq, W), lambda b, h, i: (b, i, col0 // G + h)),
                  pl.BlockSpec((1, tq, W), lambda b, h, i: (b, i, (col0 + H) // G + h)),
                  pl.BlockSpec((1, tq, W), lambda b, h, i: (b, i, (col0 + 2 * H) // G + h)),
                  pl.BlockSpec((G, 1, tq), lambda b, h, i: (h, 0, 0)),
                  pl.BlockSpec((G, 1, tq), lambda b, h, i: (h, 0, 0)),
                  pl.BlockSpec((1, tq, W), lambda b, h, i: (b, i, zcol0 // G + h))],
        out_specs=pl.BlockSpec((1, tq, W), lambda b, h, i: (b, i, h)),
        out_shape=jax.ShapeDtypeStruct((B, S, H * Dh), jnp.bfloat16),
        scratch_shapes=[pltpu.VMEM((G, S, Dh + LANES), jnp.bfloat16), pltpu.VMEM((G, S, 2 * Dh), jnp.bfloat16),
                        pltpu.VMEM((G, nbp, Dh), jnp.float32), pltpu.VMEM((G, tq, tq), jnp.float32),
                        pltpu.VMEM((G, tq, tq), jnp.float32), pltpu.VMEM((G, tq, tq), jnp.float32),
                        pltpu.VMEM((G, tq, LANES), jnp.float32), pltpu.VMEM((G, tq, 2 * Dh), jnp.float32)],
        compiler_params=_params("arbitrary", "arbitrary", "arbitrary"),
        name="moba_attention",
    )(qkv, qkv, qkv, rbias, far, sz)


def _branch_merge_kernel(ya_ref, yf_ref, wa_ref, wf_ref, ga_ref, gf_ref, o_ref):
    ua = jnp.dot(ya_ref[...], wa_ref[0], preferred_element_type=jnp.float32)
    uf = jnp.dot(yf_ref[...], wf_ref[0], preferred_element_type=jnp.float32)
    o_ref[...] = (ga_ref[...] * ua + gf_ref[...] * uf).astype(o_ref.dtype)


def _branch_merge(ya, yf, w, gates, tm=1024, tn=512):
    M, W = ya.shape
    D = w.shape[2]
    tm, tn = min(tm, M), min(tn, D)
    nj = D // tn
    return pl.pallas_call(
        _branch_merge_kernel,
        grid=(M // tm, nj),
        in_specs=[pl.BlockSpec((tm, W), lambda i, j: (i, 0)),
                  pl.BlockSpec((tm, W), lambda i, j: (i, 0)),
                  pl.BlockSpec((1, W, tn), lambda i, j: (0, 0, j)),
                  pl.BlockSpec((1, W, tn), lambda i, j: (1, 0, j)),
                  pl.BlockSpec((tm, tn), lambda i, j: (i, j)),
                  pl.BlockSpec((tm, tn), lambda i, j: (i, nj + j))],
        out_specs=pl.BlockSpec((tm, tn), lambda i, j: (i, j)),
        out_shape=jax.ShapeDtypeStruct((M, D), jnp.bfloat16),
        compiler_params=_params("parallel", "parallel"),
        name="branch_merge",
    )(ya, yf, w, w, gates, gates)


LN_ROW_CHUNK = 32


def _out_layernorm_kernel(h_ref, w_ref, x_ref, g_ref, b_ref, o_ref, pre_even_ref, pre_odd_ref):
    i, j = pl.program_id(0), pl.program_id(1)
    tn = w_ref.shape[1]
    chunk = o_ref.shape[0]

    @pl.when((i == 0) & (j == 0))
    def _():
        pre_odd_ref[...] = jnp.zeros_like(pre_odd_ref)

    def step(cur_ref, prev_ref):
        rows = min(LN_ROW_CHUNK, chunk)
        for c in range(chunk // rows):
            r = prev_ref[pl.ds(pl.multiple_of(j * chunk + c * rows, rows), rows), :]
            mu = jnp.mean(r, axis=-1, keepdims=True)
            d = r - mu
            var = jnp.mean(d * d, axis=-1, keepdims=True)
            o_ref[c * rows:(c + 1) * rows, :] = d * lax.rsqrt(var + LN_EPS) * g_ref[...] + b_ref[...]
        out = jnp.dot(h_ref[...], w_ref[...], preferred_element_type=jnp.float32)
        cur_ref[:, pl.ds(pl.multiple_of(j * tn, tn), tn)] = DEEPNORM_ALPHA * x_ref[...] + out

    @pl.when(i % 2 == 0)
    def _():
        step(pre_even_ref, pre_odd_ref)

    @pl.when(i % 2 == 1)
    def _():
        step(pre_odd_ref, pre_even_ref)


def _out_layernorm(h, w, x, gain, bias, tm=512, tn=1024):
    M, D = x.shape
    tm, tn = min(tm, M), min(tn, D)
    n_i, n_j = M // tm, D // tn
    chunk = tm // n_j
    assert chunk % 8 == 0
    return pl.pallas_call(
        _out_layernorm_kernel,
        grid=(n_i + 1, n_j),
        in_specs=[pl.BlockSpec((tm, D), lambda i, j: (jnp.minimum(i, n_i - 1), 0)),
                  pl.BlockSpec((D, tn), lambda i, j: (0, j)),
                  pl.BlockSpec((tm, tn), lambda i, j: (jnp.minimum(i, n_i - 1), j)),
                  pl.BlockSpec((1, D), lambda i, j: (0, 0)),
                  pl.BlockSpec((1, D), lambda i, j: (0, 0))],
        out_specs=pl.BlockSpec((chunk, D), lambda i, j: (jnp.where(i == 0, 0, (i - 1) * n_j + j), 0)),
        out_shape=jax.ShapeDtypeStruct((M, D), jnp.float32),
        scratch_shapes=[pltpu.VMEM((tm, D), jnp.float32), pltpu.VMEM((tm, D), jnp.float32)],
        compiler_params=_params("arbitrary", "arbitrary"),
        name="out_layernorm",
    )(h, w, x, gain, bias)


def kernel(x, w_in, b_forget, b_gate, rel_bias_table, w_branch, w_out, ln_gain, ln_bias):
    B, S, D = x.shape
    Ha, Hf = MOBA_HEADS, FOX_HEADS
    Wa, Wf = Ha * HEAD_DIM, Hf * HEAD_DIM
    bf16, f32 = jnp.bfloat16, jnp.float32
    depth = w_in.shape[0]
    for layer in range(depth):
        w_t = jnp.swapaxes(w_in[layer], 0, 1).astype(bf16)
        n_attn = 4 * Wa + 4 * Wf
        w_f = jnp.pad(w_t[n_attn:n_attn + Hf], ((0, LANES - Hf), (0, 0)))
        b_f = jnp.pad(b_forget[layer].astype(f32), (0, LANES - Hf))[None, :]
        x2 = x.reshape(B * S, D).astype(f32)
        xb, cx = _forget_bias_columns(x2, w_f, b_f, Hf, B)

        scale = HEAD_DIM ** -0.5 * LOG2E
        qkv_scale = jnp.concatenate([jnp.full((Wa,), scale, f32), jnp.ones((2 * Wa,), f32),
                                     jnp.full((Wf,), scale, f32), jnp.ones((2 * Wf,), f32)])[None, :]
        tn = PROJ_TN
        assert Wa % tn == 0 and Wf % tn == 0
        a_blocks, f_blocks = Wa // tn, Wf // tn
        qkv = _matmul(xb, w_t, qkv_scale, jnp.zeros_like(qkv_scale), bf16, "none",
                      lambda j: jnp.where(j < 3 * a_blocks, j, j + a_blocks), tn=tn).reshape(B, S, -1)
        ones_z = jnp.ones((1, Wa + Wf), f32)
        sz = _matmul(xb, w_t, ones_z, jnp.zeros_like(ones_z), f32, "silu",
                     lambda j: jnp.where(j < a_blocks, j + 3 * a_blocks, j + 3 * a_blocks + 3 * f_blocks),
                     tn=tn).reshape(B, S, -1)
        gates = _matmul(xb, w_t, jnp.ones((1, 2 * D), f32), b_gate[layer].reshape(1, 2 * D).astype(f32),
                        f32, "sigmoid", w_row_start=n_attn + Hf, tn=tn)

        ya = _moba_attention(qkv, rel_bias_table, sz, Ha, 0, 0)
        yf = _fox_attention(qkv, cx, sz, Hf, 3 * Ha, Ha)

        assert Wa == Wf
        merged = _branch_merge(ya.reshape(B * S, Wa), yf.reshape(B * S, Wf), w_branch[layer].astype(bf16), gates)
        y = _out_layernorm(merged, w_out[layer].astype(bf16), x2,
                           ln_gain[layer].reshape(1, D).astype(f32), ln_bias[layer].reshape(1, D).astype(f32))
        x = y.reshape(B, S, D).astype(x.dtype)
    return x
```

```python
import functools
import math
from typing import Any, Callable, NamedTuple, Sequence

import jax
import jax.numpy as jnp
import numpy as np
from jax import lax
from jax.experimental import pallas as pl
from jax.experimental.pallas import tpu as pltpu

HEAD_DIM = 128
LANES = 128
MOBA_HEADS = 16
FOX_HEADS = 16
MOBA_BLOCK = 256
MOBA_TOPK = 3
T5_NUM_BUCKETS = 32
T5_MAX_DISTANCE = 128
LN_EPS = 1e-5
DEPTH = 1
DEEPNORM_ALPHA = (2.0 * DEPTH) ** 0.25
LOG2E = math.log2(math.e)

V7X_VMEM_LIMIT_BYTES = 56 * 1024 * 1024
PROJ_TN = 1024
NEG_BIG = -1e30

_NT_DIMS = (((1,), (1,)), ((), ()))


def _params(*semantics):
    return pltpu.CompilerParams(dimension_semantics=semantics,
                                vmem_limit_bytes=V7X_VMEM_LIMIT_BYTES)


def _split3_bf16(a):
    hi = a.astype(jnp.bfloat16).astype(jnp.float32)
    r = a - hi
    mid = r.astype(jnp.bfloat16).astype(jnp.float32)
    lo = (r - mid).astype(jnp.bfloat16).astype(jnp.float32)
    return hi, mid, lo


def _matmul_kernel(x_ref, wt_ref, s_ref, b_ref, o_ref, *, act):
    acc = lax.dot_general(x_ref[...], wt_ref[...], _NT_DIMS, preferred_element_type=jnp.float32)
    acc = acc * s_ref[...] + b_ref[...]
    if act == "silu":
        acc = acc * jax.nn.sigmoid(acc)
    elif act == "sigmoid":
        acc = jax.nn.sigmoid(acc)
    o_ref[...] = acc.astype(o_ref.dtype)


def _matmul(x, w_t, col_scale, col_bias, out_dtype, act, w_row_block=lambda j: j, w_row_start=None,
            tm=1024, tn=1024):
    M, K = x.shape
    N = col_scale.shape[1]
    tm, tn = min(tm, M), min(tn, N)
    if w_row_start is None:
        w_spec = pl.BlockSpec((tn, K), lambda i, j: (w_row_block(j), 0))
    else:
        align = math.gcd(w_row_start, tn)
        w_spec = pl.BlockSpec((pl.Element(tn), pl.Element(K)),
                              lambda i, j: (pl.multiple_of(w_row_start + j * tn, align), 0))
    return pl.pallas_call(
        functools.partial(_matmul_kernel, act=act),
        grid=(M // tm, N // tn),
        in_specs=[pl.BlockSpec((tm, K), lambda i, j: (i, 0)),
                  w_spec,
                  pl.BlockSpec((1, tn), lambda i, j: (0, j)),
                  pl.BlockSpec((1, tn), lambda i, j: (0, j))],
        out_specs=pl.BlockSpec((tm, tn), lambda i, j: (i, j)),
        out_shape=jax.ShapeDtypeStruct((M, N), out_dtype),
        compiler_params=_params("parallel", "parallel"),
        name="proj_" + act,
    )(x, w_t, col_scale, col_bias)


def _forget_bias_kernel(x_ref, w_ref, b_ref, xb_ref, o_ref, carry_ref, *, heads):
    @pl.when(pl.program_id(1) == 0)
    def _():
        carry_ref[...] = jnp.zeros_like(carry_ref)

    ts = x_ref.shape[0]
    xb = x_ref[...].astype(xb_ref.dtype)
    xb_ref[...] = xb
    z = lax.dot_general(xb, w_ref[...], _NT_DIMS, preferred_element_type=jnp.float32) + b_ref[...]
    log_f = jnp.minimum(z, 0.0) - jnp.log1p(jnp.exp(-jnp.abs(z)))
    row = lax.broadcasted_iota(jnp.int32, (ts, ts), 0)
    col = lax.broadcasted_iota(jnp.int32, (ts, ts), 1)
    lower = jnp.where(col <= row, 1.0, 0.0).astype(jnp.bfloat16)
    c = carry_ref[...]
    for part in _split3_bf16(log_f):
        c = c + jnp.dot(lower, part.astype(jnp.bfloat16), preferred_element_type=jnp.float32)
    carry_ref[...] = c[ts - 1:ts, :]
    hi, mid, lo = _split3_bf16(c * (-LOG2E))
    lane = lax.broadcasted_iota(jnp.int32, (ts, LANES), 1)
    cols = jnp.where(lane < heads, hi,
                     jnp.where(lane < 2 * heads, pltpu.roll(mid, heads, 1),
                               jnp.where(lane < 3 * heads, pltpu.roll(lo, 2 * heads, 1), 0.0)))
    o_ref[0] = cols.astype(o_ref.dtype)


def _forget_bias_columns(x, w_f, b_forget, heads, batch, ts=512):
    M, D = x.shape
    S = M // batch
    assert 3 * heads <= LANES
    ts = min(ts, S)
    steps = S // ts
    return pl.pallas_call(
        functools.partial(_forget_bias_kernel, heads=heads),
        grid=(batch, steps),
        in_specs=[pl.BlockSpec((ts, D), lambda b, i: (b * steps + i, 0)),
                  pl.BlockSpec((LANES, D), lambda b, i: (0, 0)),
                  pl.BlockSpec((1, LANES), lambda b, i: (0, 0))],
        out_specs=[pl.BlockSpec((ts, D), lambda b, i: (b * steps + i, 0)),
                   pl.BlockSpec((1, ts, LANES), lambda b, i: (b, i, 0))],
        out_shape=[jax.ShapeDtypeStruct((M, D), jnp.bfloat16),
                   jax.ShapeDtypeStruct((batch, S, LANES), jnp.bfloat16)],
        scratch_shapes=[pltpu.VMEM((1, LANES), jnp.float32)],
        compiler_params=_params("parallel", "arbitrary"),
        name="forget_bias_columns",
    )(x, w_f, b_forget)


ATTN_HEADS_PER_STEP = 4


def _softmax_tile(s, v_ext, m_ref, acc_ref):
    tk = s.shape[1]
    chunks = [s[:, c:c + LANES] for c in range(0, tk, LANES)]
    mx = functools.reduce(jnp.maximum, chunks)
    m_prev = m_ref[...]
    m_new = jnp.maximum(m_prev, jnp.max(mx, axis=-1, keepdims=True))
    alpha = jnp.exp2(m_prev - m_new)
    p = jnp.concatenate([jnp.exp2(ch - m_new) for ch in chunks], axis=1).astype(v_ext.dtype)
    pv = jnp.dot(p, v_ext, preferred_element_type=jnp.float32)
    acc_ref[...] = acc_ref[...] * jnp.concatenate([alpha, alpha], axis=1) + pv
    m_ref[...] = m_new


class _Part(NamedTuple):
    row0: int
    rows: int
    width: int
    post: Callable[[Any], Any]


class _Tile(NamedTuple):
    qk: Callable[[], Sequence[Any]]
    parts: Sequence[_Part]
    values: Callable[[], Any]


class _HeadTiles(NamedTuple):
    specials: Sequence[_Tile]
    qk: Callable[[Any], Any]
    values: Callable[[Any], Any]
    s_refs: Sequence[Any]
    m_ref: Any
    acc_ref: Any


def _pipelined_tiles(heads, n_rest):
    def fill(buf, logits_of):
        for hd in heads:
            s = logits_of(hd)
            hd.s_refs[buf][:, :s.shape[1]] = s

    def consume(buf, values_of):
        for hd in heads:
            v_ext = values_of(hd)
            _softmax_tile(hd.s_refs[buf][:, :v_ext.shape[0]], v_ext, hd.m_ref, hd.acc_ref)

    def fill_special(buf, t):
        for hd in heads:
            tile = hd.specials[t]
            for part, s in zip(tile.parts, tile.qk()):
                hd.s_refs[buf][part.row0:part.row0 + part.rows, :part.width] = s

    def consume_special(buf, t):
        for hd in heads:
            tile = hd.specials[t]
            v_ext = tile.values()
            for part in tile.parts:
                rows = pl.ds(part.row0, part.rows)
                s = hd.s_refs[buf][part.row0:part.row0 + part.rows, :part.width]
                _softmax_tile(part.post(s), v_ext[:part.width], hd.m_ref.at[rows], hd.acc_ref.at[rows])

    n_special = len(heads[0].specials)
    no_uniform_tiles = isinstance(n_rest, int) and n_rest == 0
    fill_special(0, 0)
    for t in range(n_special):
        if t + 1 < n_special:
            fill_special((t + 1) % 2, t + 1)
        elif not no_uniform_tiles:
            fill((t + 1) % 2, lambda hd: hd.qk(0))
        consume_special(t % 2, t)
    if no_uniform_tiles:
        return
    cur, nxt = n_special % 2, (n_special + 1) % 2

    def two_tiles(i, carry):
        fill(nxt, lambda hd: hd.qk(2 * i + 1))
        consume(cur, lambda hd: hd.values(2 * i))
        fill(cur, lambda hd: hd.qk(jnp.minimum(2 * i + 2, n_rest - 1)))
        consume(nxt, lambda hd: hd.values(2 * i + 1))
        return carry

    lax.fori_loop(0, n_rest // 2, two_tiles, 0)

    @pl.when(n_rest % 2 == 1)
    def _():
        consume(cur, lambda hd: hd.values(n_rest - 1))


def _diagonal_tile(q_aug, key_tile, values, bias=None, lead=0):
    tq = q_aug.shape[0]
    half = tq // 2
    shift = lead * tq

    def masked(row0, rows, width):
        def post(s):
            row = lax.broadcasted_iota(jnp.int32, (rows, width), 0) + (row0 + shift)
            col = lax.broadcasted_iota(jnp.int32, (rows, width), 1)
            if bias is not None:
                s = s + bias(row0, rows, width)
            return jnp.where(col <= row, s, NEG_BIG)
        return _Part(row0, rows, width, post)

    def qk():
        k = key_tile()
        return [lax.dot_general(q_aug[:half], k[:shift + half], _NT_DIMS, preferred_element_type=jnp.float32),
                lax.dot_general(q_aug[half:], k, _NT_DIMS, preferred_element_type=jnp.float32)]

    return _Tile(qk, [masked(0, half, shift + half), masked(half, tq - half, shift + tq)], values)


def _init_softmax_state(m_ref, acc_ref):
    m_ref[...] = jnp.full_like(m_ref, NEG_BIG)
    acc_ref[...] = jnp.zeros_like(acc_ref)


def _finish_attention(acc_ref, z, dtype):
    dh = z.shape[1]
    return (acc_ref[:, :dh] / acc_ref[:, dh:] * z).astype(dtype)


def _fox_kernel(q_ref, k_ref, cx_ref, v_ref, z_ref, o_ref, kaug_ref, vext_ref, sa_ref, sb_ref, m_ref, acc_ref,
                *, heads):
    qi = pl.program_id(2)
    tq = q_ref.shape[1]
    dh = HEAD_DIM
    G = kaug_ref.shape[0]

    def rows(j):
        return pl.ds(pl.multiple_of(j * tq, tq), tq)

    for g in range(G):
        kaug_ref[g, rows(qi), :dh] = k_ref[0, :, g * dh:(g + 1) * dh]
        kaug_ref[g, rows(qi), dh:] = cx_ref[0]
        vext_ref[g, rows(qi), :dh] = v_ref[0, :, g * dh:(g + 1) * dh]
        vext_ref[g, rows(qi), dh:] = jnp.ones((tq, dh), vext_ref.dtype)

    lane = lax.broadcasted_iota(jnp.int32, (1, LANES), 1)
    tiles = []
    for g in range(G):
        h = pl.program_id(1) * G + g
        pick = (lane == h) | (lane == heads + h) | (lane == 2 * heads + h)
        ones_at_head_terms = jnp.broadcast_to(jnp.where(pick, 1.0, 0.0), (tq, LANES)).astype(q_ref.dtype)
        q_aug = jnp.concatenate([q_ref[0, :, g * dh:(g + 1) * dh], ones_at_head_terms], axis=1)
        _init_softmax_state(m_ref.at[g], acc_ref.at[g])

        def wide_rows(f):
            return pl.ds(pl.multiple_of(2 * f * tq, tq), 2 * tq)

        def logits(f, g=g, q_aug=q_aug):
            return lax.dot_general(q_aug, kaug_ref[g, wide_rows(f), :], _NT_DIMS,
                                   preferred_element_type=jnp.float32)

        def values(f, g=g):
            return vext_ref[g, wide_rows(f), :]

        def last_rows(n_tiles):
            return pl.ds(pl.multiple_of((qi + 1 - n_tiles) * tq, tq), n_tiles * tq)

        alone = _diagonal_tile(q_aug, lambda g=g: kaug_ref[g, last_rows(1), :],
                               lambda g=g: vext_ref[g, last_rows(1), :])
        paired = _diagonal_tile(q_aug, lambda g=g: kaug_ref[g, last_rows(2), :],
                                lambda g=g: vext_ref[g, last_rows(2), :], lead=1)
        tiles.append((alone, paired, logits, values, (sa_ref.at[g], sb_ref.at[g]), m_ref.at[g], acc_ref.at[g]))

    @pl.when(qi == 0)
    def _():
        _pipelined_tiles([_HeadTiles([t[0]], *t[2:]) for t in tiles], 0)

    @pl.when(qi == 1)
    def _():
        _pipelined_tiles([_HeadTiles([t[1]], *t[2:]) for t in tiles], 0)

    @pl.when((qi >= 2) & (qi % 2 == 0))
    def _():
        _pipelined_tiles([_HeadTiles([t[0]], *t[2:]) for t in tiles], qi // 2)

    @pl.when((qi >= 3) & (qi % 2 == 1))
    def _():
        _pipelined_tiles([_HeadTiles([t[1]], *t[2:]) for t in tiles], (qi - 1) // 2)

    for g in range(G):
        o_ref[0, :, g * dh:(g + 1) * dh] = _finish_attention(acc_ref.at[g], z_ref[0, :, g * dh:(g + 1) * dh],
                                                             o_ref.dtype)


def _fox_attention(qkv, cx, sz, H, col0, zcol0, tq=512):
    B, S, _ = qkv.shape
    tq = min(tq, S)
    Dh = HEAD_DIM
    G = ATTN_HEADS_PER_STEP
    assert H % G == 0 and col0 % G == 0 and zcol0 % G == 0
    W = G * Dh
    return pl.pallas_call(
        functools.partial(_fox_kernel, heads=H),
        grid=(B, H // G, S // tq),
        in_specs=[pl.BlockSpec((1, tq, W), lambda b, h, i: (b, i, col0 // G + h)),
                  pl.BlockSpec((1, tq, W), lambda b, h, i: (b, i, (col0 + H) // G + h)),
                  pl.BlockSpec((1, tq, LANES), lambda b, h, i: (b, i, 0)),
                  pl.BlockSpec((1, tq, W), lambda b, h, i: (b, i, (col0 + 2 * H) // G + h)),
                  pl.BlockSpec((1, tq, W), lambda b, h, i: (b, i, zcol0 // G + h))],
        out_specs=pl.BlockSpec((1, tq, W), lambda b, h, i: (b, i, h)),
        out_shape=jax.ShapeDtypeStruct((B, S, H * Dh), jnp.bfloat16),
        scratch_shapes=[pltpu.VMEM((G, S, Dh + LANES), jnp.bfloat16), pltpu.VMEM((G, S, 2 * Dh), jnp.bfloat16),
                        pltpu.VMEM((G, tq, 2 * tq), jnp.float32), pltpu.VMEM((G, tq, 2 * tq), jnp.float32),
                        pltpu.VMEM((G, tq, LANES), jnp.float32), pltpu.VMEM((G, tq, 2 * Dh), jnp.float32)],
        compiler_params=_params("arbitrary", "arbitrary", "arbitrary"),
        name="fox_attention",
    )(qkv, qkv, cx, qkv, sz)


def _t5_bucket_np(dist):
    max_exact = T5_NUM_BUCKETS // 2
    d = np.maximum(dist, 1).astype(np.float32)
    ratio = (np.log(d / np.float32(max_exact)) / np.float32(math.log(T5_MAX_DISTANCE / max_exact))
             * np.float32(T5_NUM_BUCKETS - max_exact))
    large = np.minimum(max_exact + ratio.astype(np.int32), T5_NUM_BUCKETS - 1)
    return np.where(dist < max_exact, dist, large).astype(np.int32)


MOBA_Q_BLOCKS = 2


def _moba_query_columns(q, kmean, far_row, first_block, blk, nb):
    nbp, tq = kmean.shape[0], q.shape[0]
    km_hi = kmean.astype(jnp.bfloat16)
    km_lo = (kmean - km_hi.astype(jnp.float32)).astype(jnp.bfloat16)
    gate = (lax.dot_general(km_hi, q, _NT_DIMS, preferred_element_type=jnp.float32)
            + lax.dot_general(km_lo, q, _NT_DIMS, preferred_element_type=jnp.float32))
    n_idx = lax.broadcasted_iota(jnp.int32, (nbp, tq), 0)
    own = first_block + lax.broadcasted_iota(jnp.int32, (nbp, tq), 1) // blk
    past = n_idx < own
    gate = jnp.where(past, gate, NEG_BIG)
    beaten = jnp.zeros((nbp, tq), jnp.int32)
    for n in range(nb):
        g_n = gate[n:n + 1, :]
        lower_index_wins_ties = jnp.where(n_idx > n, 1, 0)
        beaten = beaten + jnp.where(g_n > gate, 1, 0) + jnp.where(g_n == gate, lower_index_wins_ties, 0)
    selected = past & (beaten < MOBA_TOPK)
    far_bias = jnp.broadcast_to(far_row, (nbp, tq))
    term = jnp.where(selected, jnp.where(n_idx == own - 1, 0.0, far_bias), NEG_BIG)
    term = jnp.where(n_idx == own, 0.0, term)
    term_hi = term.astype(jnp.bfloat16).astype(jnp.float32)
    term_lo = jnp.where(selected, term - term_hi, 0.0)
    pieces = [term_hi[:nb], term_lo[:nb]]
    if 2 * nb < LANES:
        pieces.append(jnp.zeros((LANES - 2 * nb, tq), jnp.float32))
    return jnp.concatenate(pieces, axis=0).T.astype(q.dtype)


def _moba_kernel(q_ref, k_ref, v_ref, rbias_ref, far_ref, z_ref, o_ref,
                 kaug_ref, vext_ref, kmean_ref, t5_ref, sa_ref, sb_ref, m_ref, acc_ref, *, nb, blk):
    qi = pl.program_id(2)
    tq = q_ref.shape[1]
    dh = HEAD_DIM
    G = kaug_ref.shape[0]
    first_block = qi * (tq // blk)

    def rows(j):
        return pl.ds(pl.multiple_of(j * tq, tq), tq)

    @pl.when(qi == 0)
    def _():
        for g in range(G):
            kmean_ref[g] = jnp.zeros(kmean_ref.shape[1:], kmean_ref.dtype)
            rb = jnp.broadcast_to(rbias_ref[g], (tq, tq))
            t5_ref[g] = pltpu.roll(rb, 1, 1, stride=1, stride_axis=0)

    key_block = first_block + lax.broadcasted_iota(jnp.int32, (tq, LANES), 0) // blk
    lane = lax.broadcasted_iota(jnp.int32, (1, LANES), 1)
    lane_block = jnp.where(lane < nb, lane, jnp.where(lane < 2 * nb, lane - nb, -1))
    hot = jnp.where(jnp.broadcast_to(lane_block, (tq, LANES)) == key_block, 1.0, 0.0).astype(kaug_ref.dtype)
    for g in range(G):
        k = k_ref[0, :, g * dh:(g + 1) * dh]
        kaug_ref[g, rows(qi), :dh] = k
        kaug_ref[g, rows(qi), dh:] = hot
        vext_ref[g, rows(qi), :dh] = v_ref[0, :, g * dh:(g + 1) * dh]
        vext_ref[g, rows(qi), dh:] = jnp.ones((tq, dh), vext_ref.dtype)
        for n in range(tq // blk):
            kb = k[n * blk:(n + 1) * blk, :].astype(jnp.float32)
            kmean_ref[g, pl.ds(first_block + n, 1), :] = jnp.sum(kb, axis=0, keepdims=True) * (1.0 / blk)

    def add_previous_block_bias(s, g):
        return jnp.concatenate([s[:, :tq - blk], s[:, tq - blk:] + t5_ref[g, tq - blk:, :blk]], axis=1)

    tiles = []
    for g in range(G):
        q = q_ref[0, :, g * dh:(g + 1) * dh]
        q_extra = _moba_query_columns(q, kmean_ref[g], far_ref[g], first_block, blk, nb)
        q_aug = jnp.concatenate([q, q_extra], axis=1)
        _init_softmax_state(m_ref.at[g], acc_ref.at[g])

        def logits(j, g=g, q_aug=q_aug):
            return lax.dot_general(q_aug, kaug_ref[g, rows(j), :], _NT_DIMS, preferred_element_type=jnp.float32)

        def values(j, g=g):
            return vext_ref[g, rows(j), :]

        diagonal = _diagonal_tile(q_aug, lambda g=g: kaug_ref[g, rows(qi), :], functools.partial(values, qi),
                                  bias=lambda row0, n, width, g=g: t5_ref[g, row0:row0 + n, :width])

        def previous_logits(logits=logits):
            s_prev = logits(qi - 1)
            return [s_prev[:blk], s_prev[blk:]]

        previous = _Tile(previous_logits,
                         [_Part(0, blk, tq, functools.partial(add_previous_block_bias, g=g)),
                          _Part(blk, tq - blk, tq, lambda s_rows: s_rows)],
                         functools.partial(values, qi - 1))
        tiles.append((diagonal, previous, logits, values, (sa_ref.at[g], sb_ref.at[g]), m_ref.at[g], acc_ref.at[g]))

    @pl.when(qi == 0)
    def _():
        _pipelined_tiles([_HeadTiles([t[0]], *t[2:]) for t in tiles], 0)

    @pl.when(qi >= 1)
    def _():
        _pipelined_tiles([_HeadTiles([t[0], t[1]], *t[2:]) for t in tiles], qi - 1)

    for g in range(G):
        o_ref[0, :, g * dh:(g + 1) * dh] = _finish_attention(acc_ref.at[g], z_ref[0, :, g * dh:(g + 1) * dh],
                                                             o_ref.dtype)


def _moba_attention(qkv, rel_bias_table, sz, H, col0, zcol0):
    B, S, _ = qkv.shape
    blk = MOBA_BLOCK
    tq = MOBA_Q_BLOCKS * blk
    assert MOBA_Q_BLOCKS == 2 and S % tq == 0
    nb = S // blk
    nbp = -(-nb // 8) * 8
    assert 2 * nb <= LANES
    Dh = HEAD_DIM
    G = ATTN_HEADS_PER_STEP
    assert H % G == 0 and col0 % G == 0 and zcol0 % G == 0
    W = G * Dh
    table_t = rel_bias_table.T.astype(jnp.float32) * LOG2E
    rbias = table_t[:, _t5_bucket_np(np.arange(tq)[::-1])][:, None, :]
    far_bucket = int(_t5_bucket_np(np.array([blk + 1]))[0])
    assert (_t5_bucket_np(np.arange(blk + 1, S + blk)) == far_bucket).all()
    far = jnp.broadcast_to(table_t[:, far_bucket][:, None, None], (H, 1, tq))
    return pl.pallas_call(
        functools.partial(_moba_kernel, nb=nb, blk=blk),
        grid=(B, H // G, S // tq),
        in_specs=[pl.BlockSpec((1, tq, W), lambda b, h, i: (b, i, col0 // G + h)),
                  pl.BlockSpec((1, tq, W), lambda b, h, i: (b, i, (col0 + H) // G + h)),
                  pl.BlockSpec((1, tq, W), lambda b, h, i: (b, i, (col0 + 2 * H) // G + h)),
                  pl.BlockSpec((G, 1, tq), lambda b, h, i: (h, 0, 0)),
                  pl.BlockSpec((G, 1, tq), lambda b, h, i: (h, 0, 0)),
                  pl.BlockSpec((1, tq, W), lambda b, h, i: (b, i, zcol0 // G + h))],
        out_specs=pl.BlockSpec((1, tq, W), lambda b, h, i: (b, i, h)),
        out_shape=jax.ShapeDtypeStruct((B, S, H * Dh), jnp.bfloat16),
        scratch_shapes=[pltpu.VMEM((G, S, Dh + LANES), jnp.bfloat16), pltpu.VMEM((G, S, 2 * Dh), jnp.bfloat16),
                        pltpu.VMEM((G, nbp, Dh), jnp.float32), pltpu.VMEM((G, tq, tq), jnp.float32),
                        pltpu.VMEM((G, tq, tq), jnp.float32), pltpu.VMEM((G, tq, tq), jnp.float32),
                        pltpu.VMEM((G, tq, LANES), jnp.float32), pltpu.VMEM((G, tq, 2 * Dh), jnp.float32)],
        compiler_params=_params("arbitrary", "arbitrary", "arbitrary"),
        name="moba_attention",
    )(qkv, qkv, qkv, rbias, far, sz)


def _branch_merge_kernel(ya_ref, yf_ref, wa_ref, wf_ref, ga_ref, gf_ref, o_ref):
    ua = jnp.dot(ya_ref[...], wa_ref[0], preferred_element_type=jnp.float32)
    uf = jnp.dot(yf_ref[...], wf_ref[0], preferred_element_type=jnp.float32)
    o_ref[...] = (ga_ref[...] * ua + gf_ref[...] * uf).astype(o_ref.dtype)


def _branch_merge(ya, yf, w, gates, tm=1024, tn=512):
    M, W = ya.shape
    D = w.shape[2]
    tm, tn = min(tm, M), min(tn, D)
    nj = D // tn
    return pl.pallas_call(
        _branch_merge_kernel,
        grid=(M // tm, nj),
        in_specs=[pl.BlockSpec((tm, W), lambda i, j: (i, 0)),
                  pl.BlockSpec((tm, W), lambda i, j: (i, 0)),
                  pl.BlockSpec((1, W, tn), lambda i, j: (0, 0, j)),
                  pl.BlockSpec((1, W, tn), lambda i, j: (1, 0, j)),
                  pl.BlockSpec((tm, tn), lambda i, j: (i, j)),
                  pl.BlockSpec((tm, tn), lambda i, j: (i, nj + j))],
        out_specs=pl.BlockSpec((tm, tn), lambda i, j: (i, j)),
        out_shape=jax.ShapeDtypeStruct((M, D), jnp.bfloat16),
        compiler_params=_params("parallel", "parallel"),
        name="branch_merge",
    )(ya, yf, w, w, gates, gates)


LN_ROW_CHUNK = 32


def _out_layernorm_kernel(h_ref, w_ref, x_ref, g_ref, b_ref, o_ref, pre_even_ref, pre_odd_ref):
    i, j = pl.program_id(0), pl.program_id(1)
    tn = w_ref.shape[1]
    chunk = o_ref.shape[0]

    @pl.when((i == 0) & (j == 0))
    def _():
        pre_odd_ref[...] = jnp.zeros_like(pre_odd_ref)

    def step(cur_ref, prev_ref):
        rows = min(LN_ROW_CHUNK, chunk)
        for c in range(chunk // rows):
            r = prev_ref[pl.ds(pl.multiple_of(j * chunk + c * rows, rows), rows), :]
            mu = jnp.mean(r, axis=-1, keepdims=True)
            d = r - mu
            var = jnp.mean(d * d, axis=-1, keepdims=True)
            o_ref[c * rows:(c + 1) * rows, :] = d * lax.rsqrt(var + LN_EPS) * g_ref[...] + b_ref[...]
        out = jnp.dot(h_ref[...], w_ref[...], preferred_element_type=jnp.float32)
        cur_ref[:, pl.ds(pl.multiple_of(j * tn, tn), tn)] = DEEPNORM_ALPHA * x_ref[...] + out

    @pl.when(i % 2 == 0)
    def _():
        step(pre_even_ref, pre_odd_ref)

    @pl.when(i % 2 == 1)
    def _():
        step(pre_odd_ref, pre_even_ref)


def _out_layernorm(h, w, x, gain, bias, tm=512, tn=1024):
    M, D = x.shape
    tm, tn = min(tm, M), min(tn, D)
    n_i, n_j = M // tm, D // tn
    chunk = tm // n_j
    assert chunk % 8 == 0
    return pl.pallas_call(
        _out_layernorm_kernel,
        grid=(n_i + 1, n_j),
        in_specs=[pl.BlockSpec((tm, D), lambda i, j: (jnp.minimum(i, n_i - 1), 0)),
                  pl.BlockSpec((D, tn), lambda i, j: (0, j)),
                  pl.BlockSpec((tm, tn), lambda i, j: (jnp.minimum(i, n_i - 1), j)),
                  pl.BlockSpec((1, D), lambda i, j: (0, 0)),
                  pl.BlockSpec((1, D), lambda i, j: (0, 0))],
        out_specs=pl.BlockSpec((chunk, D), lambda i, j: (jnp.where(i == 0, 0, (i - 1) * n_j + j), 0)),
        out_shape=jax.ShapeDtypeStruct((M, D), jnp.float32),
        scratch_shapes=[pltpu.VMEM((tm, D), jnp.float32), pltpu.VMEM((tm, D), jnp.float32)],
        compiler_params=_params("arbitrary", "arbitrary"),
        name="out_layernorm",
    )(h, w, x, gain, bias)


def kernel(x, w_in, b_forget, b_gate, rel_bias_table, w_branch, w_out, ln_gain, ln_bias):
    B, S, D = x.shape
    Ha, Hf = MOBA_HEADS, FOX_HEADS
    Wa, Wf = Ha * HEAD_DIM, Hf * HEAD_DIM
    bf16, f32 = jnp.bfloat16, jnp.float32
    depth = w_in.shape[0]
    for layer in range(depth):
        w_t = jnp.swapaxes(w_in[layer], 0, 1).astype(bf16)
        n_attn = 4 * Wa + 4 * Wf
        w_f = jnp.pad(w_t[n_attn:n_attn + Hf], ((0, LANES - Hf), (0, 0)))
        b_f = jnp.pad(b_forget[layer].astype(f32), (0, LANES - Hf))[None, :]
        x2 = x.reshape(B * S, D).astype(f32)
        xb, cx = _forget_bias_columns(x2, w_f, b_f, Hf, B)

        scale = HEAD_DIM ** -0.5 * LOG2E
        qkv_scale = jnp.concatenate([jnp.full((Wa,), scale, f32), jnp.ones((2 * Wa,), f32),
                                     jnp.full((Wf,), scale, f32), jnp.ones((2 * Wf,), f32)])[None, :]
        tn = PROJ_TN
        assert Wa % tn == 0 and Wf % tn == 0
        a_blocks, f_blocks = Wa // tn, Wf // tn
        qkv = _matmul(xb, w_t, qkv_scale, jnp.zeros_like(qkv_scale), bf16, "none",
                      lambda j: jnp.where(j < 3 * a_blocks, j, j + a_blocks), tn=tn).reshape(B, S, -1)
        ones_z = jnp.ones((1, Wa + Wf), f32)
        sz = _matmul(xb, w_t, ones_z, jnp.zeros_like(ones_z), f32, "silu",
                     lambda j: jnp.where(j < a_blocks, j + 3 * a_blocks, j + 3 * a_blocks + 3 * f_blocks),
                     tn=tn).reshape(B, S, -1)
        gates = _matmul(xb, w_t, jnp.ones((1, 2 * D), f32), b_gate[layer].reshape(1, 2 * D).astype(f32),
                        f32, "sigmoid", w_row_start=n_attn + Hf, tn=tn)

        ya = _moba_attention(qkv, rel_bias_table, sz, Ha, 0, 0)
        yf = _fox_attention(qkv, cx, sz, Hf, 3 * Ha, Ha)

        assert Wa == Wf
        merged = _branch_merge(ya.reshape(B * S, Wa), yf.reshape(B * S, Wf), w_branch[layer].astype(bf16), gates)
        y = _out_layernorm(merged, w_out[layer].astype(bf16), x2,
                           ln_gain[layer].reshape(1, D).astype(f32), ln_bias[layer].reshape(1, D).astype(f32))
        x = y.reshape(B, S, D).astype(x.dtype)
    return x
```

```python
import functools
import math
from typing import Any, Callable, NamedTuple, Sequence

import jax
import jax.numpy as jnp
import numpy as np
from jax import lax
from jax.experimental import pallas as pl
from jax.experimental.pallas import tpu as pltpu

HEAD_DIM = 128
LANES = 128
MOBA_HEADS = 16
FOX_HEADS = 16
MOBA_BLOCK = 256
MOBA_TOPK = 3
T5_NUM_BUCKETS = 32
T5_MAX_DISTANCE = 128
LN_EPS = 1e-5
DEPTH = 1
DEEPNORM_ALPHA = (2.0 * DEPTH) ** 0.25
LOG2E = math.log2(math.e)

V7X_VMEM_LIMIT_BYTES = 56 * 1024 * 1024
PROJ_TN = 1024
NEG_BIG = -1e30

_NT_DIMS = (((1,), (1,)), ((), ()))


def _params(*semantics):
    return pltpu.CompilerParams(dimension_semantics=semantics,
                                vmem_limit_bytes=V7X_VMEM_LIMIT_BYTES)


def _split3_bf16(a):
    hi = a.astype(jnp.bfloat16).astype(jnp.float32)
    r = a - hi
    mid = r.astype(jnp.bfloat16).astype(jnp.float32)
    lo = (r - mid).astype(jnp.bfloat16).astype(jnp.float32)
    return hi, mid, lo


def _matmul_kernel(x_ref, wt_ref, s_ref, b_ref, o_ref, *, act):
    acc = lax.dot_general(x_ref[...], wt_ref[...], _NT_DIMS, preferred_element_type=jnp.float32)
    acc = acc * s_ref[...] + b_ref[...]
    if act == "silu":
        acc = acc * jax.nn.sigmoid(acc)
    elif act == "sigmoid":
        acc = jax.nn.sigmoid(acc)
    o_ref[...] = acc.astype(o_ref.dtype)


def _matmul(x, w_t, col_scale, col_bias, out_dtype, act, w_row_block=lambda j: j, w_row_start=None,
            tm=1024, tn=1024):
    M, K = x.shape
    N = col_scale.shape[1]
    tm, tn = min(tm, M), min(tn, N)
    if w_row_start is None:
        w_spec = pl.BlockSpec((tn, K), lambda i, j: (w_row_block(j), 0))
    else:
        align = math.gcd(w_row_start, tn)
        w_spec = pl.BlockSpec((pl.Element(tn), pl.Element(K)),
                              lambda i, j: (pl.multiple_of(w_row_start + j * tn, align), 0))
    return pl.pallas_call(
        functools.partial(_matmul_kernel, act=act),
        grid=(M // tm, N // tn),
        in_specs=[pl.BlockSpec((tm, K), lambda i, j: (i, 0)),
                  w_spec,
                  pl.BlockSpec((1, tn), lambda i, j: (0, j)),
                  pl.BlockSpec((1, tn), lambda i, j: (0, j))],
        out_specs=pl.BlockSpec((tm, tn), lambda i, j: (i, j)),
        out_shape=jax.ShapeDtypeStruct((M, N), out_dtype),
        compiler_params=_params("parallel", "parallel"),
        name="proj_" + act,
    )(x, w_t, col_scale, col_bias)


def _forget_bias_kernel(x_ref, w_ref, b_ref, xb_ref, o_ref, carry_ref, *, heads):
    @pl.when(pl.program_id(1) == 0)
    def _():
        carry_ref[...] = jnp.zeros_like(carry_ref)

    ts = x_ref.shape[0]
    xb = x_ref[...].astype(xb_ref.dtype)
    xb_ref[...] = xb
    z = lax.dot_general(xb, w_ref[...], _NT_DIMS, preferred_element_type=jnp.float32) + b_ref[...]
    log_f = jnp.minimum(z, 0.0) - jnp.log1p(jnp.exp(-jnp.abs(z)))
    row = lax.broadcasted_iota(jnp.int32, (ts, ts), 0)
    col = lax.broadcasted_iota(jnp.int32, (ts, ts), 1)
    lower = jnp.where(col <= row, 1.0, 0.0).astype(jnp.bfloat16)
    c = carry_ref[...]
    for part in _split3_bf16(log_f):
        c = c + jnp.dot(lower, part.astype(jnp.bfloat16), preferred_element_type=jnp.float32)
    carry_ref[...] = c[ts - 1:ts, :]
    hi, mid, lo = _split3_bf16(c * (-LOG2E))
    lane = lax.broadcasted_iota(jnp.int32, (ts, LANES), 1)
    cols = jnp.where(lane < heads, hi,
                     jnp.where(lane < 2 * heads, pltpu.roll(mid, heads, 1),
                               jnp.where(lane < 3 * heads, pltpu.roll(lo, 2 * heads, 1), 0.0)))
    o_ref[0] = cols.astype(o_ref.dtype)


def _forget_bias_columns(x, w_f, b_forget, heads, batch, ts=512):
    M, D = x.shape
    S = M // batch
    assert 3 * heads <= LANES
    ts = min(ts, S)
    steps = S // ts
    return pl.pallas_call(
        functools.partial(_forget_bias_kernel, heads=heads),
        grid=(batch, steps),
        in_specs=[pl.BlockSpec((ts, D), lambda b, i: (b * steps + i, 0)),
                  pl.BlockSpec((LANES, D), lambda b, i: (0, 0)),
                  pl.BlockSpec((1, LANES), lambda b, i: (0, 0))],
        out_specs=[pl.BlockSpec((ts, D), lambda b, i: (b * steps + i, 0)),
                   pl.BlockSpec((1, ts, LANES), lambda b, i: (b, i, 0))],
        out_shape=[jax.ShapeDtypeStruct((M, D), jnp.bfloat16),
                   jax.ShapeDtypeStruct((batch, S, LANES), jnp.bfloat16)],
        scratch_shapes=[pltpu.VMEM((1, LANES), jnp.float32)],
        compiler_params=_params("parallel", "arbitrary"),
        name="forget_bias_columns",
    )(x, w_f, b_forget)


ATTN_HEADS_PER_STEP = 4


def _softmax_tile(s, v_ext, m_ref, acc_ref):
    tk = s.shape[1]
    chunks = [s[:, c:c + LANES] for c in range(0, tk, LANES)]
    mx = functools.reduce(jnp.maximum, chunks)
    m_prev = m_ref[...]
    m_new = jnp.maximum(m_prev, jnp.max(mx, axis=-1, keepdims=True))
    alpha = jnp.exp2(m_prev - m_new)
    p = jnp.concatenate([jnp.exp2(ch - m_new) for ch in chunks], axis=1).astype(v_ext.dtype)
    pv = jnp.dot(p, v_ext, preferred_element_type=jnp.float32)
    acc_ref[...] = acc_ref[...] * jnp.concatenate([alpha, alpha], axis=1) + pv
    m_ref[...] = m_new


class _Part(NamedTuple):
    row0: int
    rows: int
    width: int
    post: Callable[[Any], Any]


class _Tile(NamedTuple):
    qk: Callable[[], Sequence[Any]]
    parts: Sequence[_Part]
    values: Callable[[], Any]


class _HeadTiles(NamedTuple):
    specials: Sequence[_Tile]
    qk: Callable[[Any], Any]
    values: Callable[[Any], Any]
    s_refs: Sequence[Any]
    m_ref: Any
    acc_ref: Any


def _pipelined_tiles(heads, n_rest):
    def fill(buf, logits_of):
        for hd in heads:
            s = logits_of(hd)
            hd.s_refs[buf][:, :s.shape[1]] = s

    def consume(buf, values_of):
        for hd in heads:
            v_ext = values_of(hd)
            _softmax_tile(hd.s_refs[buf][:, :v_ext.shape[0]], v_ext, hd.m_ref, hd.acc_ref)

    def fill_special(buf, t):
        for hd in heads:
            tile = hd.specials[t]
            for part, s in zip(tile.parts, tile.qk()):
                hd.s_refs[buf][part.row0:part.row0 + part.rows, :part.width] = s

    def consume_special(buf, t):
        for hd in heads:
            tile = hd.specials[t]
            v_ext = tile.values()
            for part in tile.parts:
                rows = pl.ds(part.row0, part.rows)
                s = hd.s_refs[buf][part.row0:part.row0 + part.rows, :part.width]
                _softmax_tile(part.post(s), v_ext[:part.width], hd.m_ref.at[rows], hd.acc_ref.at[rows])

    n_special = len(heads[0].specials)
    no_uniform_tiles = isinstance(n_rest, int) and n_rest == 0
    fill_special(0, 0)
    for t in range(n_special):
        if t + 1 < n_special:
            fill_special((t + 1) % 2, t + 1)
        elif not no_uniform_tiles:
            fill((t + 1) % 2, lambda hd: hd.qk(0))
        consume_special(t % 2, t)
    if no_uniform_tiles:
        return
    cur, nxt = n_special % 2, (n_special + 1) % 2

    def two_tiles(i, carry):
        fill(nxt, lambda hd: hd.qk(2 * i + 1))
        consume(cur, lambda hd: hd.values(2 * i))
        fill(cur, lambda hd: hd.qk(jnp.minimum(2 * i + 2, n_rest - 1)))
        consume(nxt, lambda hd: hd.values(2 * i + 1))
        return carry

    lax.fori_loop(0, n_rest // 2, two_tiles, 0)

    @pl.when(n_rest % 2 == 1)
    def _():
        consume(cur, lambda hd: hd.values(n_rest - 1))


def _diagonal_tile(q_aug, key_tile, values, add_bias=None, lead=0):
    tq = q_aug.shape[0]
    half = tq // 2
    shift = lead * tq

    def masked(row0, rows, width):
        def post(s):
            row = lax.broadcasted_iota(jnp.int32, (rows, width), 0) + (row0 + shift)
            col = lax.broadcasted_iota(jnp.int32, (rows, width), 1)
            if add_bias is not None:
                s = add_bias(s, row0)
            return jnp.where(col <= row, s, NEG_BIG)
        return _Part(row0, rows, width, post)

    def qk():
        k = key_tile()
        return [lax.dot_general(q_aug[:half], k[:shift + half], _NT_DIMS, preferred_element_type=jnp.float32),
                lax.dot_general(q_aug[half:], k, _NT_DIMS, preferred_element_type=jnp.float32)]

    return _Tile(qk, [masked(0, half, shift + half), masked(half, tq - half, shift + tq)], values)


def _init_softmax_state(m_ref, acc_ref):
    m_ref[...] = jnp.full_like(m_ref, NEG_BIG)
    acc_ref[...] = jnp.zeros_like(acc_ref)


def _finish_attention(acc_ref, z, dtype):
    dh = z.shape[1]
    return (acc_ref[:, :dh] / acc_ref[:, dh:] * z).astype(dtype)


def _fox_kernel(q_ref, k_ref, cx_ref, v_ref, z_ref, o_ref, kaug_ref, vext_ref, sa_ref, sb_ref, m_ref, acc_ref,
                *, heads):
    qi = pl.program_id(2)
    tq = q_ref.shape[1]
    dh = HEAD_DIM
    G = kaug_ref.shape[0]

    def rows(j):
        return pl.ds(pl.multiple_of(j * tq, tq), tq)

    for g in range(G):
        kaug_ref[g, rows(qi), :dh] = k_ref[0, :, g * dh:(g + 1) * dh]
        kaug_ref[g, rows(qi), dh:] = cx_ref[0]
        vext_ref[g, rows(qi), :dh] = v_ref[0, :, g * dh:(g + 1) * dh]
        vext_ref[g, rows(qi), dh:] = jnp.ones((tq, dh), vext_ref.dtype)

    lane = lax.broadcasted_iota(jnp.int32, (1, LANES), 1)
    tiles = []
    for g in range(G):
        h = pl.program_id(1) * G + g
        pick = (lane == h) | (lane == heads + h) | (lane == 2 * heads + h)
        ones_at_head_terms = jnp.broadcast_to(jnp.where(pick, 1.0, 0.0), (tq, LANES)).astype(q_ref.dtype)
        q_aug = jnp.concatenate([q_ref[0, :, g * dh:(g + 1) * dh], ones_at_head_terms], axis=1)
        _init_softmax_state(m_ref.at[g], acc_ref.at[g])

        def wide_rows(f):
            return pl.ds(pl.multiple_of(2 * f * tq, tq), 2 * tq)

        def logits(f, g=g, q_aug=q_aug):
            return lax.dot_general(q_aug, kaug_ref[g, wide_rows(f), :], _NT_DIMS,
                                   preferred_element_type=jnp.float32)

        def values(f, g=g):
            return vext_ref[g, wide_rows(f), :]

        def last_rows(n_tiles):
            return pl.ds(pl.multiple_of((qi + 1 - n_tiles) * tq, tq), n_tiles * tq)

        alone = _diagonal_tile(q_aug, lambda g=g: kaug_ref[g, last_rows(1), :],
                               lambda g=g: vext_ref[g, last_rows(1), :])
        paired = _diagonal_tile(q_aug, lambda g=g: kaug_ref[g, last_rows(2), :],
                                lambda g=g: vext_ref[g, last_rows(2), :], lead=1)
        tiles.append((alone, paired, logits, values, (sa_ref.at[g], sb_ref.at[g]), m_ref.at[g], acc_ref.at[g]))

    @pl.when(qi == 0)
    def _():
        _pipelined_tiles([_HeadTiles([t[0]], *t[2:]) for t in tiles], 0)

    @pl.when(qi == 1)
    def _():
        _pipelined_tiles([_HeadTiles([t[1]], *t[2:]) for t in tiles], 0)

    @pl.when((qi >= 2) & (qi % 2 == 0))
    def _():
        _pipelined_tiles([_HeadTiles([t[0]], *t[2:]) for t in tiles], qi // 2)

    @pl.when((qi >= 3) & (qi % 2 == 1))
    def _():
        _pipelined_tiles([_HeadTiles([t[1]], *t[2:]) for t in tiles], (qi - 1) // 2)

    for g in range(G):
        o_ref[0, :, g * dh:(g + 1) * dh] = _finish_attention(acc_ref.at[g], z_ref[0, :, g * dh:(g + 1) * dh],
                                                             o_ref.dtype)


def _fox_attention(qkv, cx, sz, H, col0, zcol0, tq=512):
    B, S, _ = qkv.shape
    tq = min(tq, S)
    Dh = HEAD_DIM
    G = ATTN_HEADS_PER_STEP
    assert H % G == 0 and col0 % G == 0 and zcol0 % G == 0
    W = G * Dh
    return pl.pallas_call(
        functools.partial(_fox_kernel, heads=H),
        grid=(B, H // G, S // tq),
        in_specs=[pl.BlockSpec((1, tq, W), lambda b, h, i: (b, i, col0 // G + h)),
                  pl.BlockSpec((1, tq, W), lambda b, h, i: (b, i, (col0 + H) // G + h)),
                  pl.BlockSpec((1, tq, LANES), lambda b, h, i: (b, i, 0)),
                  pl.BlockSpec((1, tq, W), lambda b, h, i: (b, i, (col0 + 2 * H) // G + h)),
                  pl.BlockSpec((1, tq, W), lambda b, h, i: (b, i, zcol0 // G + h))],
        out_specs=pl.BlockSpec((1, tq, W), lambda b, h, i: (b, i, h)),
        out_shape=jax.ShapeDtypeStruct((B, S, H * Dh), jnp.bfloat16),
        scratch_shapes=[pltpu.VMEM((G, S, Dh + LANES), jnp.bfloat16), pltpu.VMEM((G, S, 2 * Dh), jnp.bfloat16),
                        pltpu.VMEM((G, tq, 2 * tq), jnp.float32), pltpu.VMEM((G, tq, 2 * tq), jnp.float32),
                        pltpu.VMEM((G, tq, LANES), jnp.float32), pltpu.VMEM((G, tq, 2 * Dh), jnp.float32)],
        compiler_params=_params("arbitrary", "arbitrary", "arbitrary"),
        name="fox_attention",
    )(qkv, qkv, cx, qkv, sz)


def _t5_bucket_np(dist):
    max_exact = T5_NUM_BUCKETS // 2
    d = np.maximum(dist, 1).astype(np.float32)
    ratio = (np.log(d / np.float32(max_exact)) / np.float32(math.log(T5_MAX_DISTANCE / max_exact))
             * np.float32(T5_NUM_BUCKETS - max_exact))
    large = np.minimum(max_exact + ratio.astype(np.int32), T5_NUM_BUCKETS - 1)
    return np.where(dist < max_exact, dist, large).astype(np.int32)


MOBA_Q_BLOCKS = 2


def _moba_query_columns(q, kmean, far_row, first_block, blk, nb):
    nbp, tq = kmean.shape[0], q.shape[0]
    km_hi = kmean.astype(jnp.bfloat16)
    km_lo = (kmean - km_hi.astype(jnp.float32)).astype(jnp.bfloat16)
    gate = (lax.dot_general(km_hi, q, _NT_DIMS, preferred_element_type=jnp.float32)
            + lax.dot_general(km_lo, q, _NT_DIMS, preferred_element_type=jnp.float32))
    n_idx = lax.broadcasted_iota(jnp.int32, (nbp, tq), 0)
    own = first_block + lax.broadcasted_iota(jnp.int32, (nbp, tq), 1) // blk
    past = n_idx < own
    gate = jnp.where(past, gate, NEG_BIG)
    beaten = jnp.zeros((nbp, tq), jnp.int32)
    for n in range(nb):
        g_n = gate[n:n + 1, :]
        lower_index_wins_ties = jnp.where(n_idx > n, 1, 0)
        beaten = beaten + jnp.where(g_n > gate, 1, 0) + jnp.where(g_n == gate, lower_index_wins_ties, 0)
    selected = past & (beaten < MOBA_TOPK)
    far_bias = jnp.broadcast_to(far_row, (nbp, tq))
    term = jnp.where(selected, jnp.where(n_idx == own - 1, 0.0, far_bias), NEG_BIG)
    term = jnp.where(n_idx == own, 0.0, term)
    term_hi = term.astype(jnp.bfloat16).astype(jnp.float32)
    term_lo = jnp.where(selected, term - term_hi, 0.0)
    pieces = [term_hi[:nb], term_lo[:nb]]
    if 2 * nb < LANES:
        pieces.append(jnp.zeros((LANES - 2 * nb, tq), jnp.float32))
    return jnp.concatenate(pieces, axis=0).T.astype(q.dtype)


def _moba_kernel(q_ref, k_ref, v_ref, rbias_ref, far_ref, z_ref, o_ref,
                 kaug_ref, vext_ref, kmean_ref, t5_ref, sa_ref, sb_ref, m_ref, acc_ref, *, nb, blk):
    qi = pl.program_id(2)
    tq = q_ref.shape[1]
    dh = HEAD_DIM
    G = kaug_ref.shape[0]
    first_block = qi * (tq // blk)

    def rows(j):
        return pl.ds(pl.multiple_of(j * tq, tq), tq)

    @pl.when(qi == 0)
    def _():
        for g in range(G):
            kmean_ref[g] = jnp.zeros(kmean_ref.shape[1:], kmean_ref.dtype)
            rb = jnp.broadcast_to(rbias_ref[g], (tq, tq))
            t5_ref[g] = pltpu.roll(rb, 1, 1, stride=1, stride_axis=0)

    key_block = first_block + lax.broadcasted_iota(jnp.int32, (tq, LANES), 0) // blk
    lane = lax.broadcasted_iota(jnp.int32, (1, LANES), 1)
    lane_block = jnp.where(lane < nb, lane, jnp.where(lane < 2 * nb, lane - nb, -1))
    hot = jnp.where(jnp.broadcast_to(lane_block, (tq, LANES)) == key_block, 1.0, 0.0).astype(kaug_ref.dtype)
    for g in range(G):
        k = k_ref[0, :, g * dh:(g + 1) * dh]
        kaug_ref[g, rows(qi), :dh] = k
        kaug_ref[g, rows(qi), dh:] = hot
        vext_ref[g, rows(qi), :dh] = v_ref[0, :, g * dh:(g + 1) * dh]
        vext_ref[g, rows(qi), dh:] = jnp.ones((tq, dh), vext_ref.dtype)
        for n in range(tq // blk):
            kb = k[n * blk:(n + 1) * blk, :].astype(jnp.float32)
            kmean_ref[g, pl.ds(first_block + n, 1), :] = jnp.sum(kb, axis=0, keepdims=True) * (1.0 / blk)

    assert tq == 2 * blk

    def add_previous_block_bias(s, g):
        width = s.shape[1]
        return jnp.concatenate([s[:, :width - blk], s[:, width - blk:] + t5_ref[g, tq - blk:, :blk]], axis=1)

    def add_diagonal_bias(s, row0, g, lead):
        rows, width = s.shape
        own = s[:, lead * tq:] + t5_ref[g, row0:row0 + rows, :width - lead * tq]
        if lead == 0:
            return own
        front = add_previous_block_bias(s[:, :tq], g) if row0 == 0 else s[:, :tq]
        return jnp.concatenate([front, own], axis=1)

    tiles = []
    for g in range(G):
        q = q_ref[0, :, g * dh:(g + 1) * dh]
        q_extra = _moba_query_columns(q, kmean_ref[g], far_ref[g], first_block, blk, nb)
        q_aug = jnp.concatenate([q, q_extra], axis=1)
        _init_softmax_state(m_ref.at[g], acc_ref.at[g])

        def wide_rows(f):
            return pl.ds(pl.multiple_of(2 * f * tq, tq), 2 * tq)

        def logits(f, g=g, q_aug=q_aug):
            return lax.dot_general(q_aug, kaug_ref[g, wide_rows(f), :], _NT_DIMS,
                                   preferred_element_type=jnp.float32)

        def values(f, g=g):
            return vext_ref[g, wide_rows(f), :]

        def last_rows(n_tiles, skip=0):
            return pl.ds(pl.multiple_of((qi + 1 - skip - n_tiles) * tq, tq), n_tiles * tq)

        alone = _diagonal_tile(q_aug, lambda g=g: kaug_ref[g, last_rows(1), :],
                               lambda g=g: vext_ref[g, last_rows(1), :],
                               add_bias=functools.partial(add_diagonal_bias, g=g, lead=0))
        paired = _diagonal_tile(q_aug, lambda g=g: kaug_ref[g, last_rows(2), :],
                                lambda g=g: vext_ref[g, last_rows(2), :],
                                add_bias=functools.partial(add_diagonal_bias, g=g, lead=1), lead=1)

        def previous_pair_logits(g=g, q_aug=q_aug):
            s_prev = lax.dot_general(q_aug, kaug_ref[g, last_rows(2, skip=1), :], _NT_DIMS,
                                     preferred_element_type=jnp.float32)
            return [s_prev[:blk], s_prev[blk:]]

        previous_pair = _Tile(previous_pair_logits,
                              [_Part(0, blk, 2 * tq, functools.partial(add_previous_block_bias, g=g)),
                               _Part(blk, tq - blk, 2 * tq, lambda s_rows: s_rows)],
                              lambda g=g: vext_ref[g, last_rows(2, skip=1), :])
        tiles.append((alone, paired, previous_pair, logits, values, (sa_ref.at[g], sb_ref.at[g]),
                      m_ref.at[g], acc_ref.at[g]))

    def heads_with(*specials):
        return [_HeadTiles([t[i] for i in specials], *t[3:]) for t in tiles]

    @pl.when(qi == 0)
    def _():
        _pipelined_tiles(heads_with(0), 0)

    @pl.when(qi == 1)
    def _():
        _pipelined_tiles(heads_with(1), 0)

    @pl.when((qi >= 2) & (qi % 2 == 0))
    def _():
        _pipelined_tiles(heads_with(0, 2), (qi - 2) // 2)

    @pl.when((qi >= 3) & (qi % 2 == 1))
    def _():
        _pipelined_tiles(heads_with(1), (qi - 1) // 2)

    for g in range(G):
        o_ref[0, :, g * dh:(g + 1) * dh] = _finish_attention(acc_ref.at[g], z_ref[0, :, g * dh:(g + 1) * dh],
                                                             o_ref.dtype)


def _moba_attention(qkv, rel_bias_table, sz, H, col0, zcol0):
    B, S, _ = qkv.shape
    blk = MOBA_BLOCK
    tq = MOBA_Q_BLOCKS * blk
    assert MOBA_Q_BLOCKS == 2 and S % tq == 0
    nb = S // blk
    nbp = -(-nb // 8) * 8
    assert 2 * nb <= LANES
    Dh = HEAD_DIM
    G = ATTN_HEADS_PER_STEP
    assert H % G == 0 and col0 % G == 0 and zcol0 % G == 0
    W = G * Dh
    table_t = rel_bias_table.T.astype(jnp.float32) * LOG2E
    rbias = table_t[:, _t5_bucket_np(np.arange(tq)[::-1])][:, None, :]
    far_bucket = int(_t5_bucket_np(np.array([blk + 1]))[0])
    assert (_t5_bucket_np(np.arange(blk + 1, S + blk)) == far_bucket).all()
    far = jnp.broadcast_to(table_t[:, far_bucket][:, None, None], (H, 1, tq))
    return pl.pallas_call(
        functools.partial(_moba_kernel, nb=nb, blk=blk),
        grid=(B, H // G, S // tq),
        in_specs=[pl.BlockSpec((1, tq, W), lambda b, h, i: (b, i, col0 // G + h)),
                  pl.BlockSpec((1, tq, W), lambda b, h, i: (b, i, (col0 + H) // G + h)),
                  pl.BlockSpec((1, tq, W), lambda b, h, i: (b, i, (col0 + 2 * H) // G + h)),
                  pl.BlockSpec((G, 1, tq), lambda b, h, i: (h, 0, 0)),
                  pl.BlockSpec((G, 1, tq), lambda b, h, i: (h, 0, 0)),
                  pl.BlockSpec((1, tq, W), lambda b, h, i: (b, i, zcol0 // G + h))],
        out_specs=pl.BlockSpec((1, tq, W), lambda b, h, i: (b, i, h)),
        out_shape=jax.ShapeDtypeStruct((B, S, H * Dh), jnp.bfloat16),
        scratch_shapes=[pltpu.VMEM((G, S, Dh + LANES), jnp.bfloat16), pltpu.VMEM((G, S, 2 * Dh), jnp.bfloat16),
                        pltpu.VMEM((G, nbp, Dh), jnp.float32), pltpu.VMEM((G, tq, tq), jnp.float32),
                        pltpu.VMEM((G, tq, 2 * tq), jnp.float32), pltpu.VMEM((G, tq, 2 * tq), jnp.float32),
                        pltpu.VMEM((G, tq, LANES), jnp.float32), pltpu.VMEM((G, tq, 2 * Dh), jnp.float32)],
        compiler_params=_params("arbitrary", "arbitrary", "arbitrary"),
        name="moba_attention",
    )(qkv, qkv, qkv, rbias, far, sz)


def _branch_merge_kernel(ya_ref, yf_ref, wa_ref, wf_ref, ga_ref, gf_ref, o_ref):
    ua = jnp.dot(ya_ref[...], wa_ref[0], preferred_element_type=jnp.float32)
    uf = jnp.dot(yf_ref[...], wf_ref[0], preferred_element_type=jnp.float32)
    o_ref[...] = (ga_ref[...] * ua + gf_ref[...] * uf).astype(o_ref.dtype)


def _branch_merge(ya, yf, w, gates, tm=1024, tn=512):
    M, W = ya.shape
    D = w.shape[2]
    tm, tn = min(tm, M), min(tn, D)
    nj = D // tn
    return pl.pallas_call(
        _branch_merge_kernel,
        grid=(M // tm, nj),
        in_specs=[pl.BlockSpec((tm, W), lambda i, j: (i, 0)),
                  pl.BlockSpec((tm, W), lambda i, j: (i, 0)),
                  pl.BlockSpec((1, W, tn), lambda i, j: (0, 0, j)),
                  pl.BlockSpec((1, W, tn), lambda i, j: (1, 0, j)),
                  pl.BlockSpec((tm, tn), lambda i, j: (i, j)),
                  pl.BlockSpec((tm, tn), lambda i, j: (i, nj + j))],
        out_specs=pl.BlockSpec((tm, tn), lambda i, j: (i, j)),
        out_shape=jax.ShapeDtypeStruct((M, D), jnp.bfloat16),
        compiler_params=_params("parallel", "parallel"),
        name="branch_merge",
    )(ya, yf, w, w, gates, gates)


LN_ROW_CHUNK = 32


def _out_layernorm_kernel(h_ref, w_ref, x_ref, g_ref, b_ref, o_ref, pre_even_ref, pre_odd_ref):
    i, j = pl.program_id(0), pl.program_id(1)
    tn = w_ref.shape[1]
    chunk = o_ref.shape[0]

    @pl.when((i == 0) & (j == 0))
    def _():
        pre_odd_ref[...] = jnp.zeros_like(pre_odd_ref)

    def step(cur_ref, prev_ref):
        rows = min(LN_ROW_CHUNK, chunk)
        for c in range(chunk // rows):
            r = prev_ref[pl.ds(pl.multiple_of(j * chunk + c * rows, rows), rows), :]
            mu = jnp.mean(r, axis=-1, keepdims=True)
            d = r - mu
            var = jnp.mean(d * d, axis=-1, keepdims=True)
            o_ref[c * rows:(c + 1) * rows, :] = d * lax.rsqrt(var + LN_EPS) * g_ref[...] + b_ref[...]
        out = jnp.dot(h_ref[...], w_ref[...], preferred_element_type=jnp.float32)
        cur_ref[:, pl.ds(pl.multiple_of(j * tn, tn), tn)] = DEEPNORM_ALPHA * x_ref[...] + out

    @pl.when(i % 2 == 0)
    def _():
        step(pre_even_ref, pre_odd_ref)

    @pl.when(i % 2 == 1)
    def _():
        step(pre_odd_ref, pre_even_ref)


def _out_layernorm(h, w, x, gain, bias, tm=512, tn=1024):
    M, D = x.shape
    tm, tn = min(tm, M), min(tn, D)
    n_i, n_j = M // tm, D // tn
    chunk = tm // n_j
    assert chunk % 8 == 0
    return pl.pallas_call(
        _out_layernorm_kernel,
        grid=(n_i + 1, n_j),
        in_specs=[pl.BlockSpec((tm, D), lambda i, j: (jnp.minimum(i, n_i - 1), 0)),
                  pl.BlockSpec((D, tn), lambda i, j: (0, j)),
                  pl.BlockSpec((tm, tn), lambda i, j: (jnp.minimum(i, n_i - 1), j)),
                  pl.BlockSpec((1, D), lambda i, j: (0, 0)),
                  pl.BlockSpec((1, D), lambda i, j: (0, 0))],
        out_specs=pl.BlockSpec((chunk, D), lambda i, j: (jnp.where(i == 0, 0, (i - 1) * n_j + j), 0)),
        out_shape=jax.ShapeDtypeStruct((M, D), jnp.float32),
        scratch_shapes=[pltpu.VMEM((tm, D), jnp.float32), pltpu.VMEM((tm, D), jnp.float32)],
        compiler_params=_params("arbitrary", "arbitrary"),
        name="out_layernorm",
    )(h, w, x, gain, bias)


def kernel(x, w_in, b_forget, b_gate, rel_bias_table, w_branch, w_out, ln_gain, ln_bias):
    B, S, D = x.shape
    Ha, Hf = MOBA_HEADS, FOX_HEADS
    Wa, Wf = Ha * HEAD_DIM, Hf * HEAD_DIM
    bf16, f32 = jnp.bfloat16, jnp.float32
    depth = w_in.shape[0]
    for layer in range(depth):
        w_t = jnp.swapaxes(w_in[layer], 0, 1).astype(bf16)
        n_attn = 4 * Wa + 4 * Wf
        w_f = jnp.pad(w_t[n_attn:n_attn + Hf], ((0, LANES - Hf), (0, 0)))
        b_f = jnp.pad(b_forget[layer].astype(f32), (0, LANES - Hf))[None, :]
        x2 = x.reshape(B * S, D).astype(f32)
        xb, cx = _forget_bias_columns(x2, w_f, b_f, Hf, B)

        scale = HEAD_DIM ** -0.5 * LOG2E
        qkv_scale = jnp.concatenate([jnp.full((Wa,), scale, f32), jnp.ones((2 * Wa,), f32),
                                     jnp.full((Wf,), scale, f32), jnp.ones((2 * Wf,), f32)])[None, :]
        tn = PROJ_TN
        assert Wa % tn == 0 and Wf % tn == 0
        a_blocks, f_blocks = Wa // tn, Wf // tn
        qkv = _matmul(xb, w_t, qkv_scale, jnp.zeros_like(qkv_scale), bf16, "none",
                      lambda j: jnp.where(j < 3 * a_blocks, j, j + a_blocks), tn=tn).reshape(B, S, -1)
        ones_z = jnp.ones((1, Wa + Wf), f32)
        sz = _matmul(xb, w_t, ones_z, jnp.zeros_like(ones_z), f32, "silu",
                     lambda j: jnp.where(j < a_blocks, j + 3 * a_blocks, j + 3 * a_blocks + 3 * f_blocks),
                     tn=tn).reshape(B, S, -1)
        gates = _matmul(xb, w_t, jnp.ones((1, 2 * D), f32), b_gate[layer].reshape(1, 2 * D).astype(f32),
                        f32, "sigmoid", w_row_start=n_attn + Hf, tn=tn)

        ya = _moba_attention(qkv, rel_bias_table, sz, Ha, 0, 0)
        yf = _fox_attention(qkv, cx, sz, Hf, 3 * Ha, Ha)

        assert Wa == Wf
        merged = _branch_merge(ya.reshape(B * S, Wa), yf.reshape(B * S, Wf), w_branch[layer].astype(bf16), gates)
        y = _out_layernorm(merged, w_out[layer].astype(bf16), x2,
                           ln_gain[layer].reshape(1, D).astype(f32), ln_bias[layer].reshape(1, D).astype(f32))
        x = y.reshape(B, S, D).astype(x.dtype)
    return x
```

```python
import functools
import math
from typing import Any, Callable, NamedTuple, Sequence

import jax
import jax.numpy as jnp
import numpy as np
from jax import lax
from jax.experimental import pallas as pl
from jax.experimental.pallas import tpu as pltpu

HEAD_DIM = 128
LANES = 128
MOBA_HEADS = 16
FOX_HEADS = 16
MOBA_BLOCK = 256
MOBA_TOPK = 3
T5_NUM_BUCKETS = 32
T5_MAX_DISTANCE = 128
LN_EPS = 1e-5
DEPTH = 1
DEEPNORM_ALPHA = (2.0 * DEPTH) ** 0.25
LOG2E = math.log2(math.e)

V7X_VMEM_LIMIT_BYTES = 56 * 1024 * 1024
PROJ_TN = 1024
NEG_BIG = -1e30

_NT_DIMS = (((1,), (1,)), ((), ()))


def _params(*semantics):
    return pltpu.CompilerParams(dimension_semantics=semantics,
                                vmem_limit_bytes=V7X_VMEM_LIMIT_BYTES)


def _split3_bf16(a):
    hi = a.astype(jnp.bfloat16).astype(jnp.float32)
    r = a - hi
    mid = r.astype(jnp.bfloat16).astype(jnp.float32)
    lo = (r - mid).astype(jnp.bfloat16).astype(jnp.float32)
    return hi, mid, lo


def _matmul_kernel(x_ref, wt_ref, s_ref, b_ref, o_ref, *, act):
    acc = lax.dot_general(x_ref[...], wt_ref[...], _NT_DIMS, preferred_element_type=jnp.float32)
    acc = acc * s_ref[...] + b_ref[...]
    if act == "silu":
        acc = acc * jax.nn.sigmoid(acc)
    elif act == "sigmoid":
        acc = jax.nn.sigmoid(acc)
    o_ref[...] = acc.astype(o_ref.dtype)


def _matmul(x, w_t, col_scale, col_bias, out_dtype, act, w_row_block=lambda j: j, w_row_start=None,
            tm=1024, tn=1024):
    M, K = x.shape
    N = col_scale.shape[1]
    tm, tn = min(tm, M), min(tn, N)
    if w_row_start is None:
        w_spec = pl.BlockSpec((tn, K), lambda i, j: (w_row_block(j), 0))
    else:
        align = math.gcd(w_row_start, tn)
        w_spec = pl.BlockSpec((pl.Element(tn), pl.Element(K)),
                              lambda i, j: (pl.multiple_of(w_row_start + j * tn, align), 0))
    return pl.pallas_call(
        functools.partial(_matmul_kernel, act=act),
        grid=(M // tm, N // tn),
        in_specs=[pl.BlockSpec((tm, K), lambda i, j: (i, 0)),
                  w_spec,
                  pl.BlockSpec((1, tn), lambda i, j: (0, j)),
                  pl.BlockSpec((1, tn), lambda i, j: (0, j))],
        out_specs=pl.BlockSpec((tm, tn), lambda i, j: (i, j)),
        out_shape=jax.ShapeDtypeStruct((M, N), out_dtype),
        compiler_params=_params("parallel", "parallel"),
        name="proj_" + act,
    )(x, w_t, col_scale, col_bias)


def _forget_bias_kernel(x_ref, w_ref, b_ref, xb_ref, o_ref, carry_ref, *, heads):
    @pl.when(pl.program_id(1) == 0)
    def _():
        carry_ref[...] = jnp.zeros_like(carry_ref)

    ts = x_ref.shape[0]
    xb = x_ref[...].astype(xb_ref.dtype)
    xb_ref[...] = xb
    z = lax.dot_general(xb, w_ref[...], _NT_DIMS, preferred_element_type=jnp.float32) + b_ref[...]
    log_f = jnp.minimum(z, 0.0) - jnp.log1p(jnp.exp(-jnp.abs(z)))
    row = lax.broadcasted_iota(jnp.int32, (ts, ts), 0)
    col = lax.broadcasted_iota(jnp.int32, (ts, ts), 1)
    lower = jnp.where(col <= row, 1.0, 0.0).astype(jnp.bfloat16)
    c = carry_ref[...]
    for part in _split3_bf16(log_f):
        c = c + jnp.dot(lower, part.astype(jnp.bfloat16), preferred_element_type=jnp.float32)
    carry_ref[...] = c[ts - 1:ts, :]
    hi, mid, lo = _split3_bf16(c * (-LOG2E))
    lane = lax.broadcasted_iota(jnp.int32, (ts, LANES), 1)
    cols = jnp.where(lane < heads, hi,
                     jnp.where(lane < 2 * heads, pltpu.roll(mid, heads, 1),
                               jnp.where(lane < 3 * heads, pltpu.roll(lo, 2 * heads, 1), 0.0)))
    o_ref[0] = cols.astype(o_ref.dtype)


def _forget_bias_columns(x, w_f, b_forget, heads, batch, ts=512):
    M, D = x.shape
    S = M // batch
    assert 3 * heads <= LANES
    ts = min(ts, S)
    steps = S // ts
    return pl.pallas_call(
        functools.partial(_forget_bias_kernel, heads=heads),
        grid=(batch, steps),
        in_specs=[pl.BlockSpec((ts, D), lambda b, i: (b * steps + i, 0)),
                  pl.BlockSpec((LANES, D), lambda b, i: (0, 0)),
                  pl.BlockSpec((1, LANES), lambda b, i: (0, 0))],
        out_specs=[pl.BlockSpec((ts, D), lambda b, i: (b * steps + i, 0)),
                   pl.BlockSpec((1, ts, LANES), lambda b, i: (b, i, 0))],
        out_shape=[jax.ShapeDtypeStruct((M, D), jnp.bfloat16),
                   jax.ShapeDtypeStruct((batch, S, LANES), jnp.bfloat16)],
        scratch_shapes=[pltpu.VMEM((1, LANES), jnp.float32)],
        compiler_params=_params("parallel", "arbitrary"),
        name="forget_bias_columns",
    )(x, w_f, b_forget)


ATTN_HEADS_PER_STEP = 4


def _softmax_tile(s, v_ext, m_ref, acc_ref):
    tk = s.shape[1]
    chunks = [s[:, c:c + LANES] for c in range(0, tk, LANES)]
    mx = functools.reduce(jnp.maximum, chunks)
    m_prev = m_ref[...]
    m_new = jnp.maximum(m_prev, jnp.max(mx, axis=-1, keepdims=True))
    alpha = jnp.exp2(m_prev - m_new)
    p = jnp.concatenate([jnp.exp2(ch - m_new) for ch in chunks], axis=1).astype(v_ext.dtype)
    pv = jnp.dot(p, v_ext, preferred_element_type=jnp.float32)
    acc_ref[...] = acc_ref[...] * jnp.concatenate([alpha, alpha], axis=1) + pv
    m_ref[...] = m_new


class _Part(NamedTuple):
    row0: int
    rows: int
    width: int
    post: Callable[[Any], Any]


class _Tile(NamedTuple):
    qk: Callable[[], Sequence[Any]]
    parts: Sequence[_Part]
    values: Callable[[], Any]


class _HeadTiles(NamedTuple):
    specials: Sequence[_Tile]
    qk: Callable[[Any], Any]
    values: Callable[[Any], Any]
    s_refs: Sequence[Any]
    m_ref: Any
    acc_ref: Any


def _pipelined_tiles(heads, n_rest):
    def fill(buf, logits_of):
        for hd in heads:
            s = logits_of(hd)
            hd.s_refs[buf][:, :s.shape[1]] = s

    def consume(buf, values_of):
        for hd in heads:
            v_ext = values_of(hd)
            _softmax_tile(hd.s_refs[buf][:, :v_ext.shape[0]], v_ext, hd.m_ref, hd.acc_ref)

    def fill_special(buf, t):
        for hd in heads:
            tile = hd.specials[t]
            for part, s in zip(tile.parts, tile.qk()):
                hd.s_refs[buf][part.row0:part.row0 + part.rows, :part.width] = s

    def consume_special(buf, t):
        for hd in heads:
            tile = hd.specials[t]
            v_ext = tile.values()
            for part in tile.parts:
                rows = pl.ds(part.row0, part.rows)
                s = hd.s_refs[buf][part.row0:part.row0 + part.rows, :part.width]
                _softmax_tile(part.post(s), v_ext[:part.width], hd.m_ref.at[rows], hd.acc_ref.at[rows])

    n_special = len(heads[0].specials)
    no_uniform_tiles = isinstance(n_rest, int) and n_rest == 0
    fill_special(0, 0)
    for t in range(n_special):
        if t + 1 < n_special:
            fill_special((t + 1) % 2, t + 1)
        elif not no_uniform_tiles:
            fill((t + 1) % 2, lambda hd: hd.qk(0))
        consume_special(t % 2, t)
    if no_uniform_tiles:
        return
    cur, nxt = n_special % 2, (n_special + 1) % 2

    def two_tiles(i, carry):
        fill(nxt, lambda hd: hd.qk(2 * i + 1))
        consume(cur, lambda hd: hd.values(2 * i))
        fill(cur, lambda hd: hd.qk(jnp.minimum(2 * i + 2, n_rest - 1)))
        consume(nxt, lambda hd: hd.values(2 * i + 1))
        return carry

    lax.fori_loop(0, n_rest // 2, two_tiles, 0)

    @pl.when(n_rest % 2 == 1)
    def _():
        consume(cur, lambda hd: hd.values(n_rest - 1))


def _diagonal_tile(q_aug, key_tile, values, add_bias=None, lead=0):
    tq = q_aug.shape[0]
    half = tq // 2
    shift = lead * tq

    def masked(row0, rows, width):
        def post(s):
            row = lax.broadcasted_iota(jnp.int32, (rows, width), 0) + (row0 + shift)
            col = lax.broadcasted_iota(jnp.int32, (rows, width), 1)
            if add_bias is not None:
                s = add_bias(s, row0)
            return jnp.where(col <= row, s, NEG_BIG)
        return _Part(row0, rows, width, post)

    def qk():
        k = key_tile()
        return [lax.dot_general(q_aug[:half], k[:shift + half], _NT_DIMS, preferred_element_type=jnp.float32),
                lax.dot_general(q_aug[half:], k, _NT_DIMS, preferred_element_type=jnp.float32)]

    return _Tile(qk, [masked(0, half, shift + half), masked(half, tq - half, shift + tq)], values)


def _init_softmax_state(m_ref, acc_ref):
    m_ref[...] = jnp.full_like(m_ref, NEG_BIG)
    acc_ref[...] = jnp.zeros_like(acc_ref)


def _finish_attention(acc_ref, z, dtype):
    dh = z.shape[1]
    return (acc_ref[:, :dh] / acc_ref[:, dh:] * z).astype(dtype)


def _fox_kernel(q_ref, k_ref, cx_ref, v_ref, z_ref, o_ref, kaug_ref, vext_ref, sa_ref, sb_ref, m_ref, acc_ref,
                *, heads):
    qi = pl.program_id(2)
    tq = q_ref.shape[1]
    dh = HEAD_DIM
    G = kaug_ref.shape[0]

    def rows(j):
        return pl.ds(pl.multiple_of(j * tq, tq), tq)

    for g in range(G):
        kaug_ref[g, rows(qi), :dh] = k_ref[0, :, g * dh:(g + 1) * dh]
        kaug_ref[g, rows(qi), dh:] = cx_ref[0]
        vext_ref[g, rows(qi), :dh] = v_ref[0, :, g * dh:(g + 1) * dh]
        vext_ref[g, rows(qi), dh:] = jnp.ones((tq, dh), vext_ref.dtype)

    lane = lax.broadcasted_iota(jnp.int32, (1, LANES), 1)
    tiles = []
    for g in range(G):
        h = pl.program_id(1) * G + g
        pick = (lane == h) | (lane == heads + h) | (lane == 2 * heads + h)
        ones_at_head_terms = jnp.broadcast_to(jnp.where(pick, 1.0, 0.0), (tq, LANES)).astype(q_ref.dtype)
        q_aug = jnp.concatenate([q_ref[0, :, g * dh:(g + 1) * dh], ones_at_head_terms], axis=1)
        _init_softmax_state(m_ref.at[g], acc_ref.at[g])

        def wide_rows(f):
            return pl.ds(pl.multiple_of(2 * f * tq, tq), 2 * tq)

        def logits(f, g=g, q_aug=q_aug):
            return lax.dot_general(q_aug, kaug_ref[g, wide_rows(f), :], _NT_DIMS,
                                   preferred_element_type=jnp.float32)

        def values(f, g=g):
            return vext_ref[g, wide_rows(f), :]

        def last_rows(n_tiles):
            return pl.ds(pl.multiple_of((qi + 1 - n_tiles) * tq, tq), n_tiles * tq)

        alone = _diagonal_tile(q_aug, lambda g=g: kaug_ref[g, last_rows(1), :],
                               lambda g=g: vext_ref[g, last_rows(1), :])
        paired = _diagonal_tile(q_aug, lambda g=g: kaug_ref[g, last_rows(2), :],
                                lambda g=g: vext_ref[g, last_rows(2), :], lead=1)
        tiles.append((alone, paired, logits, values, (sa_ref.at[g], sb_ref.at[g]), m_ref.at[g], acc_ref.at[g]))

    @pl.when(qi == 0)
    def _():
        _pipelined_tiles([_HeadTiles([t[0]], *t[2:]) for t in tiles], 0)

    @pl.when(qi == 1)
    def _():
        _pipelined_tiles([_HeadTiles([t[1]], *t[2:]) for t in tiles], 0)

    @pl.when((qi >= 2) & (qi % 2 == 0))
    def _():
        _pipelined_tiles([_HeadTiles([t[0]], *t[2:]) for t in tiles], qi // 2)

    @pl.when((qi >= 3) & (qi % 2 == 1))
    def _():
        _pipelined_tiles([_HeadTiles([t[1]], *t[2:]) for t in tiles], (qi - 1) // 2)

    for g in range(G):
        o_ref[0, :, g * dh:(g + 1) * dh] = _finish_attention(acc_ref.at[g], z_ref[0, :, g * dh:(g + 1) * dh],
                                                             o_ref.dtype)


def _fox_attention(qkv, cx, sz, H, col0, zcol0, tq=512):
    B, S, _ = qkv.shape
    tq = min(tq, S)
    Dh = HEAD_DIM
    G = ATTN_HEADS_PER_STEP
    assert H % G == 0 and col0 % G == 0 and zcol0 % G == 0
    W = G * Dh
    return pl.pallas_call(
        functools.partial(_fox_kernel, heads=H),
        grid=(B, H // G, S // tq),
        in_specs=[pl.BlockSpec((1, tq, W), lambda b, h, i: (b, i, col0 // G + h)),
                  pl.BlockSpec((1, tq, W), lambda b, h, i: (b, i, (col0 + H) // G + h)),
                  pl.BlockSpec((1, tq, LANES), lambda b, h, i: (b, i, 0)),
                  pl.BlockSpec((1, tq, W), lambda b, h, i: (b, i, (col0 + 2 * H) // G + h)),
                  pl.BlockSpec((1, tq, W), lambda b, h, i: (b, i, zcol0 // G + h))],
        out_specs=pl.BlockSpec((1, tq, W), lambda b, h, i: (b, i, h)),
        out_shape=jax.ShapeDtypeStruct((B, S, H * Dh), jnp.bfloat16),
        scratch_shapes=[pltpu.VMEM((G, S, Dh + LANES), jnp.bfloat16), pltpu.VMEM((G, S, 2 * Dh), jnp.bfloat16),
                        pltpu.VMEM((G, tq, 2 * tq), jnp.float32), pltpu.VMEM((G, tq, 2 * tq), jnp.float32),
                        pltpu.VMEM((G, tq, LANES), jnp.float32), pltpu.VMEM((G, tq, 2 * Dh), jnp.float32)],
        compiler_params=_params("arbitrary", "arbitrary", "arbitrary"),
        name="fox_attention",
    )(qkv, qkv, cx, qkv, sz)


def _t5_bucket_np(dist):
    max_exact = T5_NUM_BUCKETS // 2
    d = np.maximum(dist, 1).astype(np.float32)
    ratio = (np.log(d / np.float32(max_exact)) / np.float32(math.log(T5_MAX_DISTANCE / max_exact))
             * np.float32(T5_NUM_BUCKETS - max_exact))
    large = np.minimum(max_exact + ratio.astype(np.int32), T5_NUM_BUCKETS - 1)
    return np.where(dist < max_exact, dist, large).astype(np.int32)


MOBA_Q_BLOCKS = 2


def _moba_query_columns(q, kmean, far_row, first_block, blk, nb):
    nbp, tq = kmean.shape[0], q.shape[0]
    km_hi = kmean.astype(jnp.bfloat16)
    km_lo = (kmean - km_hi.astype(jnp.float32)).astype(jnp.bfloat16)
    gate = (lax.dot_general(km_hi, q, _NT_DIMS, preferred_element_type=jnp.float32)
            + lax.dot_general(km_lo, q, _NT_DIMS, preferred_element_type=jnp.float32))
    n_idx = lax.broadcasted_iota(jnp.int32, (nbp, tq), 0)
    own = first_block + lax.broadcasted_iota(jnp.int32, (nbp, tq), 1) // blk
    past = n_idx < own
    gate = jnp.where(past, gate, NEG_BIG)
    beaten = jnp.zeros((nbp, tq), jnp.int32)
    for n in range(nb):
        g_n = gate[n:n + 1, :]
        lower_index_wins_ties = jnp.where(n_idx > n, 1, 0)
        beaten = beaten + jnp.where(g_n > gate, 1, 0) + jnp.where(g_n == gate, lower_index_wins_ties, 0)
    selected = past & (beaten < MOBA_TOPK)
    far_bias = jnp.broadcast_to(far_row, (nbp, tq))
    term = jnp.where(selected, jnp.where(n_idx == own - 1, 0.0, far_bias), NEG_BIG)
    term = jnp.where(n_idx == own, 0.0, term)
    term_hi = term.astype(jnp.bfloat16).astype(jnp.float32)
    term_lo = jnp.where(selected, term - term_hi, 0.0)
    pieces = [term_hi[:nb], term_lo[:nb]]
    if 2 * nb < LANES:
        pieces.append(jnp.zeros((LANES - 2 * nb, tq), jnp.float32))
    return jnp.concatenate(pieces, axis=0).T.astype(q.dtype)


def _moba_kernel(q_ref, k_ref, v_ref, rbias_ref, far_ref, z_ref, o_ref,
                 kaug_ref, vext_ref, kmean_ref, t5_ref, sa_ref, sb_ref, m_ref, acc_ref, *, nb, blk):
    qi = pl.program_id(2)
    tq = q_ref.shape[1]
    dh = HEAD_DIM
    G = kaug_ref.shape[0]
    first_block = qi * (tq // blk)

    def rows(j):
        return pl.ds(pl.multiple_of(j * tq, tq), tq)

    @pl.when(qi == 0)
    def _():
        for g in range(G):
            kmean_ref[g] = jnp.zeros(kmean_ref.shape[1:], kmean_ref.dtype)
            rb = jnp.broadcast_to(rbias_ref[g], (tq, tq))
            t5_ref[g] = pltpu.roll(rb, 1, 1, stride=1, stride_axis=0)

    key_block = first_block + lax.broadcasted_iota(jnp.int32, (tq, LANES), 0) // blk
    lane = lax.broadcasted_iota(jnp.int32, (1, LANES), 1)
    lane_block = jnp.where(lane < nb, lane, jnp.where(lane < 2 * nb, lane - nb, -1))
    hot = jnp.where(jnp.broadcast_to(lane_block, (tq, LANES)) == key_block, 1.0, 0.0).astype(kaug_ref.dtype)
    for g in range(G):
        k = k_ref[0, :, g * dh:(g + 1) * dh]
        kaug_ref[g, rows(qi), :dh] = k
        kaug_ref[g, rows(qi), dh:] = hot
        vext_ref[g, rows(qi), :dh] = v_ref[0, :, g * dh:(g + 1) * dh]
        vext_ref[g, rows(qi), dh:] = jnp.ones((tq, dh), vext_ref.dtype)
        for n in range(tq // blk):
            kb = k[n * blk:(n + 1) * blk, :].astype(jnp.float32)
            kmean_ref[g, pl.ds(first_block + n, 1), :] = jnp.sum(kb, axis=0, keepdims=True) * (1.0 / blk)

    assert tq == 2 * blk

    def add_previous_block_bias(s, g):
        width = s.shape[1]
        return jnp.concatenate([s[:, :width - blk], s[:, width - blk:] + t5_ref[g, tq - blk:, :blk]], axis=1)

    def add_diagonal_bias(s, row0, g, lead):
        rows, width = s.shape
        own = s[:, lead * tq:] + t5_ref[g, row0:row0 + rows, :width - lead * tq]
        if lead == 0:
            return own
        front = add_previous_block_bias(s[:, :tq], g) if row0 == 0 else s[:, :tq]
        return jnp.concatenate([front, own], axis=1)

    tiles = []
    for g in range(G):
        q = q_ref[0, :, g * dh:(g + 1) * dh]
        q_extra = _moba_query_columns(q, kmean_ref[g], far_ref[g], first_block, blk, nb)
        q_aug = jnp.concatenate([q, q_extra], axis=1)
        _init_softmax_state(m_ref.at[g], acc_ref.at[g])

        def wide_rows(f):
            return pl.ds(pl.multiple_of(2 * f * tq, tq), 2 * tq)

        def logits(f, g=g, q_aug=q_aug):
            return lax.dot_general(q_aug, kaug_ref[g, wide_rows(f), :], _NT_DIMS,
                                   preferred_element_type=jnp.float32)

        def values(f, g=g):
            return vext_ref[g, wide_rows(f), :]

        def last_rows(n_tiles, skip=0):
            return pl.ds(pl.multiple_of((qi + 1 - skip - n_tiles) * tq, tq), n_tiles * tq)

        alone = _diagonal_tile(q_aug, lambda g=g: kaug_ref[g, last_rows(1), :],
                               lambda g=g: vext_ref[g, last_rows(1), :],
                               add_bias=functools.partial(add_diagonal_bias, g=g, lead=0))
        paired = _diagonal_tile(q_aug, lambda g=g: kaug_ref[g, last_rows(2), :],
                                lambda g=g: vext_ref[g, last_rows(2), :],
                                add_bias=functools.partial(add_diagonal_bias, g=g, lead=1), lead=1)

        def previous_pair_logits(g=g, q_aug=q_aug):
            s_prev = lax.dot_general(q_aug, kaug_ref[g, last_rows(2, skip=1), :], _NT_DIMS,
                                     preferred_element_type=jnp.float32)
            return [s_prev[:blk], s_prev[blk:]]

        previous_pair = _Tile(previous_pair_logits,
                              [_Part(0, blk, 2 * tq, functools.partial(add_previous_block_bias, g=g)),
                               _Part(blk, tq - blk, 2 * tq, lambda s_rows: s_rows)],
                              lambda g=g: vext_ref[g, last_rows(2, skip=1), :])
        tiles.append((alone, paired, previous_pair, logits, values, (sa_ref.at[g], sb_ref.at[g]),
                      m_ref.at[g], acc_ref.at[g]))

    def heads_with(*specials):
        return [_HeadTiles([t[i] for i in specials], *t[3:]) for t in tiles]

    @pl.when(qi == 0)
    def _():
        _pipelined_tiles(heads_with(0), 0)

    @pl.when(qi == 1)
    def _():
        _pipelined_tiles(heads_with(1), 0)

    @pl.when((qi >= 2) & (qi % 2 == 0))
    def _():
        _pipelined_tiles(heads_with(0, 2), (qi - 2) // 2)

    @pl.when((qi >= 3) & (qi % 2 == 1))
    def _():
        _pipelined_tiles(heads_with(1), (qi - 1) // 2)

    for g in range(G):
        o_ref[0, :, g * dh:(g + 1) * dh] = _finish_attention(acc_ref.at[g], z_ref[0, :, g * dh:(g + 1) * dh],
                                                             o_ref.dtype)


def _moba_attention(qkv, rel_bias_table, sz, H, col0, zcol0):
    B, S, _ = qkv.shape
    blk = MOBA_BLOCK
    tq = MOBA_Q_BLOCKS * blk
    assert MOBA_Q_BLOCKS == 2 and S % tq == 0
    nb = S // blk
    nbp = -(-nb // 8) * 8
    assert 2 * nb <= LANES
    Dh = HEAD_DIM
    G = ATTN_HEADS_PER_STEP
    assert H % G == 0 and col0 % G == 0 and zcol0 % G == 0
    W = G * Dh
    table_t = rel_bias_table.T.astype(jnp.float32) * LOG2E
    rbias = table_t[:, _t5_bucket_np(np.arange(tq)[::-1])][:, None, :]
    far_bucket = int(_t5_bucket_np(np.array([blk + 1]))[0])
    assert (_t5_bucket_np(np.arange(blk + 1, S + blk)) == far_bucket).all()
    far = jnp.broadcast_to(table_t[:, far_bucket][:, None, None], (H, 1, tq))
    return pl.pallas_call(
        functools.partial(_moba_kernel, nb=nb, blk=blk),
        grid=(B, H // G, S // tq),
        in_specs=[pl.BlockSpec((1, tq, W), lambda b, h, i: (b, i, col0 // G + h)),
                  pl.BlockSpec((1, tq, W), lambda b, h, i: (b, i, (col0 + H) // G + h)),
                  pl.BlockSpec((1, tq, W), lambda b, h, i: (b, i, (col0 + 2 * H) // G + h)),
                  pl.BlockSpec((G, 1, tq), lambda b, h, i: (h, 0, 0)),
                  pl.BlockSpec((G, 1, tq), lambda b, h, i: (h, 0, 0)),
                  pl.BlockSpec((1, tq, W), lambda b, h, i: (b, i, zcol0 // G + h))],
        out_specs=pl.BlockSpec((1, tq, W), lambda b, h, i: (b, i, h)),
        out_shape=jax.ShapeDtypeStruct((B, S, H * Dh), jnp.bfloat16),
        scratch_shapes=[pltpu.VMEM((G, S, Dh + LANES), jnp.bfloat16), pltpu.VMEM((G, S, 2 * Dh), jnp.bfloat16),
                        pltpu.VMEM((G, nbp, Dh), jnp.float32), pltpu.VMEM((G, tq, tq), jnp.float32),
                        pltpu.VMEM((G, tq, 2 * tq), jnp.float32), pltpu.VMEM((G, tq, 2 * tq), jnp.float32),
                        pltpu.VMEM((G, tq, LANES), jnp.float32), pltpu.VMEM((G, tq, 2 * Dh), jnp.float32)],
        compiler_params=_params("arbitrary", "arbitrary", "arbitrary"),
        name="moba_attention",
    )(qkv, qkv, qkv, rbias, far, sz)


def _branch_merge_kernel(ya_hbm, yf_hbm, wa_ref, wf_ref, ga_ref, gf_ref, o_ref, ya_buf, yf_buf, sem):
    i, j = pl.program_id(0), pl.program_id(1)
    n_i = pl.num_programs(0)
    tm = ya_buf.shape[1]

    def copies(row_block, slot):
        rows = pl.ds(pl.multiple_of(row_block * tm, tm), tm)
        return (pltpu.make_async_copy(ya_hbm.at[rows, :], ya_buf.at[slot], sem.at[0, slot]),
                pltpu.make_async_copy(yf_hbm.at[rows, :], yf_buf.at[slot], sem.at[1, slot]))

    slot = i % 2

    @pl.when((i == 0) & (j == 0))
    def _():
        for c in copies(0, 0):
            c.start()

    @pl.when((j == 0) & (i + 1 < n_i))
    def _():
        for c in copies(i + 1, 1 - slot):
            c.start()

    @pl.when(j == 0)
    def _():
        for c in copies(i, slot):
            c.wait()

    ua = jnp.dot(ya_buf[slot], wa_ref[0], preferred_element_type=jnp.float32)
    uf = jnp.dot(yf_buf[slot], wf_ref[0], preferred_element_type=jnp.float32)
    o_ref[...] = (ga_ref[...] * ua + gf_ref[...] * uf).astype(o_ref.dtype)


def _branch_merge(ya, yf, w, gates, tm=1024, tn=512):
    M, W = ya.shape
    D = w.shape[2]
    tm, tn = min(tm, M), min(tn, D)
    nj = D // tn
    return pl.pallas_call(
        _branch_merge_kernel,
        grid=(M // tm, nj),
        in_specs=[pl.BlockSpec(memory_space=pl.ANY),
                  pl.BlockSpec(memory_space=pl.ANY),
                  pl.BlockSpec((1, W, tn), lambda i, j: (0, 0, j)),
                  pl.BlockSpec((1, W, tn), lambda i, j: (1, 0, j)),
                  pl.BlockSpec((tm, tn), lambda i, j: (i, j)),
                  pl.BlockSpec((tm, tn), lambda i, j: (i, nj + j))],
        out_specs=pl.BlockSpec((tm, tn), lambda i, j: (i, j)),
        out_shape=jax.ShapeDtypeStruct((M, D), jnp.bfloat16),
        scratch_shapes=[pltpu.VMEM((2, tm, W), ya.dtype), pltpu.VMEM((2, tm, W), yf.dtype),
                        pltpu.SemaphoreType.DMA((2, 2))],
        compiler_params=_params("arbitrary", "arbitrary"),
        name="branch_merge",
    )(ya, yf, w, w, gates, gates)


LN_ROW_CHUNK = 128


def _out_layernorm_kernel(h_ref, w_ref, x_ref, g_ref, b_ref, o_ref, pre_even_ref, pre_odd_ref):
    i, j = pl.program_id(0), pl.program_id(1)
    tn = w_ref.shape[1]
    chunk = o_ref.shape[0]

    @pl.when((i == 0) & (j == 0))
    def _():
        pre_odd_ref[...] = jnp.zeros_like(pre_odd_ref)

    def step(cur_ref, prev_ref):
        rows = min(LN_ROW_CHUNK, chunk)
        for c in range(chunk // rows):
            r = prev_ref[pl.ds(pl.multiple_of(j * chunk + c * rows, rows), rows), :]
            mu = jnp.mean(r, axis=-1, keepdims=True)
            d = r - mu
            var = jnp.mean(d * d, axis=-1, keepdims=True)
            o_ref[c * rows:(c + 1) * rows, :] = d * lax.rsqrt(var + LN_EPS) * g_ref[...] + b_ref[...]
        out = jnp.dot(h_ref[...], w_ref[...], preferred_element_type=jnp.float32)
        cur_ref[:, pl.ds(pl.multiple_of(j * tn, tn), tn)] = DEEPNORM_ALPHA * x_ref[...] + out

    @pl.when(i % 2 == 0)
    def _():
        step(pre_even_ref, pre_odd_ref)

    @pl.when(i % 2 == 1)
    def _():
        step(pre_odd_ref, pre_even_ref)


def _out_layernorm(h, w, x, gain, bias, tm=512, tn=1024):
    M, D = x.shape
    tm, tn = min(tm, M), min(tn, D)
    n_i, n_j = M // tm, D // tn
    chunk = tm // n_j
    assert chunk % 8 == 0
    return pl.pallas_call(
        _out_layernorm_kernel,
        grid=(n_i + 1, n_j),
        in_specs=[pl.BlockSpec((tm, D), lambda i, j: (jnp.minimum(i, n_i - 1), 0)),
                  pl.BlockSpec((D, tn), lambda i, j: (0, j)),
                  pl.BlockSpec((tm, tn), lambda i, j: (jnp.minimum(i, n_i - 1), j)),
                  pl.BlockSpec((1, D), lambda i, j: (0, 0)),
                  pl.BlockSpec((1, D), lambda i, j: (0, 0))],
        out_specs=pl.BlockSpec((chunk, D), lambda i, j: (jnp.where(i == 0, 0, (i - 1) * n_j + j), 0)),
        out_shape=jax.ShapeDtypeStruct((M, D), jnp.float32),
        scratch_shapes=[pltpu.VMEM((tm, D), jnp.float32), pltpu.VMEM((tm, D), jnp.float32)],
        compiler_params=_params("arbitrary", "arbitrary"),
        name="out_layernorm",
    )(h, w, x, gain, bias)


def kernel(x, w_in, b_forget, b_gate, rel_bias_table, w_branch, w_out, ln_gain, ln_bias):
    B, S, D = x.shape
    Ha, Hf = MOBA_HEADS, FOX_HEADS
    Wa, Wf = Ha * HEAD_DIM, Hf * HEAD_DIM
    bf16, f32 = jnp.bfloat16, jnp.float32
    depth = w_in.shape[0]
    for layer in range(depth):
        w_t = jnp.swapaxes(w_in[layer], 0, 1).astype(bf16)
        n_attn = 4 * Wa + 4 * Wf
        w_f = jnp.pad(w_t[n_attn:n_attn + Hf], ((0, LANES - Hf), (0, 0)))
        b_f = jnp.pad(b_forget[layer].astype(f32), (0, LANES - Hf))[None, :]
        x2 = x.reshape(B * S, D).astype(f32)
        xb, cx = _forget_bias_columns(x2, w_f, b_f, Hf, B)

        scale = HEAD_DIM ** -0.5 * LOG2E
        qkv_scale = jnp.concatenate([jnp.full((Wa,), scale, f32), jnp.ones((2 * Wa,), f32),
                                     jnp.full((Wf,), scale, f32), jnp.ones((2 * Wf,), f32)])[None, :]
        tn = PROJ_TN
        assert Wa % tn == 0 and Wf % tn == 0
        a_blocks, f_blocks = Wa // tn, Wf // tn
        qkv = _matmul(xb, w_t, qkv_scale, jnp.zeros_like(qkv_scale), bf16, "none",
                      lambda j: jnp.where(j < 3 * a_blocks, j, j + a_blocks), tn=tn).reshape(B, S, -1)
        ones_z = jnp.ones((1, Wa + Wf), f32)
        sz = _matmul(xb, w_t, ones_z, jnp.zeros_like(ones_z), f32, "silu",
                     lambda j: jnp.where(j < a_blocks, j + 3 * a_blocks, j + 3 * a_blocks + 3 * f_blocks),
                     tn=tn).reshape(B, S, -1)
        gates = _matmul(xb, w_t, jnp.ones((1, 2 * D), f32), b_gate[layer].reshape(1, 2 * D).astype(f32),
                        f32, "sigmoid", w_row_start=n_attn + Hf, tn=tn)

        ya = _moba_attention(qkv, rel_bias_table, sz, Ha, 0, 0)
        yf = _fox_attention(qkv, cx, sz, Hf, 3 * Ha, Ha)

        assert Wa == Wf
        merged = _branch_merge(ya.reshape(B * S, Wa), yf.reshape(B * S, Wf), w_branch[layer].astype(bf16), gates)
        y = _out_layernorm(merged, w_out[layer].astype(bf16), x2,
                           ln_gain[layer].reshape(1, D).astype(f32), ln_bias[layer].reshape(1, D).astype(f32))
        x = y.reshape(B, S, D).astype(x.dtype)
    return x
```
